```python
import jax, jax.numpy as jnp
from jax import lax
import numpy as np

D_MODEL = 2048
BATCH = 2
SEQ = 8192
DEPTH = 2

GLA_HEADS = 4
GLA_DK = D_MODEL // 2
GLA_DV = D_MODEL
GLA_HK = GLA_DK // GLA_HEADS
GLA_HV = GLA_DV // GLA_HEADS
GLA_LOWRANK = 16
GLA_GATE_TEMP = 16.0
GLA_CHUNK = 64
POOL_WIDTH = D_MODEL // 2
POOL_WINDOWS = (2, 4, 8, 16)
POOL_GROUPS = 4
POOL_GW = POOL_WIDTH // POOL_GROUPS
N_BRANCH = 2
SPLIT_SIZES = (GLA_DK, GLA_DK, GLA_DV, GLA_DV, GLA_LOWRANK, POOL_WIDTH, D_MODEL, D_MODEL)
D_IN = 2 * GLA_DK + 2 * GLA_DV + GLA_LOWRANK + POOL_WIDTH + N_BRANCH * D_MODEL
D_FF = 5632
N_EXPERTS = 8
TOP_K = 2
N_DENSE = (DEPTH + 1) // 2
N_MOE = DEPTH // 2
EPS = 1e-6

kernel_name = "hybrid_gla_pool_moe_trunk"


def rmsnorm(x, g):
    xf = x.astype(jnp.float32)
    y = xf * lax.rsqrt(jnp.mean(xf * xf, axis=-1, keepdims=True) + EPS)
    return (y * g.astype(jnp.float32)).astype(x.dtype)


def swiglu(h, w_gate, w_up, w_down):
    return (jax.nn.silu(h @ w_gate) * (h @ w_up)) @ w_down


def gla_chunked(q, k, v, log_a):
    B, S, H, dk = q.shape
    dv = v.shape[-1]
    C = GLA_CHUNK
    nc = S // C

    def chunk(t):
        return t.astype(jnp.float32).reshape(B, nc, C, H, t.shape[-1]).transpose(1, 0, 3, 2, 4)

    qc, kc, vc, ac = chunk(q), chunk(k), chunk(v), chunk(log_a)
    b = jnp.cumsum(ac, axis=3)
    b_last = b[:, :, :, -1:, :]
    q_t = qc * jnp.exp(b) * (dk ** -0.5)
    k_t = kc * jnp.exp(-b)
    k_end = kc * jnp.exp(b_last - b)
    causal = jnp.tril(jnp.ones((C, C), dtype=bool))
    att = jnp.einsum('nbhid,nbhjd->nbhij', q_t, k_t)
    att = jnp.where(causal, att, 0.0)
    o_intra = jnp.einsum('nbhij,nbhjv->nbhiv', att, vc)

    def step(state, inp):
        q_i, k_i, v_i, dec = inp
        o = jnp.einsum('bhid,bhdv->bhiv', q_i, state)
        state = state * jnp.swapaxes(dec, -1, -2) + jnp.einsum('bhjd,bhjv->bhdv', k_i, v_i)
        return state, o

    s0 = jnp.zeros((B, H, dk, dv), jnp.float32)
    _, o_inter = lax.scan(step, s0, (q_t, k_end, vc, jnp.exp(b_last)))
    o = o_intra + o_inter
    return o.transpose(1, 0, 3, 2, 4).reshape(B, S, H, dv)


def pool_mixer(p, w_pool, pool_scale):
    B, S, _ = p.shape
    pf = p.astype(jnp.float32).reshape(B, S, POOL_GROUPS, POOL_GW)
    cs = jnp.cumsum(pf, axis=1)
    pos = jnp.arange(S)
    outs = []
    for g, w in enumerate(POOL_WINDOWS):
        c = cs[:, :, g]
        c_prev = jnp.pad(c, ((0, 0), (w, 0), (0, 0)))[:, :S]
        cnt = jnp.minimum(pos + 1, w).astype(jnp.float32)[None, :, None]
        outs.append((c - c_prev) / cnt - pf[:, :, g])
    mixed = jnp.stack(outs, axis=2).astype(p.dtype)
    y = jnp.einsum('bsgc,gcd->bsgd', mixed, w_pool)
    return y.reshape(B, S, POOL_WIDTH) * pool_scale


def token_mixer(h, w_in, w_alpha, b_alpha, gla_norm, w_pool, pool_scale,
                w_branch_gla, w_branch_pool, w_out):
    B, S, _ = h.shape
    z = h @ w_in
    idx = [int(i) for i in np.cumsum(SPLIT_SIZES)[:-1]]
    q, k, v, r, a1, p, g_a, g_b = jnp.split(z, idx, axis=-1)
    log_a = jax.nn.log_sigmoid((a1 @ w_alpha + b_alpha).astype(jnp.float32)) / GLA_GATE_TEMP
    hd = lambda t, d: t.reshape(B, S, GLA_HEADS, d)
    o = gla_chunked(hd(q, GLA_HK), hd(k, GLA_HK), hd(v, GLA_HV), hd(log_a, GLA_HK))
    o = o * lax.rsqrt(jnp.mean(o * o, axis=-1, keepdims=True) + EPS) * gla_norm.astype(jnp.float32)
    o = o * jax.nn.silu(hd(r, GLA_HV).astype(jnp.float32))
    y_a = o.reshape(B, S, GLA_DV).astype(h.dtype) @ w_branch_gla
    y_b = pool_mixer(p, w_pool, pool_scale) @ w_branch_pool
    u = jax.nn.sigmoid(g_a) * y_a + jax.nn.sigmoid(g_b) * y_b
    return u.astype(h.dtype) @ w_out


def moe_ffn(h, w_router, w_gate, w_up, w_down):
    B, S, D = h.shape
    t = h.reshape(B * S, D)
    logits = (t @ w_router).astype(jnp.float32)
    top_val, top_idx = lax.top_k(logits, TOP_K)
    top_w = jax.nn.softmax(top_val, axis=-1)
    combine = jnp.sum(jax.nn.one_hot(top_idx, N_EXPERTS, dtype=jnp.float32) * top_w[..., None], axis=1)
    out = jnp.zeros((B * S, D), jnp.float32)
    for e in range(N_EXPERTS):
        out = out + combine[:, e:e + 1] * swiglu(t, w_gate[e], w_up[e], w_down[e]).astype(jnp.float32)
    return out.astype(h.dtype).reshape(B, S, D)


def setup_inputs(seed: int = 0) -> dict:
    key = jax.random.key(seed)
    ks = jax.random.split(key, 24)
    f32 = jnp.float32
    nrm = lambda k, shape, fan_in: jax.random.normal(k, shape, f32) * (fan_in ** -0.5)
    gain = lambda k, shape: 1.0 + 0.02 * jax.random.normal(k, shape, f32)
    return {
        "x": jax.random.normal(ks[0], (BATCH, SEQ, D_MODEL), f32),
        "ln_mix": gain(ks[1], (DEPTH, D_MODEL)),
        "w_in": nrm(ks[2], (DEPTH, D_MODEL, D_IN), D_MODEL),
        "w_alpha": nrm(ks[3], (DEPTH, GLA_LOWRANK, GLA_DK), GLA_LOWRANK),
        "b_alpha": 0.1 * jax.random.normal(ks[4], (DEPTH, GLA_DK), f32),
        "gla_norm": gain(ks[5], (DEPTH, GLA_HV)),
        "w_pool": nrm(ks[6], (DEPTH, POOL_GROUPS, POOL_GW, POOL_GW), POOL_GW),
        "pool_scale": 1.0 + 0.1 * jax.random.normal(ks[7], (DEPTH, POOL_WIDTH), f32),
        "w_branch_gla": nrm(ks[8], (DEPTH, GLA_DV, D_MODEL), GLA_DV),
        "w_branch_pool": nrm(ks[9], (DEPTH, POOL_WIDTH, D_MODEL), POOL_WIDTH),
        "w_out": nrm(ks[10], (DEPTH, D_MODEL, D_MODEL), D_MODEL),
        "ln_ffn": gain(ks[11], (DEPTH, D_MODEL)),
        "ffn_w_gate": nrm(ks[12], (N_DENSE, D_MODEL, D_FF), D_MODEL),
        "ffn_w_up": nrm(ks[13], (N_DENSE, D_MODEL, D_FF), D_MODEL),
        "ffn_w_down": nrm(ks[14], (N_DENSE, D_FF, D_MODEL), D_FF),
        "router_w": nrm(ks[15], (N_MOE, D_MODEL, N_EXPERTS), D_MODEL),
        "exp_w_gate": nrm(ks[16], (N_MOE, N_EXPERTS, D_MODEL, D_FF), D_MODEL),
        "exp_w_up": nrm(ks[17], (N_MOE, N_EXPERTS, D_MODEL, D_FF), D_MODEL),
        "exp_w_down": nrm(ks[18], (N_MOE, N_EXPERTS, D_FF, D_MODEL), D_FF),
        "ln_final": gain(ks[19], (D_MODEL,)),
    }


def reference(x, ln_mix, w_in, w_alpha, b_alpha, gla_norm, w_pool, pool_scale,
              w_branch_gla, w_branch_pool, w_out, ln_ffn, ffn_w_gate, ffn_w_up,
              ffn_w_down, router_w, exp_w_gate, exp_w_up, exp_w_down, ln_final):
    for l in range(DEPTH):
        h = rmsnorm(x, ln_mix[l])
        x = x + token_mixer(h, w_in[l], w_alpha[l], b_alpha[l], gla_norm[l], w_pool[l],
                            pool_scale[l], w_branch_gla[l], w_branch_pool[l], w_out[l])
        h = rmsnorm(x, ln_ffn[l])
        if l % 2 == 0:
            i = l // 2
            x = x + swiglu(h, ffn_w_gate[i], ffn_w_up[i], ffn_w_down[i])
        else:
            i = l // 2
            x = x + moe_ffn(h, router_w[i], exp_w_gate[i], exp_w_up[i], exp_w_down[i])
    return rmsnorm(x, ln_final)
```

```python
import functools

import jax
import jax.numpy as jnp
from jax import lax
from jax.experimental import pallas as pl
from jax.experimental.pallas import tpu as pltpu

F32 = jnp.float32
BF16 = jnp.bfloat16
I32 = jnp.int32

EPS = 1e-6
GLA_HEADS = 4
GLA_LOWRANK = 16
GLA_GATE_TEMP = 16.0
GLA_CHUNK = 64
POOL_WINDOWS = (2, 4, 8, 16)
POOL_GROUPS = 4
N_EXPERTS = 8

LANES = 128
VMEM_LIMIT_BYTES = 56 * 1024 * 1024

ROW_TILE = 1024
COL_TILE = 512
FFN_ROW_TILE = 512
FFN_COL_TILE = 512
SEQ_TILE = 512
TOK_TILE = 256
PLAN_TILE = 512
HALO = 16


def _cparams(*sem):
    return pltpu.CompilerParams(dimension_semantics=sem, vmem_limit_bytes=VMEM_LIMIT_BYTES)


def _dot(a, b):
    return jnp.dot(a, b, preferred_element_type=F32)


def _split_bf16(x):
    hi = x.astype(BF16)
    lo = (x - hi.astype(F32)).astype(BF16)
    return hi, lo


def _norm_inproj_kernel(x_ref, g_ref, w_ref, wa_ref, z_ref, a1_ref, h_ref, *, rows):
    @pl.when(pl.program_id(1) == 0)
    def _():
        def body(c, carry):
            r = pl.multiple_of(c * rows, rows)
            xf = x_ref[pl.ds(r, rows), :]
            ms = jnp.mean(xf * xf, axis=-1, keepdims=True)
            h_ref[pl.ds(r, rows), :] = (xf * lax.rsqrt(ms + EPS) * g_ref[...]).astype(BF16)
            return carry
        lax.fori_loop(0, x_ref.shape[0] // rows, body, 0)
        a1_ref[...] = _dot(h_ref[...], wa_ref[...])

    z_ref[...] = _dot(h_ref[...], w_ref[...]).astype(BF16)


def _norm_inproj(x, g, w, wa):
    m, d = x.shape
    n = w.shape[1]
    bm, bn = min(ROW_TILE, m), min(COL_TILE, n)
    assert m % bm == 0 and n % bn == 0
    return pl.pallas_call(
        functools.partial(_norm_inproj_kernel, rows=128),
        grid=(m // bm, n // bn),
        in_specs=[
            pl.BlockSpec((bm, d), lambda i, j: (i, 0)),
            pl.BlockSpec((1, d), lambda i, j: (0, 0)),
            pl.BlockSpec((d, bn), lambda i, j: (0, j)),
            pl.BlockSpec((d, LANES), lambda i, j: (0, 0)),
        ],
        out_specs=[
            pl.BlockSpec((bm, bn), lambda i, j: (i, j)),
            pl.BlockSpec((bm, LANES), lambda i, j: (i, 0)),
        ],
        out_shape=[jax.ShapeDtypeStruct((m, n), BF16), jax.ShapeDtypeStruct((m, LANES), F32)],
        scratch_shapes=[pltpu.VMEM((bm, d), BF16)],
        compiler_params=_cparams("parallel", "arbitrary"),
        name="norm_inproj",
    )(x, g, w, wa)


def _log_sigmoid(x):
    return jnp.minimum(x, 0.0) - jnp.log1p(jnp.exp(-jnp.abs(x)))


def _gla_kernel(q_ref, k_ref, v_ref, r_ref, a1_ref, wal_ref, bal_ref, gn_ref, o_ref, st_ref, *, chunk):
    @pl.when(pl.program_id(2) == 0)
    def _():
        st_ref[...] = jnp.zeros_like(st_ref)

    t, dk = q_ref.shape
    shift = chunk.bit_length() - 1

    a_hi, a_lo = _split_bf16(a1_ref[...])
    w_hi, w_lo = _split_bf16(wal_ref[...])
    xg = _dot(a_hi, w_hi) + _dot(a_hi, w_lo) + _dot(a_lo, w_hi) + bal_ref[...]
    la = _log_sigmoid(xg) * (1.0 / GLA_GATE_TEMP)

    row = lax.broadcasted_iota(I32, (t, t), 0)
    col = lax.broadcasted_iota(I32, (t, t), 1)
    causal = ((row >> shift) == (col >> shift)) & (row >= col)
    tri = jnp.where(causal, 1.0, 0.0).astype(BF16)
    la_hi, la_lo = _split_bf16(la)
    b = _dot(tri, la_hi) + _dot(tri, la_lo)

    q = q_ref[...].astype(F32)
    k = k_ref[...].astype(F32)
    v = v_ref[...]
    q_t = (q * jnp.exp(b) * (dk ** -0.5)).astype(BF16)
    k_t = (k * jnp.exp(-b)).astype(BF16)
    att = lax.dot_general(q_t, k_t, (((1,), (1,)), ((), ())), preferred_element_type=F32)
    att = jnp.where(causal, att, 0.0).astype(BF16)
    o = _dot(att, v)

    inter = []
    for c in range(t // chunk):
        lo, hi = c * chunk, (c + 1) * chunk
        b_last = b[hi - 1:hi, :]
        k_end = (k[lo:hi, :] * jnp.exp(b_last - b[lo:hi, :])).astype(BF16)
        st = st_ref[...]
        inter.append(lax.dot_general(q_t[lo:hi, :], st.astype(BF16), (((1,), (1,)), ((), ())),
                                     preferred_element_type=F32))
        upd = lax.dot_general(v[lo:hi, :], k_end, (((0,), (0,)), ((), ())),
                              preferred_element_type=F32)
        st_ref[...] = st * jnp.exp(b_last) + upd
    o = o + jnp.concatenate(inter, axis=0)

    o = o * lax.rsqrt(jnp.mean(o * o, axis=-1, keepdims=True) + EPS) * gn_ref[...]
    r = r_ref[...].astype(F32)
    o_ref[...] = (o * (r * jax.nn.sigmoid(r))).astype(BF16)


def _gla(z, a1, w_alpha, b_alpha, gla_norm, batch, seq, dk_total, dv_total):
    m = z.shape[0]
    hk, hv = dk_total // GLA_HEADS, dv_total // GLA_HEADS
    t = min(SEQ_TILE, seq)
    assert seq % t == 0 and t % GLA_CHUNK == 0 and hk % LANES == 0 and hv % LANES == 0
    ns = seq // t
    k_off = dk_total // hk
    v_off = (2 * dk_total) // hv
    r_off = (2 * dk_total + dv_total) // hv
    rowi = lambda b, h, s: b * ns + s
    return pl.pallas_call(
        functools.partial(_gla_kernel, chunk=GLA_CHUNK),
        grid=(batch, GLA_HEADS, ns),
        in_specs=[
            pl.BlockSpec((t, hk), lambda b, h, s: (rowi(b, h, s), h)),
            pl.BlockSpec((t, hk), lambda b, h, s: (rowi(b, h, s), k_off + h)),
            pl.BlockSpec((t, hv), lambda b, h, s: (rowi(b, h, s), v_off + h)),
            pl.BlockSpec((t, hv), lambda b, h, s: (rowi(b, h, s), r_off + h)),
            pl.BlockSpec((t, LANES), lambda b, h, s: (rowi(b, h, s), 0)),
            pl.BlockSpec((LANES, hk), lambda b, h, s: (0, h)),
            pl.BlockSpec((1, hk), lambda b, h, s: (0, h)),
            pl.BlockSpec((1, hv), lambda b, h, s: (0, 0)),
        ],
        out_specs=pl.BlockSpec((t, hv), lambda b, h, s: (rowi(b, h, s), h)),
        out_shape=jax.ShapeDtypeStruct((m, dv_total), BF16),
        scratch_shapes=[pltpu.VMEM((hv, hk), F32)],
        compiler_params=_cparams("parallel", "parallel", "arbitrary"),
        name="gla",
    )(z, z, z, z, a1, w_alpha, b_alpha, gla_norm)


def _pool_kernel(p_ref, ph_ref, wp_ref, sc_ref, y_ref, *, windows):
    s = pl.program_id(1)
    t = p_ref.shape[0]
    halo = ph_ref.shape[0]
    gw = wp_ref.shape[1]
    dist = lax.broadcasted_iota(I32, (t, t), 0) - lax.broadcasted_iota(I32, (t, t), 1)
    dist_h = lax.broadcasted_iota(I32, (t, halo), 0) + halo - lax.broadcasted_iota(I32, (t, halo), 1)
    pos = s * t + lax.broadcasted_iota(I32, (t, 1), 0)
    for g, w in enumerate(windows):
        cs = slice(g * gw, (g + 1) * gw)
        pc = p_ref[:, cs]
        band = jnp.where((dist >= 0) & (dist < w), 1.0, 0.0).astype(BF16)
        band_h = jnp.where(dist_h < w, 1.0, 0.0).astype(BF16)
        tot = _dot(band, pc) + jnp.where(s > 0, _dot(band_h, ph_ref[:, cs]), 0.0)
        cnt = jnp.minimum(pos + 1, w).astype(F32)
        mixed = tot / cnt - pc.astype(F32)
        y = _dot(mixed.astype(BF16), wp_ref[g]) * sc_ref[:, cs]
        y_ref[:, cs] = y.astype(BF16)


def _pool(z, w_pool, pool_scale, batch, seq, p_col0):
    m = z.shape[0]
    groups, gw, _ = w_pool.shape
    pw = groups * gw
    t = min(SEQ_TILE, seq)
    assert seq % t == 0 and t % HALO == 0 and p_col0 % pw == 0 and gw % LANES == 0
    ns = seq // t
    pc = p_col0 // pw
    hb = t // HALO
    return pl.pallas_call(
        functools.partial(_pool_kernel, windows=POOL_WINDOWS),
        grid=(batch, ns),
        in_specs=[
            pl.BlockSpec((t, pw), lambda b, s: (b * ns + s, pc)),
            pl.BlockSpec((HALO, pw), lambda b, s: (jnp.maximum((b * ns + s) * hb - 1, 0), pc)),
            pl.BlockSpec((groups, gw, gw), lambda b, s: (0, 0, 0)),
            pl.BlockSpec((1, pw), lambda b, s: (0, 0)),
        ],
        out_specs=pl.BlockSpec((t, pw), lambda b, s: (b * ns + s, 0)),
        out_shape=jax.ShapeDtypeStruct((m, pw), BF16),
        compiler_params=_cparams("parallel", "arbitrary"),
        name="pool",
    )(z, z, w_pool, pool_scale)


def _merge_kernel(og_ref, yp_ref, ga_ref, gb_ref, wa_ref, wb_ref, u_ref):
    ya = _dot(og_ref[...], wa_ref[...])
    yb = _dot(yp_ref[...], wb_ref[...])
    ga = jax.nn.sigmoid(ga_ref[...].astype(F32))
    gb = jax.nn.sigmoid(gb_ref[...].astype(F32))
    u_ref[...] = (ga * ya + gb * yb).astype(BF16)


def _merge(og, yp, z, w_a, w_b, ga_col0, gb_col0):
    m, dv = og.shape
    pw = yp.shape[1]
    d = w_a.shape[1]
    bm, bn = min(ROW_TILE, m), min(COL_TILE, d)
    assert m % bm == 0 and d % bn == 0 and ga_col0 % bn == 0 and gb_col0 % bn == 0
    ga0, gb0 = ga_col0 // bn, gb_col0 // bn
    return pl.pallas_call(
        _merge_kernel,
        grid=(m // bm, d // bn),
        in_specs=[
            pl.BlockSpec((bm, dv), lambda i, j: (i, 0)),
            pl.BlockSpec((bm, pw), lambda i, j: (i, 0)),
            pl.BlockSpec((bm, bn), lambda i, j: (i, ga0 + j)),
            pl.BlockSpec((bm, bn), lambda i, j: (i, gb0 + j)),
            pl.BlockSpec((dv, bn), lambda i, j: (0, j)),
            pl.BlockSpec((pw, bn), lambda i, j: (0, j)),
        ],
        out_specs=pl.BlockSpec((bm, bn), lambda i, j: (i, j)),
        out_shape=jax.ShapeDtypeStruct((m, d), BF16),
        compiler_params=_cparams("parallel", "arbitrary"),
        name="merge",
    )(og, yp, z, z, w_a, w_b)


def _out_proj_kernel(u_ref, w_ref, x_ref, o_ref):
    o_ref[...] = x_ref[...] + _dot(u_ref[...], w_ref[...])


def _out_proj(u, w, x):
    m, d = x.shape
    bm, bn = min(ROW_TILE, m), min(COL_TILE, d)
    assert m % bm == 0 and d % bn == 0
    return pl.pallas_call(
        _out_proj_kernel,
        grid=(m // bm, d // bn),
        in_specs=[
            pl.BlockSpec((bm, u.shape[1]), lambda i, j: (i, 0)),
            pl.BlockSpec((u.shape[1], bn), lambda i, j: (0, j)),
            pl.BlockSpec((bm, bn), lambda i, j: (i, j)),
        ],
        out_specs=pl.BlockSpec((bm, bn), lambda i, j: (i, j)),
        out_shape=jax.ShapeDtypeStruct((m, d), F32),
        compiler_params=_cparams("parallel", "arbitrary"),
        name="out_proj",
    )(u, w, x)


def _rmsnorm_rows(x_ref, g_ref, h_ref, rows):
    def body(c, carry):
        r = pl.multiple_of(c * rows, rows)
        xf = x_ref[pl.ds(r, rows), :]
        ms = jnp.mean(xf * xf, axis=-1, keepdims=True)
        h_ref[pl.ds(r, rows), :] = (xf * lax.rsqrt(ms + EPS) * g_ref[...]).astype(BF16)
        return carry
    lax.fori_loop(0, x_ref.shape[0] // rows, body, 0)


def _swiglu_block(h, wg, wu, wd):
    g = _dot(h, wg)
    u = _dot(h, wu)
    a = (g * jax.nn.sigmoid(g) * u).astype(BF16)
    return _dot(a, wd)


def _ffn_kernel(x_ref, g_ref, wg_ref, wu_ref, wd_ref, o_ref, h_ref):
    f = pl.program_id(1)

    @pl.when(f == 0)
    def _():
        _rmsnorm_rows(x_ref, g_ref, h_ref, 128)

    y = _swiglu_block(h_ref[...], wg_ref[...], wu_ref[...], wd_ref[...])

    @pl.when(f == 0)
    def _():
        o_ref[...] = x_ref[...] + y

    @pl.when(f > 0)
    def _():
        o_ref[...] += y


def _ffn(x, g, wg, wu, wd):
    m, d = x.shape
    ff = wg.shape[1]
    bm, bf = min(FFN_ROW_TILE, m), min(FFN_COL_TILE, ff)
    assert m % bm == 0 and ff % bf == 0
    return pl.pallas_call(
        _ffn_kernel,
        grid=(m // bm, ff // bf),
        in_specs=[
            pl.BlockSpec((bm, d), lambda i, f: (i, 0)),
            pl.BlockSpec((1, d), lambda i, f: (0, 0)),
            pl.BlockSpec((d, bf), lambda i, f: (0, f)),
            pl.BlockSpec((d, bf), lambda i, f: (0, f)),
            pl.BlockSpec((bf, d), lambda i, f: (f, 0)),
        ],
        out_specs=pl.BlockSpec((bm, d), lambda i, f: (i, 0)),
        out_shape=jax.ShapeDtypeStruct((m, d), F32),
        scratch_shapes=[pltpu.VMEM((bm, d), BF16)],
        compiler_params=_cparams("parallel", "arbitrary"),
        name="ffn",
    )(x, g, wg, wu, wd)


def _to_slabs(slab_ref, x):
    rows, d = x.shape
    sl = d // LANES
    for s in range(sl):
        slab_ref[pl.ds(s, rows, stride=sl), :] = x[:, s * LANES:(s + 1) * LANES]


def _from_slabs(slab_ref, rows, d):
    sl = d // LANES
    return jnp.concatenate([slab_ref[pl.ds(s, rows, stride=sl), :] for s in range(sl)], axis=1)


def _router_kernel(x_ref, g_ref, wr_ref, hs_ref, route_ref, hb_ref):
    _rmsnorm_rows(x_ref, g_ref, hb_ref, 128)
    hb = hb_ref[...]
    _to_slabs(hs_ref, hb.astype(F32))
    logits = _dot(hb, wr_ref[...])
    lane = lax.broadcasted_iota(I32, logits.shape, 1)
    lane_f = lane.astype(F32)
    neg = jnp.float32(-jnp.inf)
    l1 = jnp.where(lane < N_EXPERTS, logits, neg)
    m1 = jnp.max(l1, axis=-1, keepdims=True)
    i1 = jnp.min(jnp.where(l1 == m1, lane_f, float(LANES)), axis=-1, keepdims=True)
    l2 = jnp.where(lane_f == i1, neg, l1)
    m2 = jnp.max(l2, axis=-1, keepdims=True)
    i2 = jnp.min(jnp.where(l2 == m2, lane_f, float(LANES)), axis=-1, keepdims=True)
    e = jnp.exp(m2 - m1)
    den = 1.0 + e
    route = jnp.where(lane == 0, i1, 0.0)
    route = jnp.where(lane == 1, i2, route)
    route = jnp.where(lane == 2, 1.0 / den, route)
    route = jnp.where(lane == 3, e / den, route)
    route_ref[...] = route


def _router(x, g, wr):
    m, d = x.shape
    bm = min(FFN_ROW_TILE, m)
    sl = d // LANES
    assert m % bm == 0 and sl % 8 == 0
    return pl.pallas_call(
        _router_kernel,
        grid=(m // bm,),
        in_specs=[
            pl.BlockSpec((bm, d), lambda i: (i, 0)),
            pl.BlockSpec((1, d), lambda i: (0, 0)),
            pl.BlockSpec((d, LANES), lambda i: (0, 0)),
        ],
        out_specs=[
            pl.BlockSpec((bm * sl, LANES), lambda i: (i, 0)),
            pl.BlockSpec((bm, LANES), lambda i: (i, 0)),
        ],
        out_shape=[jax.ShapeDtypeStruct((m * sl, LANES), F32), jax.ShapeDtypeStruct((m, LANES), F32)],
        scratch_shapes=[pltpu.VMEM((bm, d), BF16)],
        compiler_params=_cparams("parallel"),
        name="router",
    )(x, g, wr)


def _route_plan_kernel(route_ref, pos_ref, meta_ref, rank_ref, *, tile, n_tiles, blk):
    m = route_ref.shape[0]
    nb = m // blk
    lane_b = lax.broadcasted_iota(I32, (blk, LANES), 1)
    lane_r = lax.broadcasted_iota(I32, (1, LANES), 1)
    ne = N_EXPERTS

    def onehot(r):
        rt = route_ref[pl.ds(r, blk), :]
        i1 = rt[:, 0:1].astype(I32)
        i2 = rt[:, 1:2].astype(I32)
        return jnp.where((lane_b == i1) | (lane_b == i2 + ne), 1.0, 0.0)

    strict = jnp.where(lax.broadcasted_iota(I32, (blk, blk), 0) > lax.broadcasted_iota(I32, (blk, blk), 1),
                       1.0, 0.0).astype(BF16)

    def rank_body(c, carry):
        r = pl.multiple_of(c * blk, blk)
        oh = onehot(r)
        rank_ref[pl.ds(r, blk), :] = _dot(strict, oh.astype(BF16)) + carry
        return carry + jnp.sum(oh, axis=0, keepdims=True)

    cnt12 = lax.fori_loop(0, nb, rank_body, jnp.zeros((1, LANES), F32))

    def lane_val(row, l):
        return jnp.sum(jnp.where(lane_r == l, row, 0.0), axis=-1, keepdims=True)

    cnt = jnp.zeros((1, LANES), F32)
    for e in range(ne):
        cnt = jnp.where(lane_r == e, lane_val(cnt12, e) + lane_val(cnt12, e + ne), cnt)
    tiles = jnp.floor((cnt + (tile - 1)) * (1.0 / tile))
    start = jnp.zeros((1, LANES), F32)
    for e in range(1, ne):
        start = start + jnp.where(lane_r >= e, lane_val(tiles, e - 1), 0.0)
    start = jnp.where(lane_r < ne, start * tile, 0.0)
    end = start + tiles * tile
    base = start
    for e in range(ne):
        base = jnp.where(lane_r == e + ne, lane_val(start, e) + lane_val(cnt12, e), base)

    def pos_body(c, carry):
        r = pl.multiple_of(c * blk, blk)
        slot = onehot(r) * (rank_ref[pl.ds(r, blk), :] + base)
        p1 = jnp.sum(jnp.where(lane_b < ne, slot, 0.0), axis=-1, keepdims=True)
        p2 = jnp.sum(jnp.where(lane_b >= ne, slot, 0.0), axis=-1, keepdims=True)
        both = jnp.where(lane_b == 0, p1, jnp.where(lane_b == 1, p2, 0.0))
        pos_ref[:, pl.ds(r, blk)] = jnp.transpose(both)[0:8, :].astype(I32)
        return carry

    lax.fori_loop(0, nb, pos_body, 0)

    n_used = lane_val(end, ne - 1) * (1.0 / tile)
    tile_start = lane_r.astype(F32) * tile
    tile_e = jnp.zeros((1, LANES), F32)
    last_e = jnp.zeros((1, 1), F32)
    for e in range(ne):
        tile_e = tile_e + jnp.where(lane_val(end, e) <= tile_start, 1.0, 0.0)
        last_e = jnp.where(lane_val(cnt, e) > 0, float(e), last_e)
    tile_e = jnp.where(lane_r.astype(F32) < n_used, tile_e, last_e)
    info = jnp.where(lane_r < ne, end, jnp.where(lane_r == 2 * ne, n_used, 0.0))
    for e in range(ne):
        info = jnp.where(lane_r == e + ne, lane_val(cnt, e), info)
    row8 = lax.broadcasted_iota(I32, (8, LANES), 0)
    meta = jnp.where(row8 == 0, tile_e, jnp.where(row8 == 1, info, jnp.where(row8 == 2, n_used, 0.0)))
    meta_ref[...] = meta.astype(I32)


def _route_plan(route, tile, n_tiles):
    m = route.shape[0]
    blk = min(PLAN_TILE, m)
    assert m % blk == 0 and n_tiles <= LANES
    return pl.pallas_call(
        functools.partial(_route_plan_kernel, tile=tile, n_tiles=n_tiles, blk=blk),
        out_shape=[jax.ShapeDtypeStruct((8, m), I32), jax.ShapeDtypeStruct((8, LANES), I32)],
        scratch_shapes=[pltpu.VMEM((m, LANES), F32)],
        compiler_params=pltpu.CompilerParams(vmem_limit_bytes=VMEM_LIMIT_BYTES),
        name="route_plan",
    )(route)


def _dispatch_kernel(meta_ref, pos_ref, h_ref, xs_ref, zero_ref, sem, *, tile, sl):
    i = pl.program_id(0)
    tt = h_ref.shape[0] // sl
    n_tiles = xs_ref.shape[0] // (tile * sl)

    def zero_copy(e):
        first = pl.multiple_of((meta_ref[e] - tile) * sl, tile * sl)
        return pltpu.make_async_copy(zero_ref, xs_ref.at[pl.ds(first, tile * sl), :], sem.at[1])

    def tail_copy(j):
        return pltpu.make_async_copy(zero_ref, xs_ref.at[pl.ds(j * tile * sl, tile * sl), :], sem.at[1])

    @pl.when(i == 0)
    def _():
        zero_ref[...] = jnp.zeros_like(zero_ref)
        n_used = meta_ref[2 * N_EXPERTS]
        for e in range(N_EXPERTS):
            @pl.when(meta_ref[N_EXPERTS + e] > 0)
            def _():
                zero_copy(e).start()
        for j in range(n_tiles - N_EXPERTS, n_tiles):
            @pl.when(j >= n_used)
            def _():
                tail_copy(j).start()
        for e in range(N_EXPERTS):
            @pl.when(meta_ref[N_EXPERTS + e] > 0)
            def _():
                zero_copy(e).wait()
        for j in range(n_tiles - N_EXPERTS, n_tiles):
            @pl.when(j >= n_used)
            def _():
                tail_copy(j).wait()

    def row_copy(r, c):
        src = pl.multiple_of(r * sl, sl)
        dst = pl.multiple_of(pos_ref[0, c * tt + r] * sl, sl)
        return pltpu.make_async_copy(h_ref.at[pl.ds(src, sl), :], xs_ref.at[pl.ds(dst, sl), :], sem.at[0])

    def start_body(r, carry):
        row_copy(r, 0).start()
        row_copy(r, 1).start()
        return carry

    def wait_body(r, carry):
        row_copy(r, 0).wait()
        row_copy(r, 1).wait()
        return carry

    lax.fori_loop(0, tt, start_body, 0)
    lax.fori_loop(0, tt, wait_body, 0)


def _dispatch(hs, pos_tiles, ginfo, n_slots, tile, sl):
    m = hs.shape[0] // sl
    tt = pos_tiles.shape[2] // 2
    return pl.pallas_call(
        functools.partial(_dispatch_kernel, tile=tile, sl=sl),
        grid_spec=pltpu.PrefetchScalarGridSpec(
            num_scalar_prefetch=1,
            grid=(m // tt,),
            in_specs=[
                pl.BlockSpec((None, 1, 2 * tt), lambda i, meta: (i, 0, 0), memory_space=pltpu.SMEM),
                pl.BlockSpec((tt * sl, LANES), lambda i, meta: (i, 0)),
            ],
            out_specs=pl.BlockSpec(memory_space=pl.ANY),
            scratch_shapes=[pltpu.VMEM((tile * sl, LANES), F32), pltpu.SemaphoreType.DMA((2,))],
        ),
        out_shape=jax.ShapeDtypeStruct((n_slots * sl, LANES), F32),
        compiler_params=_cparams("arbitrary"),
        name="dispatch",
    )(ginfo, pos_tiles, hs)


def _expert_ffn_kernel(te_ref, nu_ref, xs_ref, wg_ref, wu_ref, wd_ref, y_ref, h_ref, acc_ref):
    i, f = pl.program_id(0), pl.program_id(1)
    used = i < nu_ref[0]
    last = pl.num_programs(1) - 1
    tile, d = h_ref.shape

    @pl.when(used & (f == 0))
    def _():
        h_ref[...] = _from_slabs(xs_ref, tile, d).astype(BF16)

    @pl.when(used)
    def _():
        y = _swiglu_block(h_ref[...], wg_ref[...], wu_ref[...], wd_ref[...])

        @pl.when(f == 0)
        def _():
            acc_ref[...] = y

        @pl.when(f > 0)
        def _():
            acc_ref[...] += y

    @pl.when(used & (f == last))
    def _():
        _to_slabs(y_ref, acc_ref[...])

    @pl.when(jnp.logical_not(used) & (f == last))
    def _():
        y_ref[...] = jnp.zeros_like(y_ref)


def _expert_ffn(xs, tile_e, n_used, wg, wu, wd, tile, sl):
    n_slots = xs.shape[0] // sl
    d = sl * LANES
    ff = wg.shape[2]
    bf = min(FFN_COL_TILE, ff)
    assert n_slots % tile == 0 and ff % bf == 0
    nf = ff // bf
    row = lambda i, f, te, nu: (jnp.minimum(i, nu[0] - 1), 0)
    fblk = lambda i, f, nu: jnp.where(i < nu[0], f, nf - 1)
    return pl.pallas_call(
        _expert_ffn_kernel,
        grid_spec=pltpu.PrefetchScalarGridSpec(
            num_scalar_prefetch=2,
            grid=(n_slots // tile, nf),
            in_specs=[
                pl.BlockSpec((tile * sl, LANES), row),
                pl.BlockSpec((None, d, bf), lambda i, f, te, nu: (te[i], 0, fblk(i, f, nu))),
                pl.BlockSpec((None, d, bf), lambda i, f, te, nu: (te[i], 0, fblk(i, f, nu))),
                pl.BlockSpec((None, bf, d), lambda i, f, te, nu: (te[i], fblk(i, f, nu), 0)),
            ],
            out_specs=pl.BlockSpec((tile * sl, LANES), lambda i, f, te, nu: (i, 0)),
            scratch_shapes=[pltpu.VMEM((tile, d), BF16), pltpu.VMEM((tile, d), F32)],
        ),
        out_shape=jax.ShapeDtypeStruct((n_slots * sl, LANES), F32),
        compiler_params=_cparams("parallel", "arbitrary"),
        name="expert_ffn",
    )(tile_e, n_used, xs, wg, wu, wd)


def _combine_kernel(pos_ref, x_ref, route_ref, g_ref, y_ref, o_ref, buf_ref, sem):
    tt, d = x_ref.shape
    sl = d // LANES

    def row_copy(r, c):
        src = pl.multiple_of(pos_ref[0, c * tt + r] * sl, sl)
        dst = pl.multiple_of(r * sl, sl)
        return pltpu.make_async_copy(y_ref.at[pl.ds(src, sl), :], buf_ref.at[c, pl.ds(dst, sl), :], sem.at[0])

    def start_body(r, carry):
        row_copy(r, 0).start()
        row_copy(r, 1).start()
        return carry

    def wait_body(r, carry):
        row_copy(r, 0).wait()
        row_copy(r, 1).wait()
        return carry

    lax.fori_loop(0, tt, start_body, 0)
    lax.fori_loop(0, tt, wait_body, 0)

    rt = route_ref[...]
    moe = rt[:, 2:3] * _from_slabs(buf_ref.at[0], tt, d) + rt[:, 3:4] * _from_slabs(buf_ref.at[1], tt, d)
    xo = x_ref[...] + moe
    ms = jnp.mean(xo * xo, axis=-1, keepdims=True)
    o_ref[...] = xo * lax.rsqrt(ms + EPS) * g_ref[...]


def _combine(x, route, g, y, pos_tiles):
    m, d = x.shape
    tt = pos_tiles.shape[2] // 2
    return pl.pallas_call(
        _combine_kernel,
        grid=(m // tt,),
        in_specs=[
            pl.BlockSpec((None, 1, 2 * tt), lambda i: (i, 0, 0), memory_space=pltpu.SMEM),
            pl.BlockSpec((tt, d), lambda i: (i, 0)),
            pl.BlockSpec((tt, LANES), lambda i: (i, 0)),
            pl.BlockSpec((1, d), lambda i: (0, 0)),
            pl.BlockSpec(memory_space=pl.ANY),
        ],
        out_specs=pl.BlockSpec((tt, d), lambda i: (i, 0)),
        out_shape=jax.ShapeDtypeStruct((m, d), F32),
        scratch_shapes=[pltpu.VMEM((2, tt * (d // LANES), LANES), F32), pltpu.SemaphoreType.DMA((1,))],
        compiler_params=_cparams("arbitrary"),
        name="combine",
    )(pos_tiles, x, route, g, y)


def _moe_ffn_final(x, g_ffn, wr, wg, wu, wd, g_final):
    m, d = x.shape
    tile = min(FFN_ROW_TILE, m)
    n_tiles = (2 * m) // tile + N_EXPERTS
    tt = min(TOK_TILE, m)
    sl = d // LANES
    hs, route = _router(x, g_ffn, wr)
    pos, meta = _route_plan(route, tile, n_tiles)
    pos_tiles = pos[:2].reshape(2, m // tt, tt).transpose(1, 0, 2).reshape(m // tt, 1, 2 * tt)
    xs = _dispatch(hs, pos_tiles, meta[1, :2 * N_EXPERTS + 1], n_tiles * tile, tile, sl)
    y = _expert_ffn(xs, meta[0, :n_tiles], meta[2, :1], wg, wu, wd, tile, sl)
    return _combine(x, route, g_final, y, pos_tiles)


def kernel(x, ln_mix, w_in, w_alpha, b_alpha, gla_norm, w_pool, pool_scale, w_branch_gla, w_branch_pool, w_out,
           ln_ffn, ffn_w_gate, ffn_w_up, ffn_w_down, router_w, exp_w_gate, exp_w_up, exp_w_down, ln_final):
    batch, seq, d = x.shape
    depth = w_in.shape[0]
    dk = w_alpha.shape[2]
    dv = w_branch_gla.shape[1]
    pw = w_branch_pool.shape[1]
    rank = w_alpha.shape[1]
    assert depth == 2 and rank <= LANES
    a_col = 2 * dk + 2 * dv
    m = batch * seq
    xf = x.reshape(m, d)
    row = lambda v: v.reshape(1, -1)

    for l in range(depth):
        w_main = jnp.concatenate([w_in[l][:, :a_col], w_in[l][:, a_col + rank:]], axis=1).astype(BF16)
        w_a1 = jnp.pad(w_in[l][:, a_col:a_col + rank], ((0, 0), (0, LANES - rank))).astype(BF16)
        w_al = jnp.pad(w_alpha[l], ((0, LANES - rank), (0, 0)))
        z, a1 = _norm_inproj(xf, row(ln_mix[l]), w_main, w_a1)
        og = _gla(z, a1, w_al, row(b_alpha[l]), row(gla_norm[l]), batch, seq, dk, dv)
        yp = _pool(z, w_pool[l].astype(BF16), row(pool_scale[l]), batch, seq, a_col)
        u = _merge(og, yp, z, w_branch_gla[l].astype(BF16), w_branch_pool[l].astype(BF16),
                   a_col + pw, a_col + pw + d)
        xf = _out_proj(u, w_out[l].astype(BF16), xf)
        if l % 2 == 0:
            i = l // 2
            xf = _ffn(xf, row(ln_ffn[l]), ffn_w_gate[i].astype(BF16), ffn_w_up[i].astype(BF16),
                      ffn_w_down[i].astype(BF16))
        else:
            i = l // 2
            wr = jnp.pad(router_w[i], ((0, 0), (0, LANES - N_EXPERTS))).astype(BF16)
            xf = _moe_ffn_final(xf, row(ln_ffn[l]), wr, exp_w_gate[i].astype(BF16), exp_w_up[i].astype(BF16),
                                exp_w_down[i].astype(BF16), ln_final.reshape(1, -1))
    return xf.reshape(batch, seq, d)
```

```python
import functools
import math

import jax
import jax.numpy as jnp
from jax import lax
from jax.experimental import pallas as pl
from jax.experimental.pallas import tpu as pltpu

F32 = jnp.float32
BF16 = jnp.bfloat16
I32 = jnp.int32

EPS = 1e-6
GLA_HEADS = 4
GLA_LOWRANK = 16
GLA_GATE_TEMP = 16.0
GLA_CHUNK = 64
GLA_HEADS_PER_STEP = 2
POOL_WINDOWS = (2, 4, 8, 16)
POOL_GROUPS = 4
N_EXPERTS = 8

LANES = 128
VMEM_LIMIT_BYTES = 56 * 1024 * 1024

ROW_TILE = 1024
COL_TILE = 1024
FFN_ROW_TILE = 512
FFN_COL_TILE = 512
SEQ_TILE = 512
TOK_TILE = 256
PLAN_TILE = 512
HALO = 16


def _cparams(*sem):
    return pltpu.CompilerParams(dimension_semantics=sem, vmem_limit_bytes=VMEM_LIMIT_BYTES)


def _dot(a, b):
    return jnp.dot(a, b, preferred_element_type=F32)


def _split_bf16(x):
    hi = x.astype(BF16)
    lo = (x - hi.astype(F32)).astype(BF16)
    return hi, lo


def _pick(n, target, align=LANES):
    if n <= target:
        return n
    best = None
    for cand in range(align, target + 1, align):
        if n % cand == 0:
            best = cand
    assert best is not None, (n, target, align)
    return best


def _cast_kernel(x_ref, o_ref):
    o_ref[...] = x_ref[...].astype(BF16)


def _to_bf16(w, layer):
    shape = w.shape[1:]
    w3 = w.reshape(w.shape[0], -1, shape[-1])
    _, r, c = w3.shape
    br, bc = _pick(r, 1024, 16), _pick(c, 2048)
    out = pl.pallas_call(
        _cast_kernel,
        grid=(r // br, c // bc),
        in_specs=[pl.BlockSpec((None, br, bc), lambda i, j: (layer, i, j))],
        out_specs=pl.BlockSpec((br, bc), lambda i, j: (i, j)),
        out_shape=jax.ShapeDtypeStruct((r, c), BF16),
        compiler_params=_cparams("parallel", "parallel"),
        name="to_bf16",
    )(w3)
    return out.reshape(shape)


def _inproj_weight_kernel(w_ref, nxt_ref, lr_ref, wm_ref, wa_ref, *, first_shifted, rank):
    j = pl.program_id(0)

    @pl.when(j < first_shifted)
    def _():
        wm_ref[...] = w_ref[...].astype(BF16)

    @pl.when(j >= first_shifted)
    def _():
        shifted = jnp.concatenate([w_ref[:, rank:], nxt_ref[:, :rank]], axis=1)
        wm_ref[...] = shifted.astype(BF16)

    @pl.when(j == 0)
    def _():
        lane = lax.broadcasted_iota(I32, lr_ref.shape, 1)
        wa_ref[...] = jnp.where(lane < rank, lr_ref[...], 0.0).astype(BF16)


def _inproj_weight(w, layer, a_col, rank):
    _, d, n_in = w.shape
    n = n_in - rank
    bc = _pick(math.gcd(n, a_col), 512)
    assert a_col % bc == 0 and rank < LANES and bc % LANES == 0
    per = bc // LANES
    return pl.pallas_call(
        functools.partial(_inproj_weight_kernel, first_shifted=a_col // bc, rank=rank),
        grid=(n // bc,),
        in_specs=[
            pl.BlockSpec((None, d, bc), lambda j: (layer, 0, j)),
            pl.BlockSpec((None, d, LANES), lambda j: (layer, 0, (j + 1) * per)),
            pl.BlockSpec((None, d, LANES), lambda j: (layer, 0, a_col // LANES)),
        ],
        out_specs=[
            pl.BlockSpec((d, bc), lambda j: (0, j)),
            pl.BlockSpec((d, LANES), lambda j: (0, 0)),
        ],
        out_shape=[jax.ShapeDtypeStruct((d, n), BF16), jax.ShapeDtypeStruct((d, LANES), BF16)],
        compiler_params=_cparams("arbitrary"),
        name="inproj_weight",
    )(w, w, w)


def _norm_inproj_kernel(x_ref, g_ref, w_ref, wa_ref, z_ref, a1_ref, h_ref, *, rows):
    @pl.when(pl.program_id(1) == 0)
    def _():
        def body(c, carry):
            r = pl.multiple_of(c * rows, rows)
            xf = x_ref[pl.ds(r, rows), :]
            ms = jnp.mean(xf * xf, axis=-1, keepdims=True)
            h_ref[pl.ds(r, rows), :] = (xf * lax.rsqrt(ms + EPS) * g_ref[...]).astype(BF16)
            return carry
        lax.fori_loop(0, x_ref.shape[0] // rows, body, 0)
        a1_ref[...] = _dot(h_ref[...], wa_ref[...])

    z_ref[...] = _dot(h_ref[...], w_ref[...]).astype(BF16)


def _norm_inproj(x, g, w, wa):
    m, d = x.shape
    n = w.shape[1]
    bm, bn = min(ROW_TILE, m), _pick(n, COL_TILE)
    assert m % bm == 0 and n % bn == 0
    return pl.pallas_call(
        functools.partial(_norm_inproj_kernel, rows=128),
        grid=(m // bm, n // bn),
        in_specs=[
            pl.BlockSpec((bm, d), lambda i, j: (i, 0)),
            pl.BlockSpec((1, d), lambda i, j: (0, 0)),
            pl.BlockSpec((d, bn), lambda i, j: (0, j)),
            pl.BlockSpec((d, LANES), lambda i, j: (0, 0)),
        ],
        out_specs=[
            pl.BlockSpec((bm, bn), lambda i, j: (i, j)),
            pl.BlockSpec((bm, LANES), lambda i, j: (i, 0)),
        ],
        out_shape=[jax.ShapeDtypeStruct((m, n), BF16), jax.ShapeDtypeStruct((m, LANES), F32)],
        scratch_shapes=[pltpu.VMEM((bm, d), BF16)],
        compiler_params=_cparams("parallel", "arbitrary"),
        name="norm_inproj",
    )(x, g, w, wa)


def _log_sigmoid(x):
    return jnp.minimum(x, 0.0) - jnp.log(1.0 + jnp.exp(-jnp.abs(x)))


def _gla_kernel(q_ref, k_ref, v_ref, r_ref, a1_ref, wal_ref, bal_ref, gn_ref, tri_ref, o_ref, st_ref, *,
                chunk, heads):
    @pl.when(pl.program_id(2) == 0)
    def _():
        st_ref[...] = jnp.zeros_like(st_ref)

    hk = q_ref.shape[1] // heads
    hv = v_ref.shape[1] // heads
    a1 = _split_bf16(a1_ref[...])
    for h in range(heads):
        ks, vs = slice(h * hk, (h + 1) * hk), slice(h * hv, (h + 1) * hv)
        o, st_new = _gla_head(q_ref[:, ks], k_ref[:, ks], v_ref[:, vs], r_ref[:, vs], a1, wal_ref[:, ks],
                              bal_ref[:, ks], gn_ref[...], tri_ref[...], st_ref[h], chunk)
        o_ref[:, vs] = o
        st_ref[h] = st_new


def _gla_head(q, k, v, r, a1, wal, bal, gn, tri, st, chunk):
    t, dk = q.shape
    nc = t // chunk
    pair = 2 * chunk
    shift = chunk.bit_length() - 1
    nt_dot = lambda a, bb: lax.dot_general(a, bb, (((1,), (1,)), ((), ())), preferred_element_type=F32)
    rows = lambda x, c0, c1: x[c0 * chunk:c1 * chunk, :]

    a_hi, a_lo = a1
    w_hi, w_lo = _split_bf16(wal)
    xg = _dot(a_hi, w_hi) + _dot(a_hi, w_lo) + _dot(a_lo, w_hi) + bal
    la = _log_sigmoid(xg) * (1.0 / GLA_GATE_TEMP)

    la_hi, la_lo = _split_bf16(la)
    b = _dot(tri, la_hi) + _dot(tri, la_lo)

    bl = [b[(c + 1) * chunk - 1:(c + 1) * chunk, :] for c in range(nc)]
    pre = [jnp.zeros_like(bl[0])]
    for c in range(nc):
        pre.append(pre[c] + bl[c])

    q = q.astype(F32)
    k = k.astype(F32)
    q_t = q * jnp.exp(b) * (dk ** -0.5)
    k_t = (k * jnp.exp(-b)).astype(BF16)
    qc = [rows(q_t, c, c + 1) for c in range(nc)]
    kec = [rows(k, c, c + 1) * jnp.exp(bl[c] - rows(b, c, c + 1)) for c in range(nc)]
    q_tb = q_t.astype(BF16)
    k_eb = jnp.concatenate(kec, axis=0).astype(BF16)

    def q_from(mid, hi):
        return jnp.concatenate([qc[c] if c == mid else qc[c] * jnp.exp(pre[c] - pre[mid])
                                for c in range(mid, hi)], axis=0).astype(BF16)

    def k_upto(lo, mid):
        return jnp.concatenate([kec[m] if m == mid - 1 else kec[m] * jnp.exp(pre[mid] - pre[m + 1])
                                for m in range(lo, mid)], axis=0).astype(BF16)

    nb = t // pair
    blocks = [[None] * nb for _ in range(nb)]
    r2 = lax.broadcasted_iota(I32, (pair, pair), 0)
    c2 = lax.broadcasted_iota(I32, (pair, pair), 1)
    same = ((r2 >> shift) == (c2 >> shift)) & (r2 >= c2)
    cross = (r2 >= chunk) & (c2 < chunk)
    for p in range(nb):
        qp = rows(q_tb, 2 * p, 2 * p + 2)
        in_chunk = nt_dot(qp, rows(k_t, 2 * p, 2 * p + 2))
        next_chunk = nt_dot(qp, rows(k_eb, 2 * p, 2 * p + 2))
        blocks[p][p] = jnp.where(same, in_chunk, jnp.where(cross, next_chunk, 0.0))
    g = 4
    while g <= nc:
        half = g // 2
        for grp in range(nc // g):
            lo, mid, hi = grp * g, grp * g + half, (grp + 1) * g
            x = nt_dot(q_from(mid, hi), k_upto(lo, mid))
            hb = half // 2
            for i in range(hb):
                for j in range(hb):
                    blocks[mid // 2 + i][lo // 2 + j] = x[i * pair:(i + 1) * pair, j * pair:(j + 1) * pair]
        g *= 2
    zero_blk = jnp.zeros((pair, pair), F32)
    att = jnp.concatenate(
        [jnp.concatenate([zero_blk if blk is None else blk for blk in brow], axis=1) for brow in blocks], axis=0)

    o = _dot(att.astype(BF16), v) + nt_dot(q_from(0, nc), st.astype(BF16))
    upd = lax.dot_general(v, k_upto(0, nc), (((0,), (0,)), ((), ())), preferred_element_type=F32)
    st_new = st * jnp.exp(pre[nc]) + upd

    o = o * lax.rsqrt(jnp.mean(o * o, axis=-1, keepdims=True) + EPS) * gn
    r = r.astype(F32)
    return (o * (r * jax.nn.sigmoid(r))).astype(BF16), st_new


def _gla(z, a1, w_alpha, b_alpha, gla_norm, batch, seq, dk_total, dv_total):
    m = z.shape[0]
    hk, hv = dk_total // GLA_HEADS, dv_total // GLA_HEADS
    t = min(SEQ_TILE, seq)
    assert seq % t == 0 and t % GLA_CHUNK == 0 and hk % LANES == 0 and hv % LANES == 0
    ns = seq // t
    nc = t // GLA_CHUNK
    assert nc >= 2 and nc & (nc - 1) == 0 and GLA_HEADS % GLA_HEADS_PER_STEP == 0
    hps = GLA_HEADS_PER_STEP
    gk, gv = hps * hk, hps * hv
    k_off = dk_total // gk
    v_off = (2 * dk_total) // gv
    r_off = (2 * dk_total + dv_total) // gv
    idx = jnp.arange(t, dtype=I32)
    tri = ((idx[:, None] // GLA_CHUNK == idx[None, :] // GLA_CHUNK) & (idx[:, None] >= idx[None, :])).astype(BF16)
    rowi = lambda b, h, s: b * ns + s
    return pl.pallas_call(
        functools.partial(_gla_kernel, chunk=GLA_CHUNK, heads=hps),
        grid=(batch, GLA_HEADS // hps, ns),
        in_specs=[
            pl.BlockSpec((t, gk), lambda b, h, s: (rowi(b, h, s), h)),
            pl.BlockSpec((t, gk), lambda b, h, s: (rowi(b, h, s), k_off + h)),
            pl.BlockSpec((t, gv), lambda b, h, s: (rowi(b, h, s), v_off + h)),
            pl.BlockSpec((t, gv), lambda b, h, s: (rowi(b, h, s), r_off + h)),
            pl.BlockSpec((t, LANES), lambda b, h, s: (rowi(b, h, s), 0)),
            pl.BlockSpec((LANES, gk), lambda b, h, s: (0, h)),
            pl.BlockSpec((1, gk), lambda b, h, s: (0, h)),
            pl.BlockSpec((1, hv), lambda b, h, s: (0, 0)),
            pl.BlockSpec((t, t), lambda b, h, s: (0, 0)),
        ],
        out_specs=pl.BlockSpec((t, gv), lambda b, h, s: (rowi(b, h, s), h)),
        out_shape=jax.ShapeDtypeStruct((m, dv_total), BF16),
        scratch_shapes=[pltpu.VMEM((hps, hv, hk), F32)],
        compiler_params=_cparams("parallel", "parallel", "arbitrary"),
        name="gla",
    )(z, z, z, z, a1, w_alpha, b_alpha, gla_norm, tri)


def _pool_kernel(p_ref, ph_ref, wp_ref, sc_ref, y_ref, *, windows):
    s = pl.program_id(1)
    t = p_ref.shape[0]
    halo = ph_ref.shape[0]
    gw = wp_ref.shape[1]
    dist = lax.broadcasted_iota(I32, (t, t), 0) - lax.broadcasted_iota(I32, (t, t), 1)
    dist_h = lax.broadcasted_iota(I32, (t, halo), 0) + halo - lax.broadcasted_iota(I32, (t, halo), 1)
    pos = s * t + lax.broadcasted_iota(I32, (t, 1), 0)
    for g, w in enumerate(windows):
        cs = slice(g * gw, (g + 1) * gw)
        pc = p_ref[:, cs]
        band = jnp.where((dist >= 0) & (dist < w), 1.0, 0.0).astype(BF16)
        band_h = jnp.where(dist_h < w, 1.0, 0.0).astype(BF16)
        tot = _dot(band, pc) + jnp.where(s > 0, _dot(band_h, ph_ref[:, cs]), 0.0)
        cnt = jnp.minimum(pos + 1, w).astype(F32)
        mixed = tot / cnt - pc.astype(F32)
        y = _dot(mixed.astype(BF16), wp_ref[g]) * sc_ref[:, cs]
        y_ref[:, cs] = y.astype(BF16)


def _pool(z, w_pool, pool_scale, batch, seq, p_col0):
    m = z.shape[0]
    groups, gw, _ = w_pool.shape
    pw = groups * gw
    t = min(SEQ_TILE, seq)
    assert seq % t == 0 and t % HALO == 0 and p_col0 % pw == 0 and gw % LANES == 0
    ns = seq // t
    pc = p_col0 // pw
    hb = t // HALO
    return pl.pallas_call(
        functools.partial(_pool_kernel, windows=POOL_WINDOWS),
        grid=(batch, ns),
        in_specs=[
            pl.BlockSpec((t, pw), lambda b, s: (b * ns + s, pc)),
            pl.BlockSpec((HALO, pw), lambda b, s: (jnp.maximum((b * ns + s) * hb - 1, 0), pc)),
            pl.BlockSpec((groups, gw, gw), lambda b, s: (0, 0, 0)),
            pl.BlockSpec((1, pw), lambda b, s: (0, 0)),
        ],
        out_specs=pl.BlockSpec((t, pw), lambda b, s: (b * ns + s, 0)),
        out_shape=jax.ShapeDtypeStruct((m, pw), BF16),
        compiler_params=_cparams("parallel", "arbitrary"),
        name="pool",
    )(z, z, w_pool, pool_scale)


def _merge_kernel(og_ref, yp_ref, ga_ref, gb_ref, wa_ref, wb_ref, u_ref):
    ya = _dot(og_ref[...], wa_ref[...])
    yb = _dot(yp_ref[...], wb_ref[...])
    ga = jax.nn.sigmoid(ga_ref[...].astype(F32))
    gb = jax.nn.sigmoid(gb_ref[...].astype(F32))
    u_ref[...] = (ga * ya + gb * yb).astype(BF16)


def _merge(og, yp, z, w_a, w_b, ga_col0, gb_col0):
    m, dv = og.shape
    pw = yp.shape[1]
    d = w_a.shape[1]
    bm, bn = min(ROW_TILE, m), _pick(math.gcd(d, ga_col0, gb_col0), COL_TILE)
    assert m % bm == 0 and d % bn == 0 and ga_col0 % bn == 0 and gb_col0 % bn == 0
    ga0, gb0 = ga_col0 // bn, gb_col0 // bn
    return pl.pallas_call(
        _merge_kernel,
        grid=(m // bm, d // bn),
        in_specs=[
            pl.BlockSpec((bm, dv), lambda i, j: (i, 0)),
            pl.BlockSpec((bm, pw), lambda i, j: (i, 0)),
            pl.BlockSpec((bm, bn), lambda i, j: (i, ga0 + j)),
            pl.BlockSpec((bm, bn), lambda i, j: (i, gb0 + j)),
            pl.BlockSpec((dv, bn), lambda i, j: (0, j)),
            pl.BlockSpec((pw, bn), lambda i, j: (0, j)),
        ],
        out_specs=pl.BlockSpec((bm, bn), lambda i, j: (i, j)),
        out_shape=jax.ShapeDtypeStruct((m, d), BF16),
        compiler_params=_cparams("parallel", "arbitrary"),
        name="merge",
    )(og, yp, z, z, w_a, w_b)


def _out_proj_kernel(u_ref, w_ref, x_ref, o_ref):
    o_ref[...] = x_ref[...] + _dot(u_ref[...], w_ref[...])


def _out_proj(u, w, x):
    m, d = x.shape
    bm, bn = min(ROW_TILE, m), _pick(d, COL_TILE)
    assert m % bm == 0 and d % bn == 0
    return pl.pallas_call(
        _out_proj_kernel,
        grid=(m // bm, d // bn),
        in_specs=[
            pl.BlockSpec((bm, u.shape[1]), lambda i, j: (i, 0)),
            pl.BlockSpec((u.shape[1], bn), lambda i, j: (0, j)),
            pl.BlockSpec((bm, bn), lambda i, j: (i, j)),
        ],
        out_specs=pl.BlockSpec((bm, bn), lambda i, j: (i, j)),
        out_shape=jax.ShapeDtypeStruct((m, d), F32),
        compiler_params=_cparams("parallel", "arbitrary"),
        name="out_proj",
    )(u, w, x)


def _rmsnorm_rows(x_ref, g_ref, h_ref, rows):
    def body(c, carry):
        r = pl.multiple_of(c * rows, rows)
        xf = x_ref[pl.ds(r, rows), :]
        ms = jnp.mean(xf * xf, axis=-1, keepdims=True)
        h_ref[pl.ds(r, rows), :] = (xf * lax.rsqrt(ms + EPS) * g_ref[...]).astype(BF16)
        return carry
    lax.fori_loop(0, x_ref.shape[0] // rows, body, 0)


def _swiglu_block(h, wg, wu, wd):
    g = _dot(h, wg)
    u = _dot(h, wu)
    a = (g * jax.nn.sigmoid(g) * u).astype(BF16)
    return _dot(a, wd)


def _ffn_kernel(x_ref, g_ref, wg_ref, wu_ref, wd_ref, o_ref, h_ref):
    f = pl.program_id(1)

    @pl.when(f == 0)
    def _():
        _rmsnorm_rows(x_ref, g_ref, h_ref, 128)

    y = _swiglu_block(h_ref[...], wg_ref[...], wu_ref[...], wd_ref[...])

    @pl.when(f == 0)
    def _():
        o_ref[...] = x_ref[...] + y

    @pl.when(f > 0)
    def _():
        o_ref[...] += y


def _ffn(x, g, wg, wu, wd):
    m, d = x.shape
    ff = wg.shape[1]
    bm, bf = min(FFN_ROW_TILE, m), min(FFN_COL_TILE, ff)
    assert m % bm == 0 and ff % bf == 0
    return pl.pallas_call(
        _ffn_kernel,
        grid=(m // bm, ff // bf),
        in_specs=[
            pl.BlockSpec((bm, d), lambda i, f: (i, 0)),
            pl.BlockSpec((1, d), lambda i, f: (0, 0)),
            pl.BlockSpec((d, bf), lambda i, f: (0, f)),
            pl.BlockSpec((d, bf), lambda i, f: (0, f)),
            pl.BlockSpec((bf, d), lambda i, f: (f, 0)),
        ],
        out_specs=pl.BlockSpec((bm, d), lambda i, f: (i, 0)),
        out_shape=jax.ShapeDtypeStruct((m, d), F32),
        scratch_shapes=[pltpu.VMEM((bm, d), BF16)],
        compiler_params=_cparams("parallel", "arbitrary"),
        name="ffn",
    )(x, g, wg, wu, wd)


def _to_slabs(slab_ref, x):
    rows, d = x.shape
    sl = d // LANES
    for s in range(sl):
        slab_ref[pl.ds(s, rows, stride=sl), :] = x[:, s * LANES:(s + 1) * LANES]


def _from_slabs(slab_ref, rows, d):
    sl = d // LANES
    return jnp.concatenate([slab_ref[pl.ds(s, rows, stride=sl), :] for s in range(sl)], axis=1)


def _router_kernel(x_ref, g_ref, wr_ref, hs_ref, route_ref, hb_ref):
    _rmsnorm_rows(x_ref, g_ref, hb_ref, 128)
    hb = hb_ref[...]
    _to_slabs(hs_ref, hb.astype(F32))
    logits = _dot(hb, wr_ref[...])
    lane = lax.broadcasted_iota(I32, logits.shape, 1)
    lane_f = lane.astype(F32)
    neg = jnp.float32(-jnp.inf)
    l1 = jnp.where(lane < N_EXPERTS, logits, neg)
    m1 = jnp.max(l1, axis=-1, keepdims=True)
    i1 = jnp.min(jnp.where(l1 == m1, lane_f, float(LANES)), axis=-1, keepdims=True)
    l2 = jnp.where(lane_f == i1, neg, l1)
    m2 = jnp.max(l2, axis=-1, keepdims=True)
    i2 = jnp.min(jnp.where(l2 == m2, lane_f, float(LANES)), axis=-1, keepdims=True)
    e = jnp.exp(m2 - m1)
    den = 1.0 + e
    route = jnp.where(lane == 0, i1, 0.0)
    route = jnp.where(lane == 1, i2, route)
    route = jnp.where(lane == 2, 1.0 / den, route)
    route = jnp.where(lane == 3, e / den, route)
    route_ref[...] = route


def _router(x, g, wr):
    m, d = x.shape
    bm = min(FFN_ROW_TILE, m)
    sl = d // LANES
    assert m % bm == 0 and sl % 8 == 0
    return pl.pallas_call(
        _router_kernel,
        grid=(m // bm,),
        in_specs=[
            pl.BlockSpec((bm, d), lambda i: (i, 0)),
            pl.BlockSpec((1, d), lambda i: (0, 0)),
            pl.BlockSpec((d, LANES), lambda i: (0, 0)),
        ],
        out_specs=[
            pl.BlockSpec((bm * sl, LANES), lambda i: (i, 0)),
            pl.BlockSpec((bm, LANES), lambda i: (i, 0)),
        ],
        out_shape=[jax.ShapeDtypeStruct((m * sl, LANES), F32), jax.ShapeDtypeStruct((m, LANES), F32)],
        scratch_shapes=[pltpu.VMEM((bm, d), BF16)],
        compiler_params=_cparams("parallel"),
        name="router",
    )(x, g, wr)


def _route_plan_kernel(route_ref, pos_ref, meta_ref, rank_ref, *, tile, n_tiles, blk):
    m = route_ref.shape[0]
    nb = m // blk
    lane_b = lax.broadcasted_iota(I32, (blk, LANES), 1)
    lane_r = lax.broadcasted_iota(I32, (1, LANES), 1)
    ne = N_EXPERTS

    def onehot(r):
        rt = route_ref[pl.ds(r, blk), :]
        i1 = rt[:, 0:1].astype(I32)
        i2 = rt[:, 1:2].astype(I32)
        return jnp.where((lane_b == i1) | (lane_b == i2 + ne), 1.0, 0.0)

    strict = jnp.where(lax.broadcasted_iota(I32, (blk, blk), 0) > lax.broadcasted_iota(I32, (blk, blk), 1),
                       1.0, 0.0).astype(BF16)

    def rank_body(c, carry):
        r = pl.multiple_of(c * blk, blk)
        oh = onehot(r)
        rank_ref[pl.ds(r, blk), :] = _dot(strict, oh.astype(BF16)) + carry
        return carry + jnp.sum(oh, axis=0, keepdims=True)

    cnt12 = lax.fori_loop(0, nb, rank_body, jnp.zeros((1, LANES), F32))

    def lane_val(row, l):
        return jnp.sum(jnp.where(lane_r == l, row, 0.0), axis=-1, keepdims=True)

    cnt = jnp.zeros((1, LANES), F32)
    for e in range(ne):
        cnt = jnp.where(lane_r == e, lane_val(cnt12, e) + lane_val(cnt12, e + ne), cnt)
    tiles = jnp.floor((cnt + (tile - 1)) * (1.0 / tile))
    start = jnp.zeros((1, LANES), F32)
    for e in range(1, ne):
        start = start + jnp.where(lane_r >= e, lane_val(tiles, e - 1), 0.0)
    start = jnp.where(lane_r < ne, start * tile, 0.0)
    end = start + tiles * tile
    base = start
    for e in range(ne):
        base = jnp.where(lane_r == e + ne, lane_val(start, e) + lane_val(cnt12, e), base)

    def pos_body(c, carry):
        r = pl.multiple_of(c * blk, blk)
        slot = onehot(r) * (rank_ref[pl.ds(r, blk), :] + base)
        p1 = jnp.sum(jnp.where(lane_b < ne, slot, 0.0), axis=-1, keepdims=True)
        p2 = jnp.sum(jnp.where(lane_b >= ne, slot, 0.0), axis=-1, keepdims=True)
        both = jnp.where(lane_b == 0, p1, jnp.where(lane_b == 1, p2, 0.0))
        pos_ref[:, pl.ds(r, blk)] = jnp.transpose(both)[0:8, :].astype(I32)
        return carry

    lax.fori_loop(0, nb, pos_body, 0)

    n_used = lane_val(end, ne - 1) * (1.0 / tile)
    tile_start = lane_r.astype(F32) * tile
    tile_e = jnp.zeros((1, LANES), F32)
    last_e = jnp.zeros((1, 1), F32)
    for e in range(ne):
        tile_e = tile_e + jnp.where(lane_val(end, e) <= tile_start, 1.0, 0.0)
        last_e = jnp.where(lane_val(cnt, e) > 0, float(e), last_e)
    tile_e = jnp.where(lane_r.astype(F32) < n_used, tile_e, last_e)
    info = jnp.where(lane_r < ne, end, jnp.where(lane_r == 2 * ne, n_used, 0.0))
    for e in range(ne):
        info = jnp.where(lane_r == e + ne, lane_val(cnt, e), info)
    row8 = lax.broadcasted_iota(I32, (8, LANES), 0)
    meta = jnp.where(row8 == 0, tile_e, jnp.where(row8 == 1, info, jnp.where(row8 == 2, n_used, 0.0)))
    meta_ref[...] = meta.astype(I32)


def _route_plan(route, tile, n_tiles):
    m = route.shape[0]
    blk = min(PLAN_TILE, m)
    assert m % blk == 0 and n_tiles <= LANES
    return pl.pallas_call(
        functools.partial(_route_plan_kernel, tile=tile, n_tiles=n_tiles, blk=blk),
        out_shape=[jax.ShapeDtypeStruct((8, m), I32), jax.ShapeDtypeStruct((8, LANES), I32)],
        scratch_shapes=[pltpu.VMEM((m, LANES), F32)],
        compiler_params=pltpu.CompilerParams(vmem_limit_bytes=VMEM_LIMIT_BYTES),
        name="route_plan",
    )(route)


def _dispatch_kernel(meta_ref, pos_ref, h_ref, xs_ref, zero_ref, sem, *, tile, sl):
    i = pl.program_id(0)
    tt = h_ref.shape[0] // sl
    n_tiles = xs_ref.shape[0] // (tile * sl)

    def zero_copy(e):
        first = pl.multiple_of((meta_ref[e] - tile) * sl, tile * sl)
        return pltpu.make_async_copy(zero_ref, xs_ref.at[pl.ds(first, tile * sl), :], sem.at[1])

    def tail_copy(j):
        return pltpu.make_async_copy(zero_ref, xs_ref.at[pl.ds(j * tile * sl, tile * sl), :], sem.at[1])

    @pl.when(i == 0)
    def _():
        zero_ref[...] = jnp.zeros_like(zero_ref)
        n_used = meta_ref[2 * N_EXPERTS]
        for e in range(N_EXPERTS):
            @pl.when(meta_ref[N_EXPERTS + e] > 0)
            def _():
                zero_copy(e).start()
        for j in range(n_tiles - N_EXPERTS, n_tiles):
            @pl.when(j >= n_used)
            def _():
                tail_copy(j).start()
        for e in range(N_EXPERTS):
            @pl.when(meta_ref[N_EXPERTS + e] > 0)
            def _():
                zero_copy(e).wait()
        for j in range(n_tiles - N_EXPERTS, n_tiles):
            @pl.when(j >= n_used)
            def _():
                tail_copy(j).wait()

    def row_copy(r, c):
        src = pl.multiple_of(r * sl, sl)
        dst = pl.multiple_of(pos_ref[0, c * tt + r] * sl, sl)
        return pltpu.make_async_copy(h_ref.at[pl.ds(src, sl), :], xs_ref.at[pl.ds(dst, sl), :], sem.at[0])

    def start_body(r, carry):
        row_copy(r, 0).start()
        row_copy(r, 1).start()
        return carry

    lax.fori_loop(0, tt, start_body, 0, unroll=8)
    for _ in range(2):
        pltpu.make_async_copy(h_ref, xs_ref.at[pl.ds(0, tt * sl), :], sem.at[0]).wait()


def _dispatch(hs, pos_tiles, ginfo, n_slots, tile, sl):
    m = hs.shape[0] // sl
    tt = pos_tiles.shape[2] // 2
    return pl.pallas_call(
        functools.partial(_dispatch_kernel, tile=tile, sl=sl),
        grid_spec=pltpu.PrefetchScalarGridSpec(
            num_scalar_prefetch=1,
            grid=(m // tt,),
            in_specs=[
                pl.BlockSpec((None, 1, 2 * tt), lambda i, meta: (i, 0, 0), memory_space=pltpu.SMEM),
                pl.BlockSpec((tt * sl, LANES), lambda i, meta: (i, 0)),
            ],
            out_specs=pl.BlockSpec(memory_space=pl.ANY),
            scratch_shapes=[pltpu.VMEM((tile * sl, LANES), F32), pltpu.SemaphoreType.DMA((2,))],
        ),
        out_shape=jax.ShapeDtypeStruct((n_slots * sl, LANES), F32),
        compiler_params=_cparams("arbitrary"),
        name="dispatch",
    )(ginfo, pos_tiles, hs)


def _expert_ffn_kernel(te_ref, nu_ref, xs_ref, wg_ref, wu_ref, wd_ref, y_ref, h_ref, acc_ref):
    i, f = pl.program_id(0), pl.program_id(1)
    used = i < nu_ref[0]
    last = pl.num_programs(1) - 1
    tile, d = h_ref.shape

    @pl.when(used & (f == 0))
    def _():
        h_ref[...] = _from_slabs(xs_ref, tile, d).astype(BF16)

    @pl.when(used)
    def _():
        y = _swiglu_block(h_ref[...], wg_ref[...], wu_ref[...], wd_ref[...])

        @pl.when(f == 0)
        def _():
            acc_ref[...] = y

        @pl.when(f > 0)
        def _():
            acc_ref[...] += y

    @pl.when(used & (f == last))
    def _():
        _to_slabs(y_ref, acc_ref[...])

    @pl.when(jnp.logical_not(used) & (f == last))
    def _():
        y_ref[...] = jnp.zeros_like(y_ref)


def _expert_ffn(xs, tile_e, n_used, wg, wu, wd, tile, sl):
    n_slots = xs.shape[0] // sl
    d = sl * LANES
    ff = wg.shape[2]
    bf = min(FFN_COL_TILE, ff)
    assert n_slots % tile == 0 and ff % bf == 0
    nf = ff // bf
    row = lambda i, f, te, nu: (jnp.minimum(i, nu[0] - 1), 0)
    fblk = lambda i, f, nu: jnp.where(i < nu[0], f, nf - 1)
    return pl.pallas_call(
        _expert_ffn_kernel,
        grid_spec=pltpu.PrefetchScalarGridSpec(
            num_scalar_prefetch=2,
            grid=(n_slots // tile, nf),
            in_specs=[
                pl.BlockSpec((tile * sl, LANES), row),
                pl.BlockSpec((None, d, bf), lambda i, f, te, nu: (te[i], 0, fblk(i, f, nu))),
                pl.BlockSpec((None, d, bf), lambda i, f, te, nu: (te[i], 0, fblk(i, f, nu))),
                pl.BlockSpec((None, bf, d), lambda i, f, te, nu: (te[i], fblk(i, f, nu), 0)),
            ],
            out_specs=pl.BlockSpec((tile * sl, LANES), lambda i, f, te, nu: (i, 0)),
            scratch_shapes=[pltpu.VMEM((tile, d), BF16), pltpu.VMEM((tile, d), F32)],
        ),
        out_shape=jax.ShapeDtypeStruct((n_slots * sl, LANES), F32),
        compiler_params=_cparams("parallel", "arbitrary"),
        name="expert_ffn",
    )(tile_e, n_used, xs, wg, wu, wd)


def _combine_kernel(pos_ref, x_ref, route_ref, g_ref, y_ref, o_ref, buf_ref, sem):
    tt, d = x_ref.shape
    sl = d // LANES

    def row_copy(r, c):
        src = pl.multiple_of(pos_ref[0, c * tt + r] * sl, sl)
        dst = pl.multiple_of(r * sl, sl)
        return pltpu.make_async_copy(y_ref.at[pl.ds(src, sl), :], buf_ref.at[c, pl.ds(dst, sl), :], sem.at[0])

    def start_body(r, carry):
        row_copy(r, 0).start()
        row_copy(r, 1).start()
        return carry

    lax.fori_loop(0, tt, start_body, 0, unroll=8)
    for c in range(2):
        pltpu.make_async_copy(y_ref.at[pl.ds(0, tt * sl), :], buf_ref.at[c], sem.at[0]).wait()

    rt = route_ref[...]
    moe = rt[:, 2:3] * _from_slabs(buf_ref.at[0], tt, d) + rt[:, 3:4] * _from_slabs(buf_ref.at[1], tt, d)
    xo = x_ref[...] + moe
    ms = jnp.mean(xo * xo, axis=-1, keepdims=True)
    o_ref[...] = xo * lax.rsqrt(ms + EPS) * g_ref[...]


def _combine(x, route, g, y, pos_tiles):
    m, d = x.shape
    tt = pos_tiles.shape[2] // 2
    return pl.pallas_call(
        _combine_kernel,
        grid=(m // tt,),
        in_specs=[
            pl.BlockSpec((None, 1, 2 * tt), lambda i: (i, 0, 0), memory_space=pltpu.SMEM),
            pl.BlockSpec((tt, d), lambda i: (i, 0)),
            pl.BlockSpec((tt, LANES), lambda i: (i, 0)),
            pl.BlockSpec((1, d), lambda i: (0, 0)),
            pl.BlockSpec(memory_space=pl.ANY),
        ],
        out_specs=pl.BlockSpec((tt, d), lambda i: (i, 0)),
        out_shape=jax.ShapeDtypeStruct((m, d), F32),
        scratch_shapes=[pltpu.VMEM((2, tt * (d // LANES), LANES), F32), pltpu.SemaphoreType.DMA((1,))],
        compiler_params=_cparams("arbitrary"),
        name="combine",
    )(pos_tiles, x, route, g, y)


def _moe_ffn_final(x, g_ffn, wr, wg, wu, wd, g_final):
    m, d = x.shape
    tile = min(FFN_ROW_TILE, m)
    n_tiles = (2 * m) // tile + N_EXPERTS
    tt = min(TOK_TILE, m)
    sl = d // LANES
    hs, route = _router(x, g_ffn, wr)
    pos, meta = _route_plan(route, tile, n_tiles)
    pos_tiles = pos[:2].reshape(2, m // tt, tt).transpose(1, 0, 2).reshape(m // tt, 1, 2 * tt)
    xs = _dispatch(hs, pos_tiles, meta[1, :2 * N_EXPERTS + 1], n_tiles * tile, tile, sl)
    y = _expert_ffn(xs, meta[0, :n_tiles], meta[2, :1], wg, wu, wd, tile, sl)
    return _combine(x, route, g_final, y, pos_tiles)


def kernel(x, ln_mix, w_in, w_alpha, b_alpha, gla_norm, w_pool, pool_scale, w_branch_gla, w_branch_pool, w_out,
           ln_ffn, ffn_w_gate, ffn_w_up, ffn_w_down, router_w, exp_w_gate, exp_w_up, exp_w_down, ln_final):
    batch, seq, d = x.shape
    depth = w_in.shape[0]
    dk = w_alpha.shape[2]
    dv = w_branch_gla.shape[1]
    pw = w_branch_pool.shape[1]
    rank = w_alpha.shape[1]
    assert depth == 2 and rank <= LANES
    a_col = 2 * dk + 2 * dv
    m = batch * seq
    xf = x.reshape(m, d)
    row = lambda v: v.reshape(1, -1)

    for l in range(depth):
        w_main, w_a1 = _inproj_weight(w_in, l, a_col, rank)
        w_al = jnp.pad(w_alpha[l], ((0, LANES - rank), (0, 0)))
        z, a1 = _norm_inproj(xf, row(ln_mix[l]), w_main, w_a1)
        og = _gla(z, a1, w_al, row(b_alpha[l]), row(gla_norm[l]), batch, seq, dk, dv)
        yp = _pool(z, _to_bf16(w_pool, l), row(pool_scale[l]), batch, seq, a_col)
        u = _merge(og, yp, z, _to_bf16(w_branch_gla, l), _to_bf16(w_branch_pool, l), a_col + pw, a_col + pw + d)
        xf = _out_proj(u, _to_bf16(w_out, l), xf)
        if l % 2 == 0:
            i = l // 2
            xf = _ffn(xf, row(ln_ffn[l]), _to_bf16(ffn_w_gate, i), _to_bf16(ffn_w_up, i), _to_bf16(ffn_w_down, i))
        else:
            i = l // 2
            wr = jnp.pad(router_w[i], ((0, 0), (0, LANES - N_EXPERTS))).astype(BF16)
            xf = _moe_ffn_final(xf, row(ln_ffn[l]), wr, _to_bf16(exp_w_gate, i), _to_bf16(exp_w_up, i),
                                _to_bf16(exp_w_down, i), ln_final.reshape(1, -1))
    return xf.reshape(batch, seq, d)
```

```python
import functools
import math

import jax
import jax.numpy as jnp
from jax import lax
from jax.experimental import pallas as pl
from jax.experimental.pallas import tpu as pltpu

F32 = jnp.float32
BF16 = jnp.bfloat16
I32 = jnp.int32

EPS = 1e-6
GLA_HEADS = 4
GLA_LOWRANK = 16
GLA_GATE_TEMP = 16.0
GLA_CHUNK = 64
GLA_HEADS_PER_STEP = 4
POOL_WINDOWS = (2, 4, 8, 16)
POOL_GROUPS = 4
N_EXPERTS = 8

LANES = 128
VMEM_LIMIT_BYTES = 56 * 1024 * 1024

ROW_TILE = 1024
COL_TILE = 1024
FFN_ROW_TILE = 512
DENSE_FFN_ROW_TILE = 1024
FFN_COL_TILE = 512
SEQ_TILE = 512
TOK_TILE = 256
PLAN_TILE = 512
HALO = 16


def _cparams(*sem):
    return pltpu.CompilerParams(dimension_semantics=sem, vmem_limit_bytes=VMEM_LIMIT_BYTES)


def _dot(a, b):
    return jnp.dot(a, b, preferred_element_type=F32)


def _dot_nt(a, b):
    return lax.dot_general(a, b, (((1,), (1,)), ((), ())), preferred_element_type=F32)


def _split_bf16(x):
    hi = x.astype(BF16)
    lo = (x - hi.astype(F32)).astype(BF16)
    return hi, lo


def _pick(n, target, align=LANES):
    if n <= target:
        return n
    best = None
    for cand in range(align, target + 1, align):
        if n % cand == 0:
            best = cand
    assert best is not None, (n, target, align)
    return best


def _cast_kernel(x_ref, o_ref):
    o_ref[...] = x_ref[...].astype(BF16)


def _to_bf16(w, layer):
    shape = w.shape[1:]
    w3 = w.reshape(w.shape[0], -1, shape[-1])
    _, r, c = w3.shape
    br, bc = _pick(r, 1024, 16), _pick(c, 2048)
    out = pl.pallas_call(
        _cast_kernel,
        grid=(r // br, c // bc),
        in_specs=[pl.BlockSpec((None, br, bc), lambda i, j: (layer, i, j))],
        out_specs=pl.BlockSpec((br, bc), lambda i, j: (i, j)),
        out_shape=jax.ShapeDtypeStruct((r, c), BF16),
        compiler_params=_cparams("parallel", "parallel"),
        name="to_bf16",
    )(w3)
    return out.reshape(shape)


def _inproj_weight_kernel(w_ref, nxt_ref, lr_ref, wm_ref, wa_ref, *, first_shifted, rank):
    j = pl.program_id(0)

    @pl.when(j < first_shifted)
    def _():
        wm_ref[...] = w_ref[...].astype(BF16)

    @pl.when(j >= first_shifted)
    def _():
        wm_ref[...] = jnp.concatenate([w_ref[rank:, :], nxt_ref[...]], axis=0).astype(BF16)

    @pl.when(j == 0)
    def _():
        pad = jnp.zeros((wa_ref.shape[0] - rank, wa_ref.shape[1]), F32)
        wa_ref[...] = jnp.concatenate([lr_ref[...], pad], axis=0).astype(BF16)


def _inproj_weight(w_t, layer, a_col, rank):
    _, n_in, d = w_t.shape
    n = n_in - rank
    br = _pick(math.gcd(n, a_col), 512)
    assert a_col % br == 0 and rank % 8 == 0 and rank < LANES and br % rank == 0
    per = br // rank
    return pl.pallas_call(
        functools.partial(_inproj_weight_kernel, first_shifted=a_col // br, rank=rank),
        grid=(n // br,),
        in_specs=[
            pl.BlockSpec((None, br, d), lambda j: (layer, j, 0)),
            pl.BlockSpec((None, rank, d), lambda j: (layer, (j + 1) * per, 0)),
            pl.BlockSpec((None, rank, d), lambda j: (layer, a_col // rank, 0)),
        ],
        out_specs=[
            pl.BlockSpec((br, d), lambda j: (j, 0)),
            pl.BlockSpec((LANES, d), lambda j: (0, 0)),
        ],
        out_shape=[jax.ShapeDtypeStruct((n, d), BF16), jax.ShapeDtypeStruct((LANES, d), BF16)],
        compiler_params=_cparams("arbitrary"),
        name="inproj_weight",
    )(w_t, w_t, w_t)


def _norm_inproj_kernel(x_ref, g_ref, w_ref, wa_ref, z_ref, a1_ref, h_ref, *, rows):
    @pl.when(pl.program_id(1) == 0)
    def _():
        def body(c, carry):
            r = pl.multiple_of(c * rows, rows)
            xf = x_ref[pl.ds(r, rows), :]
            ms = jnp.mean(xf * xf, axis=-1, keepdims=True)
            h_ref[pl.ds(r, rows), :] = (xf * lax.rsqrt(ms + EPS) * g_ref[...]).astype(BF16)
            return carry
        lax.fori_loop(0, x_ref.shape[0] // rows, body, 0)
        a1_ref[...] = _dot_nt(h_ref[...], wa_ref[...])

    z_ref[...] = _dot_nt(h_ref[...], w_ref[...]).astype(BF16)


def _norm_inproj(x, g, w_t, wa_t):
    m, d = x.shape
    n = w_t.shape[0]
    bm, bn = min(ROW_TILE, m), _pick(n, COL_TILE)
    assert m % bm == 0 and n % bn == 0
    return pl.pallas_call(
        functools.partial(_norm_inproj_kernel, rows=128),
        grid=(m // bm, n // bn),
        in_specs=[
            pl.BlockSpec((bm, d), lambda i, j: (i, 0)),
            pl.BlockSpec((1, d), lambda i, j: (0, 0)),
            pl.BlockSpec((bn, d), lambda i, j: (j, 0)),
            pl.BlockSpec((LANES, d), lambda i, j: (0, 0)),
        ],
        out_specs=[
            pl.BlockSpec((bm, bn), lambda i, j: (i, j)),
            pl.BlockSpec((bm, LANES), lambda i, j: (i, 0)),
        ],
        out_shape=[jax.ShapeDtypeStruct((m, n), BF16), jax.ShapeDtypeStruct((m, LANES), F32)],
        scratch_shapes=[pltpu.VMEM((bm, d), BF16)],
        compiler_params=_cparams("parallel", "arbitrary"),
        name="norm_inproj",
    )(x, g, w_t, wa_t)


def _log_sigmoid(x):
    return jnp.minimum(x, 0.0) - jnp.log(1.0 + jnp.exp(-jnp.abs(x)))


def _gla_kernel(q_ref, k_ref, v_ref, r_ref, a1_ref, wal_ref, bal_ref, gn_ref, tri_ref, o_ref, st_ref, *,
                chunk, heads):
    @pl.when(pl.program_id(2) == 0)
    def _():
        st_ref[...] = jnp.zeros_like(st_ref)

    hk = q_ref.shape[1] // heads
    hv = v_ref.shape[1] // heads
    a1 = _split_bf16(a1_ref[...])
    for h in range(heads):
        ks, vs = slice(h * hk, (h + 1) * hk), slice(h * hv, (h + 1) * hv)
        o, st_new = _gla_head(q_ref[:, ks], k_ref[:, ks], v_ref[:, vs], r_ref[:, vs], a1, wal_ref[:, ks],
                              bal_ref[:, ks], gn_ref[...], tri_ref[...], st_ref[h], chunk)
        o_ref[:, vs] = o
        st_ref[h] = st_new


def _gla_head(q, k, v, r, a1, wal, bal, gn, tri, st, chunk):
    t, dk = q.shape
    nc = t // chunk
    pair = 2 * chunk
    shift = chunk.bit_length() - 1
    nt_dot = lambda a, bb: lax.dot_general(a, bb, (((1,), (1,)), ((), ())), preferred_element_type=F32)
    rows = lambda x, c0, c1: x[c0 * chunk:c1 * chunk, :]

    a_hi, a_lo = a1
    w_hi, w_lo = _split_bf16(wal)
    xg = _dot(a_hi, w_hi) + _dot(a_hi, w_lo) + _dot(a_lo, w_hi) + bal
    la = _log_sigmoid(xg) * (1.0 / GLA_GATE_TEMP)

    la_hi, la_lo = _split_bf16(la)
    b = _dot(tri, la_hi) + _dot(tri, la_lo)

    bl = [b[(c + 1) * chunk - 1:(c + 1) * chunk, :] for c in range(nc)]
    pre = [jnp.zeros_like(bl[0])]
    for c in range(nc):
        pre.append(pre[c] + bl[c])

    q = q.astype(F32)
    k = k.astype(F32)
    q_t = q * jnp.exp(b) * (dk ** -0.5)
    k_t = (k * jnp.exp(-b)).astype(BF16)
    qc = [rows(q_t, c, c + 1) for c in range(nc)]
    kec = [rows(k, c, c + 1) * jnp.exp(bl[c] - rows(b, c, c + 1)) for c in range(nc)]
    q_tb = q_t.astype(BF16)
    k_eb = jnp.concatenate(kec, axis=0).astype(BF16)

    def q_from(mid, hi):
        return jnp.concatenate([qc[c] if c == mid else qc[c] * jnp.exp(pre[c] - pre[mid])
                                for c in range(mid, hi)], axis=0).astype(BF16)

    def k_upto(lo, mid):
        return jnp.concatenate([kec[m] if m == mid - 1 else kec[m] * jnp.exp(pre[mid] - pre[m + 1])
                                for m in range(lo, mid)], axis=0).astype(BF16)

    nb = t // pair
    blocks = [[None] * nb for _ in range(nb)]
    r2 = lax.broadcasted_iota(I32, (pair, pair), 0)
    c2 = lax.broadcasted_iota(I32, (pair, pair), 1)
    same = ((r2 >> shift) == (c2 >> shift)) & (r2 >= c2)
    cross = (r2 >= chunk) & (c2 < chunk)
    for p in range(nb):
        qp = rows(q_tb, 2 * p, 2 * p + 2)
        in_chunk = nt_dot(qp, rows(k_t, 2 * p, 2 * p + 2))
        next_chunk = nt_dot(qp, rows(k_eb, 2 * p, 2 * p + 2))
        blocks[p][p] = jnp.where(same, in_chunk, jnp.where(cross, next_chunk, 0.0))
    g = 4
    while g <= nc:
        half = g // 2
        for grp in range(nc // g):
            lo, mid, hi = grp * g, grp * g + half, (grp + 1) * g
            x = nt_dot(q_from(mid, hi), k_upto(lo, mid))
            hb = half // 2
            for i in range(hb):
                for j in range(hb):
                    blocks[mid // 2 + i][lo // 2 + j] = x[i * pair:(i + 1) * pair, j * pair:(j + 1) * pair]
        g *= 2
    zero_blk = jnp.zeros((pair, pair), F32)
    att = jnp.concatenate(
        [jnp.concatenate([zero_blk if blk is None else blk for blk in brow], axis=1) for brow in blocks], axis=0)

    o = _dot(att.astype(BF16), v) + nt_dot(q_from(0, nc), st.astype(BF16))
    upd = lax.dot_general(v, k_upto(0, nc), (((0,), (0,)), ((), ())), preferred_element_type=F32)
    st_new = st * jnp.exp(pre[nc]) + upd

    o = o * lax.rsqrt(jnp.mean(o * o, axis=-1, keepdims=True) + EPS) * gn
    r = r.astype(F32)
    return (o * (r * jax.nn.sigmoid(r))).astype(BF16), st_new


def _gla(z, a1, w_alpha, b_alpha, gla_norm, batch, seq, dk_total, dv_total):
    m = z.shape[0]
    hk, hv = dk_total // GLA_HEADS, dv_total // GLA_HEADS
    t = min(SEQ_TILE, seq)
    assert seq % t == 0 and t % GLA_CHUNK == 0 and hk % LANES == 0 and hv % LANES == 0
    ns = seq // t
    nc = t // GLA_CHUNK
    assert nc >= 2 and nc & (nc - 1) == 0 and GLA_HEADS % GLA_HEADS_PER_STEP == 0
    hps = GLA_HEADS_PER_STEP
    gk, gv = hps * hk, hps * hv
    k_off = dk_total // gk
    v_off = (2 * dk_total) // gv
    r_off = (2 * dk_total + dv_total) // gv
    idx = jnp.arange(t, dtype=I32)
    tri = ((idx[:, None] // GLA_CHUNK == idx[None, :] // GLA_CHUNK) & (idx[:, None] >= idx[None, :])).astype(BF16)
    rowi = lambda b, h, s: b * ns + s
    return pl.pallas_call(
        functools.partial(_gla_kernel, chunk=GLA_CHUNK, heads=hps),
        grid=(batch, GLA_HEADS // hps, ns),
        in_specs=[
            pl.BlockSpec((t, gk), lambda b, h, s: (rowi(b, h, s), h)),
            pl.BlockSpec((t, gk), lambda b, h, s: (rowi(b, h, s), k_off + h)),
            pl.BlockSpec((t, gv), lambda b, h, s: (rowi(b, h, s), v_off + h)),
            pl.BlockSpec((t, gv), lambda b, h, s: (rowi(b, h, s), r_off + h)),
            pl.BlockSpec((t, LANES), lambda b, h, s: (rowi(b, h, s), 0)),
            pl.BlockSpec((LANES, gk), lambda b, h, s: (0, h)),
            pl.BlockSpec((1, gk), lambda b, h, s: (0, h)),
            pl.BlockSpec((1, hv), lambda b, h, s: (0, 0)),
            pl.BlockSpec((t, t), lambda b, h, s: (0, 0)),
        ],
        out_specs=pl.BlockSpec((t, gv), lambda b, h, s: (rowi(b, h, s), h)),
        out_shape=jax.ShapeDtypeStruct((m, dv_total), BF16),
        scratch_shapes=[pltpu.VMEM((hps, hv, hk), F32)],
        compiler_params=_cparams("parallel", "parallel", "arbitrary"),
        name="gla",
    )(z, z, z, z, a1, w_alpha, b_alpha, gla_norm, tri)


def _pool_kernel(p_ref, ph_ref, wp_ref, sc_ref, y_ref, *, windows):
    s = pl.program_id(1)
    t = p_ref.shape[0]
    halo = ph_ref.shape[0]
    gw = wp_ref.shape[1]
    dist = lax.broadcasted_iota(I32, (t, t), 0) - lax.broadcasted_iota(I32, (t, t), 1)
    dist_h = lax.broadcasted_iota(I32, (t, halo), 0) + halo - lax.broadcasted_iota(I32, (t, halo), 1)
    pos = s * t + lax.broadcasted_iota(I32, (t, 1), 0)
    for g, w in enumerate(windows):
        cs = slice(g * gw, (g + 1) * gw)
        pc = p_ref[:, cs]
        band = jnp.where((dist >= 0) & (dist < w), 1.0, 0.0).astype(BF16)
        band_h = jnp.where(dist_h < w, 1.0, 0.0).astype(BF16)
        tot = _dot(band, pc) + jnp.where(s > 0, _dot(band_h, ph_ref[:, cs]), 0.0)
        cnt = jnp.minimum(pos + 1, w).astype(F32)
        mixed = tot / cnt - pc.astype(F32)
        y = _dot(mixed.astype(BF16), wp_ref[g]) * sc_ref[:, cs]
        y_ref[:, cs] = y.astype(BF16)


def _pool(z, w_pool, pool_scale, batch, seq, p_col0):
    m = z.shape[0]
    groups, gw, _ = w_pool.shape
    pw = groups * gw
    t = min(SEQ_TILE, seq)
    assert seq % t == 0 and t % HALO == 0 and p_col0 % pw == 0 and gw % LANES == 0
    ns = seq // t
    pc = p_col0 // pw
    hb = t // HALO
    return pl.pallas_call(
        functools.partial(_pool_kernel, windows=POOL_WINDOWS),
        grid=(batch, ns),
        in_specs=[
            pl.BlockSpec((t, pw), lambda b, s: (b * ns + s, pc)),
            pl.BlockSpec((HALO, pw), lambda b, s: (jnp.maximum((b * ns + s) * hb - 1, 0), pc)),
            pl.BlockSpec((groups, gw, gw), lambda b, s: (0, 0, 0)),
            pl.BlockSpec((1, pw), lambda b, s: (0, 0)),
        ],
        out_specs=pl.BlockSpec((t, pw), lambda b, s: (b * ns + s, 0)),
        out_shape=jax.ShapeDtypeStruct((m, pw), BF16),
        compiler_params=_cparams("parallel", "arbitrary"),
        name="pool",
    )(z, z, w_pool, pool_scale)


def _merge_kernel(og_ref, yp_ref, ga_ref, gb_ref, wa_ref, wb_ref, u_ref):
    ya = _dot(og_ref[...], wa_ref[...])
    yb = _dot(yp_ref[...], wb_ref[...])
    ga = jax.nn.sigmoid(ga_ref[...].astype(F32))
    gb = jax.nn.sigmoid(gb_ref[...].astype(F32))
    u_ref[...] = (ga * ya + gb * yb).astype(BF16)


def _merge(og, yp, z, w_a, w_b, ga_col0, gb_col0):
    m, dv = og.shape
    pw = yp.shape[1]
    d = w_a.shape[1]
    bm, bn = min(ROW_TILE, m), _pick(math.gcd(d, ga_col0, gb_col0), COL_TILE)
    assert m % bm == 0 and d % bn == 0 and ga_col0 % bn == 0 and gb_col0 % bn == 0
    ga0, gb0 = ga_col0 // bn, gb_col0 // bn
    return pl.pallas_call(
        _merge_kernel,
        grid=(m // bm, d // bn),
        in_specs=[
            pl.BlockSpec((bm, dv), lambda i, j: (i, 0)),
            pl.BlockSpec((bm, pw), lambda i, j: (i, 0)),
            pl.BlockSpec((bm, bn), lambda i, j: (i, ga0 + j)),
            pl.BlockSpec((bm, bn), lambda i, j: (i, gb0 + j)),
            pl.BlockSpec((dv, bn), lambda i, j: (0, j)),
            pl.BlockSpec((pw, bn), lambda i, j: (0, j)),
        ],
        out_specs=pl.BlockSpec((bm, bn), lambda i, j: (i, j)),
        out_shape=jax.ShapeDtypeStruct((m, d), BF16),
        compiler_params=_cparams("parallel", "arbitrary"),
        name="merge",
    )(og, yp, z, z, w_a, w_b)


def _out_proj_kernel(u_ref, w_ref, x_ref, o_ref):
    o_ref[...] = x_ref[...] + _dot(u_ref[...], w_ref[...])


def _out_proj(u, w, x):
    m, d = x.shape
    bm, bn = min(ROW_TILE, m), _pick(d, COL_TILE)
    assert m % bm == 0 and d % bn == 0
    return pl.pallas_call(
        _out_proj_kernel,
        grid=(m // bm, d // bn),
        in_specs=[
            pl.BlockSpec((bm, u.shape[1]), lambda i, j: (i, 0)),
            pl.BlockSpec((u.shape[1], bn), lambda i, j: (0, j)),
            pl.BlockSpec((bm, bn), lambda i, j: (i, j)),
        ],
        out_specs=pl.BlockSpec((bm, bn), lambda i, j: (i, j)),
        out_shape=jax.ShapeDtypeStruct((m, d), F32),
        compiler_params=_cparams("parallel", "arbitrary"),
        name="out_proj",
    )(u, w, x)


def _rmsnorm_rows(x_ref, g_ref, h_ref, rows):
    def body(c, carry):
        r = pl.multiple_of(c * rows, rows)
        xf = x_ref[pl.ds(r, rows), :]
        ms = jnp.mean(xf * xf, axis=-1, keepdims=True)
        h_ref[pl.ds(r, rows), :] = (xf * lax.rsqrt(ms + EPS) * g_ref[...]).astype(BF16)
        return carry
    lax.fori_loop(0, x_ref.shape[0] // rows, body, 0)


def _swiglu_block(h, wg, wu, wd):
    g = _dot(h, wg)
    u = _dot(h, wu)
    a = (g * jax.nn.sigmoid(g) * u).astype(BF16)
    return _dot(a, wd)


def _ffn_kernel(x_ref, g_ref, wg_ref, wu_ref, wd_ref, o_ref, h_ref):
    f = pl.program_id(1)

    @pl.when(f == 0)
    def _():
        _rmsnorm_rows(x_ref, g_ref, h_ref, 128)

    y = _swiglu_block(h_ref[...], wg_ref[...], wu_ref[...], wd_ref[...])

    @pl.when(f == 0)
    def _():
        o_ref[...] = x_ref[...] + y

    @pl.when(f > 0)
    def _():
        o_ref[...] += y


def _ffn(x, g, wg, wu, wd):
    m, d = x.shape
    ff = wg.shape[1]
    bm, bf = min(DENSE_FFN_ROW_TILE, m), min(FFN_COL_TILE, ff)
    assert m % bm == 0 and ff % bf == 0
    return pl.pallas_call(
        _ffn_kernel,
        grid=(m // bm, ff // bf),
        in_specs=[
            pl.BlockSpec((bm, d), lambda i, f: (i, 0), pipeline_mode=pl.Buffered(1)),
            pl.BlockSpec((1, d), lambda i, f: (0, 0)),
            pl.BlockSpec((d, bf), lambda i, f: (0, f)),
            pl.BlockSpec((d, bf), lambda i, f: (0, f)),
            pl.BlockSpec((bf, d), lambda i, f: (f, 0)),
        ],
        out_specs=pl.BlockSpec((bm, d), lambda i, f: (i, 0)),
        out_shape=jax.ShapeDtypeStruct((m, d), F32),
        scratch_shapes=[pltpu.VMEM((bm, d), BF16)],
        compiler_params=_cparams("parallel", "arbitrary"),
        name="ffn",
    )(x, g, wg, wu, wd)


U32 = jnp.uint32


def _to_slabs(slab_ref, x):
    rows, d = x.shape
    half = d // 2
    sl = half // LANES
    lo = lax.bitcast_convert_type(x[:, :half], U32) >> 16
    hi = (lax.bitcast_convert_type(x[:, half:], U32) >> 16) << 16
    words = hi | lo
    for s in range(sl):
        slab_ref[pl.ds(s, rows, stride=sl), :] = words[:, s * LANES:(s + 1) * LANES]


def _from_slabs(slab_ref, rows, d):
    sl = d // (2 * LANES)
    words = jnp.concatenate([slab_ref[pl.ds(s, rows, stride=sl), :] for s in range(sl)], axis=1)
    lo = lax.bitcast_convert_type(words << 16, F32)
    hi = lax.bitcast_convert_type((words >> 16) << 16, F32)
    return jnp.concatenate([lo, hi], axis=1)


def _router_kernel(x_ref, g_ref, wr_ref, hs_ref, route_ref, hb_ref):
    _rmsnorm_rows(x_ref, g_ref, hb_ref, 128)
    hb = hb_ref[...]
    _to_slabs(hs_ref, hb.astype(F32))
    logits = _dot(hb, wr_ref[...])
    lane = lax.broadcasted_iota(I32, logits.shape, 1)
    lane_f = lane.astype(F32)
    neg = jnp.float32(-jnp.inf)
    l1 = jnp.where(lane < N_EXPERTS, logits, neg)
    m1 = jnp.max(l1, axis=-1, keepdims=True)
    i1 = jnp.min(jnp.where(l1 == m1, lane_f, float(LANES)), axis=-1, keepdims=True)
    l2 = jnp.where(lane_f == i1, neg, l1)
    m2 = jnp.max(l2, axis=-1, keepdims=True)
    i2 = jnp.min(jnp.where(l2 == m2, lane_f, float(LANES)), axis=-1, keepdims=True)
    e = jnp.exp(m2 - m1)
    den = 1.0 + e
    route = jnp.where(lane == 0, i1, 0.0)
    route = jnp.where(lane == 1, i2, route)
    route = jnp.where(lane == 2, 1.0 / den, route)
    route = jnp.where(lane == 3, e / den, route)
    route_ref[...] = route


def _router(x, g, wr):
    m, d = x.shape
    bm = min(FFN_ROW_TILE, m)
    sl = d // (2 * LANES)
    assert m % bm == 0 and sl % 8 == 0
    return pl.pallas_call(
        _router_kernel,
        grid=(m // bm,),
        in_specs=[
            pl.BlockSpec((bm, d), lambda i: (i, 0)),
            pl.BlockSpec((1, d), lambda i: (0, 0)),
            pl.BlockSpec((d, LANES), lambda i: (0, 0)),
        ],
        out_specs=[
            pl.BlockSpec((bm * sl, LANES), lambda i: (i, 0)),
            pl.BlockSpec((bm, LANES), lambda i: (i, 0)),
        ],
        out_shape=[jax.ShapeDtypeStruct((m * sl, LANES), U32), jax.ShapeDtypeStruct((m, LANES), F32)],
        scratch_shapes=[pltpu.VMEM((bm, d), BF16)],
        compiler_params=_cparams("parallel"),
        name="router",
    )(x, g, wr)


def _route_plan_kernel(route_ref, pos_ref, meta_ref, rank_ref, *, tile, n_tiles, blk):
    m = route_ref.shape[0]
    nb = m // blk
    lane_b = lax.broadcasted_iota(I32, (blk, LANES), 1)
    lane_r = lax.broadcasted_iota(I32, (1, LANES), 1)
    ne = N_EXPERTS

    def onehot(r):
        rt = route_ref[pl.ds(r, blk), :]
        i1 = rt[:, 0:1].astype(I32)
        i2 = rt[:, 1:2].astype(I32)
        return jnp.where((lane_b == i1) | (lane_b == i2 + ne), 1.0, 0.0)

    strict = jnp.where(lax.broadcasted_iota(I32, (blk, blk), 0) > lax.broadcasted_iota(I32, (blk, blk), 1),
                       1.0, 0.0).astype(BF16)

    def rank_body(c, carry):
        r = pl.multiple_of(c * blk, blk)
        oh = onehot(r)
        rank_ref[pl.ds(r, blk), :] = _dot(strict, oh.astype(BF16)) + carry
        return carry + jnp.sum(oh, axis=0, keepdims=True)

    cnt12 = lax.fori_loop(0, nb, rank_body, jnp.zeros((1, LANES), F32))

    def lane_val(row, l):
        return jnp.sum(jnp.where(lane_r == l, row, 0.0), axis=-1, keepdims=True)

    cnt = jnp.zeros((1, LANES), F32)
    for e in range(ne):
        cnt = jnp.where(lane_r == e, lane_val(cnt12, e) + lane_val(cnt12, e + ne), cnt)
    tiles = jnp.floor((cnt + (tile - 1)) * (1.0 / tile))
    start = jnp.zeros((1, LANES), F32)
    for e in range(1, ne):
        start = start + jnp.where(lane_r >= e, lane_val(tiles, e - 1), 0.0)
    start = jnp.where(lane_r < ne, start * tile, 0.0)
    end = start + tiles * tile
    base = start
    for e in range(ne):
        base = jnp.where(lane_r == e + ne, lane_val(start, e) + lane_val(cnt12, e), base)

    def pos_body(c, carry):
        r = pl.multiple_of(c * blk, blk)
        slot = onehot(r) * (rank_ref[pl.ds(r, blk), :] + base)
        p1 = jnp.sum(jnp.where(lane_b < ne, slot, 0.0), axis=-1, keepdims=True)
        p2 = jnp.sum(jnp.where(lane_b >= ne, slot, 0.0), axis=-1, keepdims=True)
        both = jnp.where(lane_b == 0, p1, jnp.where(lane_b == 1, p2, 0.0))
        pos_ref[:, pl.ds(r, blk)] = jnp.transpose(both)[0:8, :].astype(I32)
        return carry

    lax.fori_loop(0, nb, pos_body, 0)

    n_used = lane_val(end, ne - 1) * (1.0 / tile)
    tile_start = lane_r.astype(F32) * tile
    tile_e = jnp.zeros((1, LANES), F32)
    last_e = jnp.zeros((1, 1), F32)
    for e in range(ne):
        tile_e = tile_e + jnp.where(lane_val(end, e) <= tile_start, 1.0, 0.0)
        last_e = jnp.where(lane_val(cnt, e) > 0, float(e), last_e)
    tile_e = jnp.where(lane_r.astype(F32) < n_used, tile_e, last_e)
    info = jnp.where(lane_r < ne, end, jnp.where(lane_r == 2 * ne, n_used, 0.0))
    for e in range(ne):
        info = jnp.where(lane_r == e + ne, lane_val(cnt, e), info)
    row8 = lax.broadcasted_iota(I32, (8, LANES), 0)
    meta = jnp.where(row8 == 0, tile_e, jnp.where(row8 == 1, info, jnp.where(row8 == 2, n_used, 0.0)))
    meta_ref[...] = meta.astype(I32)


def _route_plan(route, tile, n_tiles):
    m = route.shape[0]
    blk = min(PLAN_TILE, m)
    assert m % blk == 0 and n_tiles <= LANES
    return pl.pallas_call(
        functools.partial(_route_plan_kernel, tile=tile, n_tiles=n_tiles, blk=blk),
        out_shape=[jax.ShapeDtypeStruct((8, m), I32), jax.ShapeDtypeStruct((8, LANES), I32)],
        scratch_shapes=[pltpu.VMEM((m, LANES), F32)],
        compiler_params=pltpu.CompilerParams(vmem_limit_bytes=VMEM_LIMIT_BYTES),
        name="route_plan",
    )(route)


def _dispatch_kernel(meta_ref, pos_ref, h_ref, xs_ref, zero_ref, sem, *, tile, sl):
    i = pl.program_id(0)
    tt = h_ref.shape[0] // sl
    n_tiles = xs_ref.shape[0] // (tile * sl)

    def zero_copy(e):
        first = pl.multiple_of((meta_ref[e] - tile) * sl, tile * sl)
        return pltpu.make_async_copy(zero_ref, xs_ref.at[pl.ds(first, tile * sl), :], sem.at[1])

    def tail_copy(j):
        return pltpu.make_async_copy(zero_ref, xs_ref.at[pl.ds(j * tile * sl, tile * sl), :], sem.at[1])

    @pl.when(i == 0)
    def _():
        zero_ref[...] = jnp.zeros_like(zero_ref)
        n_used = meta_ref[2 * N_EXPERTS]
        for e in range(N_EXPERTS):
            @pl.when(meta_ref[N_EXPERTS + e] > 0)
            def _():
                zero_copy(e).start()
        for j in range(n_tiles - N_EXPERTS, n_tiles):
            @pl.when(j >= n_used)
            def _():
                tail_copy(j).start()
        for e in range(N_EXPERTS):
            @pl.when(meta_ref[N_EXPERTS + e] > 0)
            def _():
                zero_copy(e).wait()
        for j in range(n_tiles - N_EXPERTS, n_tiles):
            @pl.when(j >= n_used)
            def _():
                tail_copy(j).wait()

    def row_copy(r, c):
        src = pl.multiple_of(r * sl, sl)
        dst = pl.multiple_of(pos_ref[0, c * tt + r] * sl, sl)
        return pltpu.make_async_copy(h_ref.at[pl.ds(src, sl), :], xs_ref.at[pl.ds(dst, sl), :], sem.at[0])

    def start_body(r, carry):
        row_copy(r, 0).start()
        row_copy(r, 1).start()
        return carry

    lax.fori_loop(0, tt, start_body, 0, unroll=8)
    for _ in range(2):
        pltpu.make_async_copy(h_ref, xs_ref.at[pl.ds(0, tt * sl), :], sem.at[0]).wait()


def _dispatch(hs, pos_tiles, ginfo, n_slots, tile, sl):
    m = hs.shape[0] // sl
    tt = pos_tiles.shape[2] // 2
    return pl.pallas_call(
        functools.partial(_dispatch_kernel, tile=tile, sl=sl),
        grid_spec=pltpu.PrefetchScalarGridSpec(
            num_scalar_prefetch=1,
            grid=(m // tt,),
            in_specs=[
                pl.BlockSpec((None, 1, 2 * tt), lambda i, meta: (i, 0, 0), memory_space=pltpu.SMEM),
                pl.BlockSpec((tt * sl, LANES), lambda i, meta: (i, 0)),
            ],
            out_specs=pl.BlockSpec(memory_space=pl.ANY),
            scratch_shapes=[pltpu.VMEM((tile * sl, LANES), U32), pltpu.SemaphoreType.DMA((2,))],
        ),
        out_shape=jax.ShapeDtypeStruct((n_slots * sl, LANES), U32),
        compiler_params=_cparams("arbitrary"),
        name="dispatch",
    )(ginfo, pos_tiles, hs)


def _expert_ffn_kernel(te_ref, nu_ref, xs_ref, wg_ref, wu_ref, wd_ref, y_ref, h_ref, acc_ref):
    i, f = pl.program_id(0), pl.program_id(1)
    used = i < nu_ref[0]
    last = pl.num_programs(1) - 1
    tile, d = h_ref.shape

    @pl.when(used & (f == 0))
    def _():
        h_ref[...] = _from_slabs(xs_ref, tile, d).astype(BF16)

    @pl.when(used)
    def _():
        y = _swiglu_block(h_ref[...], wg_ref[...], wu_ref[...], wd_ref[...])

        @pl.when(f == 0)
        def _():
            acc_ref[...] = y

        @pl.when(f > 0)
        def _():
            acc_ref[...] += y

    @pl.when(used & (f == last))
    def _():
        _to_slabs(y_ref, acc_ref[...].astype(BF16).astype(F32))

    @pl.when(jnp.logical_not(used) & (f == last))
    def _():
        y_ref[...] = jnp.zeros_like(y_ref)


def _expert_ffn(xs, tile_e, n_used, wg, wu, wd, tile, sl):
    n_slots = xs.shape[0] // sl
    d = 2 * sl * LANES
    ff = wg.shape[2]
    bf = min(FFN_COL_TILE, ff)
    assert n_slots % tile == 0 and ff % bf == 0
    nf = ff // bf
    row = lambda i, f, te, nu: (jnp.minimum(i, nu[0] - 1), 0)
    fblk = lambda i, f, nu: jnp.where(i < nu[0], f, nf - 1)
    return pl.pallas_call(
        _expert_ffn_kernel,
        grid_spec=pltpu.PrefetchScalarGridSpec(
            num_scalar_prefetch=2,
            grid=(n_slots // tile, nf),
            in_specs=[
                pl.BlockSpec((tile * sl, LANES), row),
                pl.BlockSpec((None, d, bf), lambda i, f, te, nu: (te[i], 0, fblk(i, f, nu))),
                pl.BlockSpec((None, d, bf), lambda i, f, te, nu: (te[i], 0, fblk(i, f, nu))),
                pl.BlockSpec((None, bf, d), lambda i, f, te, nu: (te[i], fblk(i, f, nu), 0)),
            ],
            out_specs=pl.BlockSpec((tile * sl, LANES), lambda i, f, te, nu: (i, 0)),
            scratch_shapes=[pltpu.VMEM((tile, d), BF16), pltpu.VMEM((tile, d), F32)],
        ),
        out_shape=jax.ShapeDtypeStruct((n_slots * sl, LANES), U32),
        compiler_params=_cparams("parallel", "arbitrary"),
        name="expert_ffn",
    )(tile_e, n_used, xs, wg, wu, wd)


def _combine_kernel(pos_ref, nxt_ref, x_ref, route_ref, g_ref, y_ref, o_ref, buf_ref, sem):
    i = pl.program_id(0)
    n = pl.num_programs(0)
    tt, d = x_ref.shape
    sl = d // (2 * LANES)
    slot = lax.rem(i, 2)

    def gather(p_ref, dst_slot):
        def row_copy(r, c):
            src = pl.multiple_of(p_ref[0, c * tt + r] * sl, sl)
            dst = pl.multiple_of(r * sl, sl)
            return pltpu.make_async_copy(y_ref.at[pl.ds(src, sl), :], buf_ref.at[dst_slot, c, pl.ds(dst, sl), :],
                                         sem.at[dst_slot])

        def start_body(r, carry):
            row_copy(r, 0).start()
            row_copy(r, 1).start()
            return carry

        lax.fori_loop(0, tt, start_body, 0, unroll=8)

    @pl.when(i == 0)
    def _():
        gather(pos_ref, 0)

    @pl.when(i + 1 < n)
    def _():
        gather(nxt_ref, 1 - slot)

    for c in range(2):
        pltpu.make_async_copy(y_ref.at[pl.ds(0, tt * sl), :], buf_ref.at[slot, c], sem.at[slot]).wait()

    rt = route_ref[...]
    moe = (rt[:, 2:3] * _from_slabs(buf_ref.at[slot, 0], tt, d)
           + rt[:, 3:4] * _from_slabs(buf_ref.at[slot, 1], tt, d))
    xo = x_ref[...] + moe
    ms = jnp.mean(xo * xo, axis=-1, keepdims=True)
    o_ref[...] = xo * lax.rsqrt(ms + EPS) * g_ref[...]


def _combine(x, route, g, y, pos_tiles):
    m, d = x.shape
    tt = pos_tiles.shape[2] // 2
    nt = m // tt
    return pl.pallas_call(
        _combine_kernel,
        grid=(nt,),
        in_specs=[
            pl.BlockSpec((None, 1, 2 * tt), lambda i: (i, 0, 0), memory_space=pltpu.SMEM),
            pl.BlockSpec((None, 1, 2 * tt), lambda i: (jnp.minimum(i + 1, nt - 1), 0, 0), memory_space=pltpu.SMEM),
            pl.BlockSpec((tt, d), lambda i: (i, 0)),
            pl.BlockSpec((tt, LANES), lambda i: (i, 0)),
            pl.BlockSpec((1, d), lambda i: (0, 0)),
            pl.BlockSpec(memory_space=pl.ANY),
        ],
        out_specs=pl.BlockSpec((tt, d), lambda i: (i, 0)),
        out_shape=jax.ShapeDtypeStruct((m, d), F32),
        scratch_shapes=[pltpu.VMEM((2, 2, tt * (d // (2 * LANES)), LANES), U32), pltpu.SemaphoreType.DMA((2,))],
        compiler_params=_cparams("arbitrary"),
        name="combine",
    )(pos_tiles, pos_tiles, x, route, g, y)


def _moe_ffn_final(x, g_ffn, wr, wg, wu, wd, g_final):
    m, d = x.shape
    tile = min(FFN_ROW_TILE, m)
    n_tiles = (2 * m) // tile + N_EXPERTS
    tt = min(TOK_TILE, m)
    sl = d // (2 * LANES)
    hs, route = _router(x, g_ffn, wr)
    pos, meta = _route_plan(route, tile, n_tiles)
    pos_tiles = pos[:2].reshape(2, m // tt, tt).transpose(1, 0, 2).reshape(m // tt, 1, 2 * tt)
    xs = _dispatch(hs, pos_tiles, meta[1, :2 * N_EXPERTS + 1], n_tiles * tile, tile, sl)
    y = _expert_ffn(xs, meta[0, :n_tiles], meta[2, :1], wg, wu, wd, tile, sl)
    return _combine(x, route, g_final, y, pos_tiles)


def kernel(x, ln_mix, w_in, w_alpha, b_alpha, gla_norm, w_pool, pool_scale, w_branch_gla, w_branch_pool, w_out,
           ln_ffn, ffn_w_gate, ffn_w_up, ffn_w_down, router_w, exp_w_gate, exp_w_up, exp_w_down, ln_final):
    batch, seq, d = x.shape
    depth = w_in.shape[0]
    dk = w_alpha.shape[2]
    dv = w_branch_gla.shape[1]
    pw = w_branch_pool.shape[1]
    rank = w_alpha.shape[1]
    assert depth == 2 and rank <= LANES
    a_col = 2 * dk + 2 * dv
    m = batch * seq
    xf = x.reshape(m, d)
    row = lambda v: v.reshape(1, -1)
    w_in_t = jnp.swapaxes(w_in, 1, 2)

    for l in range(depth):
        w_main, w_a1 = _inproj_weight(w_in_t, l, a_col, rank)
        w_al = jnp.pad(w_alpha[l], ((0, LANES - rank), (0, 0)))
        z, a1 = _norm_inproj(xf, row(ln_mix[l]), w_main, w_a1)
        og = _gla(z, a1, w_al, row(b_alpha[l]), row(gla_norm[l]), batch, seq, dk, dv)
        yp = _pool(z, _to_bf16(w_pool, l), row(pool_scale[l]), batch, seq, a_col)
        u = _merge(og, yp, z, _to_bf16(w_branch_gla, l), _to_bf16(w_branch_pool, l), a_col + pw, a_col + pw + d)
        xf = _out_proj(u, _to_bf16(w_out, l), xf)
        if l % 2 == 0:
            i = l // 2
            xf = _ffn(xf, row(ln_ffn[l]), _to_bf16(ffn_w_gate, i), _to_bf16(ffn_w_up, i), _to_bf16(ffn_w_down, i))
        else:
            i = l // 2
            wr = jnp.pad(router_w[i], ((0, 0), (0, LANES - N_EXPERTS))).astype(BF16)
            xf = _moe_ffn_final(xf, row(ln_ffn[l]), wr, _to_bf16(exp_w_gate, i), _to_bf16(exp_w_up, i),
                                _to_bf16(exp_w_down, i), ln_final.reshape(1, -1))
    return xf.reshape(batch, seq, d)
```

```python
import functools
import math

import jax
import jax.numpy as jnp
from jax import lax
from jax.experimental import pallas as pl
from jax.experimental.pallas import tpu as pltpu

F32 = jnp.float32
BF16 = jnp.bfloat16
I32 = jnp.int32

EPS = 1e-6
GLA_HEADS = 4
GLA_LOWRANK = 16
GLA_GATE_TEMP = 16.0
GLA_CHUNK = 64
GLA_HEADS_PER_STEP = 4
POOL_WINDOWS = (2, 4, 8, 16)
POOL_GROUPS = 4
N_EXPERTS = 8

LANES = 128
VMEM_LIMIT_BYTES = 56 * 1024 * 1024

ROW_TILE = 1024
COL_TILE = 1024
FFN_ROW_TILE = 512
DENSE_FFN_ROW_TILE = 512
FFN_COL_TILE = 512
SEQ_TILE = 512
TOK_TILE = 256
PLAN_TILE = 512
HALO = 16


def _cparams(*sem):
    return pltpu.CompilerParams(dimension_semantics=sem, vmem_limit_bytes=VMEM_LIMIT_BYTES)


def _dot(a, b):
    return jnp.dot(a, b, preferred_element_type=F32)


def _dot_nt(a, b):
    return lax.dot_general(a, b, (((1,), (1,)), ((), ())), preferred_element_type=F32)


def _split_bf16(x):
    hi = x.astype(BF16)
    lo = (x - hi.astype(F32)).astype(BF16)
    return hi, lo


def _pick(n, target, align=LANES):
    if n <= target:
        return n
    best = None
    for cand in range(align, target + 1, align):
        if n % cand == 0:
            best = cand
    assert best is not None, (n, target, align)
    return best


def _cast_kernel(x_ref, o_ref):
    o_ref[...] = x_ref[...].astype(BF16)


def _to_bf16(w, layer):
    shape = w.shape[1:]
    w3 = w.reshape(w.shape[0], -1, shape[-1])
    _, r, c = w3.shape
    br, bc = _pick(r, 1024, 16), _pick(c, 2048)
    out = pl.pallas_call(
        _cast_kernel,
        grid=(r // br, c // bc),
        in_specs=[pl.BlockSpec((None, br, bc), lambda i, j: (layer, i, j))],
        out_specs=pl.BlockSpec((br, bc), lambda i, j: (i, j)),
        out_shape=jax.ShapeDtypeStruct((r, c), BF16),
        compiler_params=_cparams("parallel", "parallel"),
        name="to_bf16",
    )(w3)
    return out.reshape(shape)


def _inproj_weight_kernel(w_ref, nxt_ref, lr_ref, wm_ref, wa_ref, *, first_shifted, rank):
    j = pl.program_id(0)

    @pl.when(j < first_shifted)
    def _():
        wm_ref[...] = w_ref[...].astype(BF16)

    @pl.when(j >= first_shifted)
    def _():
        wm_ref[...] = jnp.concatenate([w_ref[rank:, :], nxt_ref[...]], axis=0).astype(BF16)

    @pl.when(j == 0)
    def _():
        pad = jnp.zeros((wa_ref.shape[0] - rank, wa_ref.shape[1]), F32)
        wa_ref[...] = jnp.concatenate([lr_ref[...], pad], axis=0).astype(BF16)


def _inproj_weight(w_t, layer, a_col, rank):
    _, n_in, d = w_t.shape
    n = n_in - rank
    br = _pick(math.gcd(n, a_col), 512)
    assert a_col % br == 0 and rank % 8 == 0 and rank < LANES and br % rank == 0
    per = br // rank
    return pl.pallas_call(
        functools.partial(_inproj_weight_kernel, first_shifted=a_col // br, rank=rank),
        grid=(n // br,),
        in_specs=[
            pl.BlockSpec((None, br, d), lambda j: (layer, j, 0)),
            pl.BlockSpec((None, rank, d), lambda j: (layer, (j + 1) * per, 0)),
            pl.BlockSpec((None, rank, d), lambda j: (layer, a_col // rank, 0)),
        ],
        out_specs=[
            pl.BlockSpec((br, d), lambda j: (j, 0)),
            pl.BlockSpec((LANES, d), lambda j: (0, 0)),
        ],
        out_shape=[jax.ShapeDtypeStruct((n, d), BF16), jax.ShapeDtypeStruct((LANES, d), BF16)],
        compiler_params=_cparams("arbitrary"),
        name="inproj_weight",
    )(w_t, w_t, w_t)


def _side_cast_plan(side, steps):
    side3 = side.reshape(side.shape[0], -1, side.shape[-1])
    r = side3.shape[1]
    rb = next(c for c in range(16, r + 1, 16) if r % c == 0 and r // c <= steps)
    return side3, rb, r // rb


def _side_cast(step, n_blocks, side_ref, side_out_ref):
    @pl.when(step < n_blocks)
    def _():
        side_out_ref[...] = side_ref[...].astype(BF16)


def _norm_inproj_kernel(x_ref, g_ref, w_ref, wa_ref, side_ref, z_ref, a1_ref, side_out_ref, h_ref, *,
                        rows, side_blocks):
    _side_cast(pl.program_id(0) * pl.num_programs(1) + pl.program_id(1), side_blocks, side_ref, side_out_ref)

    @pl.when(pl.program_id(1) == 0)
    def _():
        def body(c, carry):
            r = pl.multiple_of(c * rows, rows)
            xf = x_ref[pl.ds(r, rows), :]
            ms = jnp.mean(xf * xf, axis=-1, keepdims=True)
            h_ref[pl.ds(r, rows), :] = (xf * lax.rsqrt(ms + EPS) * g_ref[...]).astype(BF16)
            return carry
        lax.fori_loop(0, x_ref.shape[0] // rows, body, 0)
        a1_ref[...] = _dot_nt(h_ref[...], wa_ref[...])

    z_ref[...] = _dot_nt(h_ref[...], w_ref[...]).astype(BF16)


def _norm_inproj(x, g, w_t, wa_t, side, side_layer):
    m, d = x.shape
    n = w_t.shape[0]
    bm, bn = min(ROW_TILE, m), _pick(n, COL_TILE)
    assert m % bm == 0 and n % bn == 0
    ni, nj = m // bm, n // bn
    side3, rb, nblk = _side_cast_plan(side, ni * nj)
    sc = side3.shape[2]
    side_idx = lambda i, j: jnp.minimum(i * nj + j, nblk - 1)
    z, a1, side_bf = pl.pallas_call(
        functools.partial(_norm_inproj_kernel, rows=128, side_blocks=nblk),
        grid=(ni, nj),
        in_specs=[
            pl.BlockSpec((bm, d), lambda i, j: (i, 0)),
            pl.BlockSpec((1, d), lambda i, j: (0, 0)),
            pl.BlockSpec((bn, d), lambda i, j: (j, 0)),
            pl.BlockSpec((LANES, d), lambda i, j: (0, 0)),
            pl.BlockSpec((None, rb, sc), lambda i, j: (side_layer, side_idx(i, j), 0)),
        ],
        out_specs=[
            pl.BlockSpec((bm, bn), lambda i, j: (i, j)),
            pl.BlockSpec((bm, LANES), lambda i, j: (i, 0)),
            pl.BlockSpec((rb, sc), lambda i, j: (side_idx(i, j), 0)),
        ],
        out_shape=[jax.ShapeDtypeStruct((m, n), BF16), jax.ShapeDtypeStruct((m, LANES), F32),
                   jax.ShapeDtypeStruct(side3.shape[1:], BF16)],
        scratch_shapes=[pltpu.VMEM((bm, d), BF16)],
        compiler_params=_cparams("arbitrary", "arbitrary"),
        name="norm_inproj",
    )(x, g, w_t, wa_t, side3)
    return z, a1, side_bf.reshape(side.shape[1:])


def _log_sigmoid(x):
    return jnp.minimum(x, 0.0) - jnp.log(1.0 + jnp.exp(-jnp.abs(x)))


def _gla_kernel(q_ref, k_ref, v_ref, r_ref, a1_ref, wal_ref, bal_ref, gn_ref, tri_ref, o_ref, st_ref, *,
                chunk, heads):
    @pl.when(pl.program_id(2) == 0)
    def _():
        st_ref[...] = jnp.zeros_like(st_ref)

    hk = q_ref.shape[1] // heads
    hv = v_ref.shape[1] // heads
    a1 = _split_bf16(a1_ref[...])
    for h in range(heads):
        ks, vs = slice(h * hk, (h + 1) * hk), slice(h * hv, (h + 1) * hv)
        o, st_new = _gla_head(q_ref[:, ks], k_ref[:, ks], v_ref[:, vs], r_ref[:, vs], a1, wal_ref[:, ks],
                              bal_ref[:, ks], gn_ref[...], tri_ref[...], st_ref[h], chunk)
        o_ref[:, vs] = o
        st_ref[h] = st_new


def _gla_head(q, k, v, r, a1, wal, bal, gn, tri, st, chunk):
    t, dk = q.shape
    nc = t // chunk
    pair = 2 * chunk
    shift = chunk.bit_length() - 1
    nt_dot = lambda a, bb: lax.dot_general(a, bb, (((1,), (1,)), ((), ())), preferred_element_type=F32)
    rows = lambda x, c0, c1: x[c0 * chunk:c1 * chunk, :]

    a_hi, a_lo = a1
    w_hi, w_lo = _split_bf16(wal)
    xg = _dot(a_hi, w_hi) + _dot(a_hi, w_lo) + _dot(a_lo, w_hi) + bal
    la = _log_sigmoid(xg) * (1.0 / GLA_GATE_TEMP)

    la_hi, la_lo = _split_bf16(la)
    b = _dot(tri, la_hi) + _dot(tri, la_lo)

    bl = [b[(c + 1) * chunk - 1:(c + 1) * chunk, :] for c in range(nc)]
    pre = [jnp.zeros_like(bl[0])]
    for c in range(nc):
        pre.append(pre[c] + bl[c])

    q = q.astype(F32)
    k = k.astype(F32)
    q_t = q * jnp.exp(b) * (dk ** -0.5)
    k_t = (k * jnp.exp(-b)).astype(BF16)
    qc = [rows(q_t, c, c + 1) for c in range(nc)]
    kec = [rows(k, c, c + 1) * jnp.exp(bl[c] - rows(b, c, c + 1)) for c in range(nc)]
    q_tb = q_t.astype(BF16)
    k_eb = jnp.concatenate(kec, axis=0).astype(BF16)

    def q_from(mid, hi):
        return jnp.concatenate([qc[c] if c == mid else qc[c] * jnp.exp(pre[c] - pre[mid])
                                for c in range(mid, hi)], axis=0).astype(BF16)

    def k_upto(lo, mid):
        return jnp.concatenate([kec[m] if m == mid - 1 else kec[m] * jnp.exp(pre[mid] - pre[m + 1])
                                for m in range(lo, mid)], axis=0).astype(BF16)

    nb = t // pair
    blocks = [[None] * nb for _ in range(nb)]
    r2 = lax.broadcasted_iota(I32, (pair, pair), 0)
    c2 = lax.broadcasted_iota(I32, (pair, pair), 1)
    same = ((r2 >> shift) == (c2 >> shift)) & (r2 >= c2)
    cross = (r2 >= chunk) & (c2 < chunk)
    for p in range(nb):
        qp = rows(q_tb, 2 * p, 2 * p + 2)
        in_chunk = nt_dot(qp, rows(k_t, 2 * p, 2 * p + 2))
        next_chunk = nt_dot(qp, rows(k_eb, 2 * p, 2 * p + 2))
        blocks[p][p] = jnp.where(same, in_chunk, jnp.where(cross, next_chunk, 0.0))
    g = 4
    while g <= nc:
        half = g // 2
        for grp in range(nc // g):
            lo, mid, hi = grp * g, grp * g + half, (grp + 1) * g
            x = nt_dot(q_from(mid, hi), k_upto(lo, mid))
            hb = half // 2
            for i in range(hb):
                for j in range(hb):
                    blocks[mid // 2 + i][lo // 2 + j] = x[i * pair:(i + 1) * pair, j * pair:(j + 1) * pair]
        g *= 2
    zero_blk = jnp.zeros((pair, pair), F32)
    att = jnp.concatenate(
        [jnp.concatenate([zero_blk if blk is None else blk for blk in brow], axis=1) for brow in blocks], axis=0)

    o = _dot(att.astype(BF16), v) + nt_dot(q_from(0, nc), st.astype(BF16))
    upd = lax.dot_general(v, k_upto(0, nc), (((0,), (0,)), ((), ())), preferred_element_type=F32)
    st_new = st * jnp.exp(pre[nc]) + upd

    o = o * lax.rsqrt(jnp.mean(o * o, axis=-1, keepdims=True) + EPS) * gn
    r = r.astype(F32)
    return (o * (r * jax.nn.sigmoid(r))).astype(BF16), st_new


def _gla(z, a1, w_alpha, b_alpha, gla_norm, batch, seq, dk_total, dv_total):
    m = z.shape[0]
    hk, hv = dk_total // GLA_HEADS, dv_total // GLA_HEADS
    t = min(SEQ_TILE, seq)
    assert seq % t == 0 and t % GLA_CHUNK == 0 and hk % LANES == 0 and hv % LANES == 0
    ns = seq // t
    nc = t // GLA_CHUNK
    assert nc >= 2 and nc & (nc - 1) == 0 and GLA_HEADS % GLA_HEADS_PER_STEP == 0
    hps = GLA_HEADS_PER_STEP
    gk, gv = hps * hk, hps * hv
    k_off = dk_total // gk
    v_off = (2 * dk_total) // gv
    r_off = (2 * dk_total + dv_total) // gv
    idx = jnp.arange(t, dtype=I32)
    tri = ((idx[:, None] // GLA_CHUNK == idx[None, :] // GLA_CHUNK) & (idx[:, None] >= idx[None, :])).astype(BF16)
    rowi = lambda b, h, s: b * ns + s
    return pl.pallas_call(
        functools.partial(_gla_kernel, chunk=GLA_CHUNK, heads=hps),
        grid=(batch, GLA_HEADS // hps, ns),
        in_specs=[
            pl.BlockSpec((t, gk), lambda b, h, s: (rowi(b, h, s), h)),
            pl.BlockSpec((t, gk), lambda b, h, s: (rowi(b, h, s), k_off + h)),
            pl.BlockSpec((t, gv), lambda b, h, s: (rowi(b, h, s), v_off + h)),
            pl.BlockSpec((t, gv), lambda b, h, s: (rowi(b, h, s), r_off + h)),
            pl.BlockSpec((t, LANES), lambda b, h, s: (rowi(b, h, s), 0)),
            pl.BlockSpec((LANES, gk), lambda b, h, s: (0, h)),
            pl.BlockSpec((1, gk), lambda b, h, s: (0, h)),
            pl.BlockSpec((1, hv), lambda b, h, s: (0, 0)),
            pl.BlockSpec((t, t), lambda b, h, s: (0, 0)),
        ],
        out_specs=pl.BlockSpec((t, gv), lambda b, h, s: (rowi(b, h, s), h)),
        out_shape=jax.ShapeDtypeStruct((m, dv_total), BF16),
        scratch_shapes=[pltpu.VMEM((hps, hv, hk), F32)],
        compiler_params=_cparams("parallel", "parallel", "arbitrary"),
        name="gla",
    )(z, z, z, z, a1, w_alpha, b_alpha, gla_norm, tri)


def _pool_kernel(p_ref, ph_ref, wp_ref, sc_ref, y_ref, *, windows):
    s = pl.program_id(1)
    t = p_ref.shape[0]
    halo = ph_ref.shape[0]
    gw = wp_ref.shape[1]
    dist = lax.broadcasted_iota(I32, (t, t), 0) - lax.broadcasted_iota(I32, (t, t), 1)
    dist_h = lax.broadcasted_iota(I32, (t, halo), 0) + halo - lax.broadcasted_iota(I32, (t, halo), 1)
    pos = s * t + lax.broadcasted_iota(I32, (t, 1), 0)
    for g, w in enumerate(windows):
        cs = slice(g * gw, (g + 1) * gw)
        pc = p_ref[:, cs]
        band = jnp.where((dist >= 0) & (dist < w), 1.0, 0.0).astype(BF16)
        band_h = jnp.where(dist_h < w, 1.0, 0.0).astype(BF16)
        tot = _dot(band, pc) + jnp.where(s > 0, _dot(band_h, ph_ref[:, cs]), 0.0)
        cnt = jnp.minimum(pos + 1, w).astype(F32)
        mixed = tot / cnt - pc.astype(F32)
        y = _dot(mixed.astype(BF16), wp_ref[g]) * sc_ref[:, cs]
        y_ref[:, cs] = y.astype(BF16)


def _pool(z, w_pool, pool_scale, batch, seq, p_col0):
    m = z.shape[0]
    groups, gw, _ = w_pool.shape
    pw = groups * gw
    t = min(SEQ_TILE, seq)
    assert seq % t == 0 and t % HALO == 0 and p_col0 % pw == 0 and gw % LANES == 0
    ns = seq // t
    pc = p_col0 // pw
    hb = t // HALO
    return pl.pallas_call(
        functools.partial(_pool_kernel, windows=POOL_WINDOWS),
        grid=(batch, ns),
        in_specs=[
            pl.BlockSpec((t, pw), lambda b, s: (b * ns + s, pc)),
            pl.BlockSpec((HALO, pw), lambda b, s: (jnp.maximum((b * ns + s) * hb - 1, 0), pc)),
            pl.BlockSpec((groups, gw, gw), lambda b, s: (0, 0, 0)),
            pl.BlockSpec((1, pw), lambda b, s: (0, 0)),
        ],
        out_specs=pl.BlockSpec((t, pw), lambda b, s: (b * ns + s, 0)),
        out_shape=jax.ShapeDtypeStruct((m, pw), BF16),
        compiler_params=_cparams("parallel", "arbitrary"),
        name="pool",
    )(z, z, w_pool, pool_scale)


def _merge_kernel(og_ref, yp_ref, ga_ref, gb_ref, wa_ref, wb_ref, u_ref):
    ya = _dot(og_ref[...], wa_ref[...])
    yb = _dot(yp_ref[...], wb_ref[...])
    ga = jax.nn.sigmoid(ga_ref[...].astype(F32))
    gb = jax.nn.sigmoid(gb_ref[...].astype(F32))
    u_ref[...] = (ga * ya + gb * yb).astype(BF16)


def _merge(og, yp, z, w_a, w_b, ga_col0, gb_col0):
    m, dv = og.shape
    pw = yp.shape[1]
    d = w_a.shape[1]
    bm, bn = min(ROW_TILE, m), _pick(math.gcd(d, ga_col0, gb_col0), COL_TILE)
    assert m % bm == 0 and d % bn == 0 and ga_col0 % bn == 0 and gb_col0 % bn == 0
    ga0, gb0 = ga_col0 // bn, gb_col0 // bn
    return pl.pallas_call(
        _merge_kernel,
        grid=(m // bm, d // bn),
        in_specs=[
            pl.BlockSpec((bm, dv), lambda i, j: (i, 0)),
            pl.BlockSpec((bm, pw), lambda i, j: (i, 0)),
            pl.BlockSpec((bm, bn), lambda i, j: (i, ga0 + j)),
            pl.BlockSpec((bm, bn), lambda i, j: (i, gb0 + j)),
            pl.BlockSpec((dv, bn), lambda i, j: (0, j)),
            pl.BlockSpec((pw, bn), lambda i, j: (0, j)),
        ],
        out_specs=pl.BlockSpec((bm, bn), lambda i, j: (i, j)),
        out_shape=jax.ShapeDtypeStruct((m, d), BF16),
        compiler_params=_cparams("parallel", "arbitrary"),
        name="merge",
    )(og, yp, z, z, w_a, w_b)


def _out_proj_kernel(u_ref, w_ref, x_ref, o_ref):
    o_ref[...] = x_ref[...] + _dot(u_ref[...], w_ref[...])


def _out_proj(u, w, x):
    m, d = x.shape
    bm, bn = min(ROW_TILE, m), _pick(d, COL_TILE)
    assert m % bm == 0 and d % bn == 0
    return pl.pallas_call(
        _out_proj_kernel,
        grid=(m // bm, d // bn),
        in_specs=[
            pl.BlockSpec((bm, u.shape[1]), lambda i, j: (i, 0)),
            pl.BlockSpec((u.shape[1], bn), lambda i, j: (0, j)),
            pl.BlockSpec((bm, bn), lambda i, j: (i, j)),
        ],
        out_specs=pl.BlockSpec((bm, bn), lambda i, j: (i, j)),
        out_shape=jax.ShapeDtypeStruct((m, d), F32),
        compiler_params=_cparams("parallel", "arbitrary"),
        name="out_proj",
    )(u, w, x)


def _rmsnorm_rows(x_ref, g_ref, h_ref, rows):
    def body(c, carry):
        r = pl.multiple_of(c * rows, rows)
        xf = x_ref[pl.ds(r, rows), :]
        ms = jnp.mean(xf * xf, axis=-1, keepdims=True)
        h_ref[pl.ds(r, rows), :] = (xf * lax.rsqrt(ms + EPS) * g_ref[...]).astype(BF16)
        return carry
    lax.fori_loop(0, x_ref.shape[0] // rows, body, 0)


def _swiglu_block(h, wg, wu, wd):
    g = _dot(h, wg)
    u = _dot(h, wu)
    a = (g * jax.nn.sigmoid(g) * u).astype(BF16)
    return _dot(a, wd)


def _ffn_kernel(x_ref, g_ref, wg_ref, wu_ref, wd_ref, side_ref, o_ref, side_out_ref, h_ref, *, side_blocks):
    f = pl.program_id(1)
    _side_cast(pl.program_id(0) * pl.num_programs(1) + f, side_blocks, side_ref, side_out_ref)

    @pl.when(f == 0)
    def _():
        _rmsnorm_rows(x_ref, g_ref, h_ref, 128)
        o_ref[...] = x_ref[...]

    o_ref[...] += _swiglu_block(h_ref[...], wg_ref[...], wu_ref[...], wd_ref[...])


def _ffn(x, g, wg, wu, wd, side, side_layer):
    m, d = x.shape
    ff = wg.shape[1]
    bm, bf = min(DENSE_FFN_ROW_TILE, m), min(FFN_COL_TILE, ff)
    assert m % bm == 0 and ff % bf == 0
    ni, nf = m // bm, ff // bf
    side3, rb, nblk = _side_cast_plan(side, ni * nf)
    sc = side3.shape[2]
    side_idx = lambda i, f: jnp.minimum(i * nf + f, nblk - 1)
    out, side_bf = pl.pallas_call(
        functools.partial(_ffn_kernel, side_blocks=nblk),
        grid=(ni, nf),
        in_specs=[
            pl.BlockSpec((bm, d), lambda i, f: (i, 0)),
            pl.BlockSpec((1, d), lambda i, f: (0, 0)),
            pl.BlockSpec((d, bf), lambda i, f: (0, f)),
            pl.BlockSpec((d, bf), lambda i, f: (0, f)),
            pl.BlockSpec((bf, d), lambda i, f: (f, 0)),
            pl.BlockSpec((None, rb, sc), lambda i, f: (side_layer, side_idx(i, f), 0)),
        ],
        out_specs=[
            pl.BlockSpec((bm, d), lambda i, f: (i, 0)),
            pl.BlockSpec((rb, sc), lambda i, f: (side_idx(i, f), 0)),
        ],
        out_shape=[jax.ShapeDtypeStruct((m, d), F32), jax.ShapeDtypeStruct(side3.shape[1:], BF16)],
        scratch_shapes=[pltpu.VMEM((bm, d), BF16)],
        compiler_params=_cparams("arbitrary", "arbitrary"),
        name="ffn",
    )(x, g, wg, wu, wd, side3)
    return out, side_bf.reshape(side.shape[1:])


U32 = jnp.uint32


def _to_slabs(slab_ref, x):
    rows, d = x.shape
    half = d // 2
    sl = half // LANES
    lo = lax.bitcast_convert_type(x[:, :half], U32) >> 16
    hi = (lax.bitcast_convert_type(x[:, half:], U32) >> 16) << 16
    words = hi | lo
    for s in range(sl):
        slab_ref[pl.ds(s, rows, stride=sl), :] = words[:, s * LANES:(s + 1) * LANES]


def _from_slabs(slab_ref, rows, d):
    sl = d // (2 * LANES)
    words = jnp.concatenate([slab_ref[pl.ds(s, rows, stride=sl), :] for s in range(sl)], axis=1)
    lo = lax.bitcast_convert_type(words << 16, F32)
    hi = lax.bitcast_convert_type((words >> 16) << 16, F32)
    return jnp.concatenate([lo, hi], axis=1)


def _router_kernel(x_ref, g_ref, wr_ref, hs_ref, route_ref, hb_ref):
    _rmsnorm_rows(x_ref, g_ref, hb_ref, 128)
    hb = hb_ref[...]
    _to_slabs(hs_ref, hb.astype(F32))
    logits = _dot(hb, wr_ref[...])
    lane = lax.broadcasted_iota(I32, logits.shape, 1)
    lane_f = lane.astype(F32)
    neg = jnp.float32(-jnp.inf)
    l1 = jnp.where(lane < N_EXPERTS, logits, neg)
    m1 = jnp.max(l1, axis=-1, keepdims=True)
    i1 = jnp.min(jnp.where(l1 == m1, lane_f, float(LANES)), axis=-1, keepdims=True)
    l2 = jnp.where(lane_f == i1, neg, l1)
    m2 = jnp.max(l2, axis=-1, keepdims=True)
    i2 = jnp.min(jnp.where(l2 == m2, lane_f, float(LANES)), axis=-1, keepdims=True)
    e = jnp.exp(m2 - m1)
    den = 1.0 + e
    route = jnp.where(lane == 0, i1, 0.0)
    route = jnp.where(lane == 1, i2, route)
    route = jnp.where(lane == 2, 1.0 / den, route)
    route = jnp.where(lane == 3, e / den, route)
    route_ref[...] = route


def _router(x, g, wr):
    m, d = x.shape
    bm = min(FFN_ROW_TILE, m)
    sl = d // (2 * LANES)
    assert m % bm == 0 and sl % 8 == 0
    return pl.pallas_call(
        _router_kernel,
        grid=(m // bm,),
        in_specs=[
            pl.BlockSpec((bm, d), lambda i: (i, 0)),
            pl.BlockSpec((1, d), lambda i: (0, 0)),
            pl.BlockSpec((d, LANES), lambda i: (0, 0)),
        ],
        out_specs=[
            pl.BlockSpec((bm * sl, LANES), lambda i: (i, 0)),
            pl.BlockSpec((bm, LANES), lambda i: (i, 0)),
        ],
        out_shape=[jax.ShapeDtypeStruct((m * sl, LANES), U32), jax.ShapeDtypeStruct((m, LANES), F32)],
        scratch_shapes=[pltpu.VMEM((bm, d), BF16)],
        compiler_params=_cparams("parallel"),
        name="router",
    )(x, g, wr)


def _route_plan_kernel(route_ref, pos_ref, meta_ref, rank_ref, *, tile, n_tiles, blk):
    m = route_ref.shape[0]
    nb = m // blk
    lane_b = lax.broadcasted_iota(I32, (blk, LANES), 1)
    lane_r = lax.broadcasted_iota(I32, (1, LANES), 1)
    ne = N_EXPERTS

    def onehot(r):
        rt = route_ref[pl.ds(r, blk), :]
        i1 = rt[:, 0:1].astype(I32)
        i2 = rt[:, 1:2].astype(I32)
        return jnp.where((lane_b == i1) | (lane_b == i2 + ne), 1.0, 0.0)

    strict = jnp.where(lax.broadcasted_iota(I32, (blk, blk), 0) > lax.broadcasted_iota(I32, (blk, blk), 1),
                       1.0, 0.0).astype(BF16)

    def rank_body(c, carry):
        r = pl.multiple_of(c * blk, blk)
        oh = onehot(r)
        rank_ref[pl.ds(r, blk), :] = _dot(strict, oh.astype(BF16)) + carry
        return carry + jnp.sum(oh, axis=0, keepdims=True)

    cnt12 = lax.fori_loop(0, nb, rank_body, jnp.zeros((1, LANES), F32))

    def lane_val(row, l):
        return jnp.sum(jnp.where(lane_r == l, row, 0.0), axis=-1, keepdims=True)

    cnt = jnp.zeros((1, LANES), F32)
    for e in range(ne):
        cnt = jnp.where(lane_r == e, lane_val(cnt12, e) + lane_val(cnt12, e + ne), cnt)
    tiles = jnp.floor((cnt + (tile - 1)) * (1.0 / tile))
    start = jnp.zeros((1, LANES), F32)
    for e in range(1, ne):
        start = start + jnp.where(lane_r >= e, lane_val(tiles, e - 1), 0.0)
    start = jnp.where(lane_r < ne, start * tile, 0.0)
    end = start + tiles * tile
    base = start
    for e in range(ne):
        base = jnp.where(lane_r == e + ne, lane_val(start, e) + lane_val(cnt12, e), base)

    def pos_body(c, carry):
        r = pl.multiple_of(c * blk, blk)
        slot = onehot(r) * (rank_ref[pl.ds(r, blk), :] + base)
        p1 = jnp.sum(jnp.where(lane_b < ne, slot, 0.0), axis=-1, keepdims=True)
        p2 = jnp.sum(jnp.where(lane_b >= ne, slot, 0.0), axis=-1, keepdims=True)
        both = jnp.where(lane_b == 0, p1, jnp.where(lane_b == 1, p2, 0.0))
        pos_ref[:, pl.ds(r, blk)] = jnp.transpose(both)[0:8, :].astype(I32)
        return carry

    lax.fori_loop(0, nb, pos_body, 0)

    n_used = lane_val(end, ne - 1) * (1.0 / tile)
    tile_start = lane_r.astype(F32) * tile
    tile_e = jnp.zeros((1, LANES), F32)
    last_e = jnp.zeros((1, 1), F32)
    for e in range(ne):
        tile_e = tile_e + jnp.where(lane_val(end, e) <= tile_start, 1.0, 0.0)
        last_e = jnp.where(lane_val(cnt, e) > 0, float(e), last_e)
    tile_e = jnp.where(lane_r.astype(F32) < n_used, tile_e, last_e)
    info = jnp.where(lane_r < ne, end, jnp.where(lane_r == 2 * ne, n_used, 0.0))
    for e in range(ne):
        info = jnp.where(lane_r == e + ne, lane_val(cnt, e), info)
    row8 = lax.broadcasted_iota(I32, (8, LANES), 0)
    meta = jnp.where(row8 == 0, tile_e, jnp.where(row8 == 1, info, jnp.where(row8 == 2, n_used, 0.0)))
    meta_ref[...] = meta.astype(I32)


def _route_plan(route, tile, n_tiles):
    m = route.shape[0]
    blk = min(PLAN_TILE, m)
    assert m % blk == 0 and n_tiles <= LANES
    return pl.pallas_call(
        functools.partial(_route_plan_kernel, tile=tile, n_tiles=n_tiles, blk=blk),
        out_shape=[jax.ShapeDtypeStruct((8, m), I32), jax.ShapeDtypeStruct((8, LANES), I32)],
        scratch_shapes=[pltpu.VMEM((m, LANES), F32)],
        compiler_params=pltpu.CompilerParams(vmem_limit_bytes=VMEM_LIMIT_BYTES),
        name="route_plan",
    )(route)


def _dispatch_kernel(meta_ref, pos_ref, h_ref, xs_ref, zero_ref, sem, *, tile, sl):
    i = pl.program_id(0)
    tt = h_ref.shape[0] // sl
    n_tiles = xs_ref.shape[0] // (tile * sl)

    def zero_copy(e):
        first = pl.multiple_of((meta_ref[e] - tile) * sl, tile * sl)
        return pltpu.make_async_copy(zero_ref, xs_ref.at[pl.ds(first, tile * sl), :], sem.at[1])

    def tail_copy(j):
        return pltpu.make_async_copy(zero_ref, xs_ref.at[pl.ds(j * tile * sl, tile * sl), :], sem.at[1])

    @pl.when(i == 0)
    def _():
        zero_ref[...] = jnp.zeros_like(zero_ref)
        n_used = meta_ref[2 * N_EXPERTS]
        for e in range(N_EXPERTS):
            @pl.when(meta_ref[N_EXPERTS + e] > 0)
            def _():
                zero_copy(e).start()
        for j in range(n_tiles - N_EXPERTS, n_tiles):
            @pl.when(j >= n_used)
            def _():
                tail_copy(j).start()
        for e in range(N_EXPERTS):
            @pl.when(meta_ref[N_EXPERTS + e] > 0)
            def _():
                zero_copy(e).wait()
        for j in range(n_tiles - N_EXPERTS, n_tiles):
            @pl.when(j >= n_used)
            def _():
                tail_copy(j).wait()

    def row_copy(r, c):
        src = pl.multiple_of(r * sl, sl)
        dst = pl.multiple_of(pos_ref[0, c * tt + r] * sl, sl)
        return pltpu.make_async_copy(h_ref.at[pl.ds(src, sl), :], xs_ref.at[pl.ds(dst, sl), :], sem.at[0])

    def start_body(r, carry):
        row_copy(r, 0).start()
        row_copy(r, 1).start()
        return carry

    lax.fori_loop(0, tt, start_body, 0, unroll=8)
    for _ in range(2):
        pltpu.make_async_copy(h_ref, xs_ref.at[pl.ds(0, tt * sl), :], sem.at[0]).wait()


def _dispatch(hs, pos_tiles, ginfo, n_slots, tile, sl):
    m = hs.shape[0] // sl
    tt = pos_tiles.shape[2] // 2
    return pl.pallas_call(
        functools.partial(_dispatch_kernel, tile=tile, sl=sl),
        grid_spec=pltpu.PrefetchScalarGridSpec(
            num_scalar_prefetch=1,
            grid=(m // tt,),
            in_specs=[
                pl.BlockSpec((None, 1, 2 * tt), lambda i, meta: (i, 0, 0), memory_space=pltpu.SMEM),
                pl.BlockSpec((tt * sl, LANES), lambda i, meta: (i, 0)),
            ],
            out_specs=pl.BlockSpec(memory_space=pl.ANY),
            scratch_shapes=[pltpu.VMEM((tile * sl, LANES), U32), pltpu.SemaphoreType.DMA((2,))],
        ),
        out_shape=jax.ShapeDtypeStruct((n_slots * sl, LANES), U32),
        compiler_params=_cparams("arbitrary"),
        name="dispatch",
    )(ginfo, pos_tiles, hs)


def _expert_ffn_kernel(te_ref, nu_ref, xs_ref, wg_ref, wu_ref, wd_ref, y_ref, h_ref, acc_ref):
    i, f = pl.program_id(0), pl.program_id(1)
    used = i < nu_ref[0]
    last = pl.num_programs(1) - 1
    tile, d = h_ref.shape

    @pl.when(used & (f == 0))
    def _():
        h_ref[...] = _from_slabs(xs_ref, tile, d).astype(BF16)
        acc_ref[...] = jnp.zeros_like(acc_ref)

    @pl.when(used)
    def _():
        acc_ref[...] += _swiglu_block(h_ref[...], wg_ref[...], wu_ref[...], wd_ref[...])

    @pl.when(used & (f == last))
    def _():
        _to_slabs(y_ref, acc_ref[...].astype(BF16).astype(F32))

    @pl.when(jnp.logical_not(used) & (f == last))
    def _():
        y_ref[...] = jnp.zeros_like(y_ref)


def _expert_ffn(xs, tile_e, n_used, wg, wu, wd, tile, sl):
    n_slots = xs.shape[0] // sl
    d = 2 * sl * LANES
    ff = wg.shape[2]
    bf = min(FFN_COL_TILE, ff)
    assert n_slots % tile == 0 and ff % bf == 0
    nf = ff // bf
    row = lambda i, f, te, nu: (jnp.minimum(i, nu[0] - 1), 0)
    fblk = lambda i, f, nu: jnp.where(i < nu[0], f, nf - 1)
    return pl.pallas_call(
        _expert_ffn_kernel,
        grid_spec=pltpu.PrefetchScalarGridSpec(
            num_scalar_prefetch=2,
            grid=(n_slots // tile, nf),
            in_specs=[
                pl.BlockSpec((tile * sl, LANES), row),
                pl.BlockSpec((None, d, bf), lambda i, f, te, nu: (te[i], 0, fblk(i, f, nu))),
                pl.BlockSpec((None, d, bf), lambda i, f, te, nu: (te[i], 0, fblk(i, f, nu))),
                pl.BlockSpec((None, bf, d), lambda i, f, te, nu: (te[i], fblk(i, f, nu), 0)),
            ],
            out_specs=pl.BlockSpec((tile * sl, LANES), lambda i, f, te, nu: (i, 0)),
            scratch_shapes=[pltpu.VMEM((tile, d), BF16), pltpu.VMEM((tile, d), F32)],
        ),
        out_shape=jax.ShapeDtypeStruct((n_slots * sl, LANES), U32),
        compiler_params=_cparams("parallel", "arbitrary"),
        name="expert_ffn",
    )(tile_e, n_used, xs, wg, wu, wd)


def _combine_kernel(pos_ref, nxt_ref, x_ref, route_ref, g_ref, y_ref, o_ref, buf_ref, sem):
    i = pl.program_id(0)
    n = pl.num_programs(0)
    tt, d = x_ref.shape
    sl = d // (2 * LANES)
    slot = lax.rem(i, 2)

    def gather(p_ref, dst_slot):
        def row_copy(r, c):
            src = pl.multiple_of(p_ref[0, c * tt + r] * sl, sl)
            dst = pl.multiple_of(r * sl, sl)
            return pltpu.make_async_copy(y_ref.at[pl.ds(src, sl), :], buf_ref.at[dst_slot, c, pl.ds(dst, sl), :],
                                         sem.at[dst_slot])

        def start_body(r, carry):
            row_copy(r, 0).start()
            row_copy(r, 1).start()
            return carry

        lax.fori_loop(0, tt, start_body, 0, unroll=8)

    @pl.when(i == 0)
    def _():
        gather(pos_ref, 0)

    @pl.when(i + 1 < n)
    def _():
        gather(nxt_ref, 1 - slot)

    for c in range(2):
        pltpu.make_async_copy(y_ref.at[pl.ds(0, tt * sl), :], buf_ref.at[slot, c], sem.at[slot]).wait()

    rt = route_ref[...]
    moe = (rt[:, 2:3] * _from_slabs(buf_ref.at[slot, 0], tt, d)
           + rt[:, 3:4] * _from_slabs(buf_ref.at[slot, 1], tt, d))
    xo = x_ref[...] + moe
    ms = jnp.mean(xo * xo, axis=-1, keepdims=True)
    o_ref[...] = xo * lax.rsqrt(ms + EPS) * g_ref[...]


def _combine(x, route, g, y, pos_tiles):
    m, d = x.shape
    tt = pos_tiles.shape[2] // 2
    nt = m // tt
    return pl.pallas_call(
        _combine_kernel,
        grid=(nt,),
        in_specs=[
            pl.BlockSpec((None, 1, 2 * tt), lambda i: (i, 0, 0), memory_space=pltpu.SMEM),
            pl.BlockSpec((None, 1, 2 * tt), lambda i: (jnp.minimum(i + 1, nt - 1), 0, 0), memory_space=pltpu.SMEM),
            pl.BlockSpec((tt, d), lambda i: (i, 0)),
            pl.BlockSpec((tt, LANES), lambda i: (i, 0)),
            pl.BlockSpec((1, d), lambda i: (0, 0)),
            pl.BlockSpec(memory_space=pl.ANY),
        ],
        out_specs=pl.BlockSpec((tt, d), lambda i: (i, 0)),
        out_shape=jax.ShapeDtypeStruct((m, d), F32),
        scratch_shapes=[pltpu.VMEM((2, 2, tt * (d // (2 * LANES)), LANES), U32), pltpu.SemaphoreType.DMA((2,))],
        compiler_params=_cparams("arbitrary"),
        name="combine",
    )(pos_tiles, pos_tiles, x, route, g, y)


def _moe_ffn_final(x, g_ffn, wr, wg, wu, wd, g_final):
    m, d = x.shape
    tile = min(FFN_ROW_TILE, m)
    n_tiles = (2 * m) // tile + N_EXPERTS
    tt = min(TOK_TILE, m)
    sl = d // (2 * LANES)
    hs, route = _router(x, g_ffn, wr)
    pos, meta = _route_plan(route, tile, n_tiles)
    pos_tiles = pos[:2].reshape(2, m // tt, tt).transpose(1, 0, 2).reshape(m // tt, 1, 2 * tt)
    xs = _dispatch(hs, pos_tiles, meta[1, :2 * N_EXPERTS + 1], n_tiles * tile, tile, sl)
    y = _expert_ffn(xs, meta[0, :n_tiles], meta[2, :1], wg, wu, wd, tile, sl)
    return _combine(x, route, g_final, y, pos_tiles)


def kernel(x, ln_mix, w_in, w_alpha, b_alpha, gla_norm, w_pool, pool_scale, w_branch_gla, w_branch_pool, w_out,
           ln_ffn, ffn_w_gate, ffn_w_up, ffn_w_down, router_w, exp_w_gate, exp_w_up, exp_w_down, ln_final):
    batch, seq, d = x.shape
    depth = w_in.shape[0]
    dk = w_alpha.shape[2]
    dv = w_branch_gla.shape[1]
    pw = w_branch_pool.shape[1]
    rank = w_alpha.shape[1]
    assert depth == 2 and rank <= LANES
    a_col = 2 * dk + 2 * dv
    m = batch * seq
    xf = x.reshape(m, d)
    row = lambda v: v.reshape(1, -1)
    w_in_t = jnp.swapaxes(w_in, 1, 2)

    side_of_inproj = (exp_w_gate, exp_w_up)
    expert_bf16 = []
    for l in range(depth):
        w_main, w_a1 = _inproj_weight(w_in_t, l, a_col, rank)
        w_al = jnp.pad(w_alpha[l], ((0, LANES - rank), (0, 0)))
        z, a1, side_bf = _norm_inproj(xf, row(ln_mix[l]), w_main, w_a1, side_of_inproj[l], 0)
        expert_bf16.append(side_bf)
        og = _gla(z, a1, w_al, row(b_alpha[l]), row(gla_norm[l]), batch, seq, dk, dv)
        yp = _pool(z, _to_bf16(w_pool, l), row(pool_scale[l]), batch, seq, a_col)
        u = _merge(og, yp, z, _to_bf16(w_branch_gla, l), _to_bf16(w_branch_pool, l), a_col + pw, a_col + pw + d)
        xf = _out_proj(u, _to_bf16(w_out, l), xf)
        if l % 2 == 0:
            i = l // 2
            xf, side_bf = _ffn(xf, row(ln_ffn[l]), _to_bf16(ffn_w_gate, i), _to_bf16(ffn_w_up, i),
                               _to_bf16(ffn_w_down, i), exp_w_down, 0)
            expert_bf16.append(side_bf)
        else:
            wr = jnp.pad(router_w[l // 2], ((0, 0), (0, LANES - N_EXPERTS))).astype(BF16)
            e_gate, e_down, e_up = expert_bf16
            xf = _moe_ffn_final(xf, row(ln_ffn[l]), wr, e_gate, e_up, e_down, ln_final.reshape(1, -1))
    return xf.reshape(batch, seq, d)
```

```python
import functools
import math

import jax
import jax.numpy as jnp
from jax import lax
from jax.experimental import pallas as pl
from jax.experimental.pallas import tpu as pltpu

F32 = jnp.float32
BF16 = jnp.bfloat16
I32 = jnp.int32

EPS = 1e-6
GLA_HEADS = 4
GLA_LOWRANK = 16
GLA_GATE_TEMP = 16.0
GLA_CHUNK = 64
GLA_HEADS_PER_STEP = 4
POOL_WINDOWS = (2, 4, 8, 16)
POOL_GROUPS = 4
N_EXPERTS = 8

LANES = 128
VMEM_LIMIT_BYTES = 56 * 1024 * 1024

ROW_TILE = 1024
COL_TILE = 1024
FFN_ROW_TILE = 512
DENSE_FFN_ROW_TILE = 512
FFN_COL_TILE = 512
EXPERT_FFN_COL_TILE = 1408
SEQ_TILE = 512
TOK_TILE = 256
PLAN_TILE = 512
HALO = 16


def _cparams(*sem):
    return pltpu.CompilerParams(dimension_semantics=sem, vmem_limit_bytes=VMEM_LIMIT_BYTES)


def _dot(a, b):
    return jnp.dot(a, b, preferred_element_type=F32)


def _dot_nt(a, b):
    return lax.dot_general(a, b, (((1,), (1,)), ((), ())), preferred_element_type=F32)


def _split_bf16(x):
    hi = x.astype(BF16)
    lo = (x - hi.astype(F32)).astype(BF16)
    return hi, lo


def _pick(n, target, align=LANES):
    if n <= target:
        return n
    best = None
    for cand in range(align, target + 1, align):
        if n % cand == 0:
            best = cand
    assert best is not None, (n, target, align)
    return best


def _cast_kernel(x_ref, o_ref):
    o_ref[...] = x_ref[...].astype(BF16)


def _to_bf16(w, layer):
    shape = w.shape[1:]
    w3 = w.reshape(w.shape[0], -1, shape[-1])
    _, r, c = w3.shape
    br, bc = _pick(r, 1024, 16), _pick(c, 2048)
    out = pl.pallas_call(
        _cast_kernel,
        grid=(r // br, c // bc),
        in_specs=[pl.BlockSpec((None, br, bc), lambda i, j: (layer, i, j))],
        out_specs=pl.BlockSpec((br, bc), lambda i, j: (i, j)),
        out_shape=jax.ShapeDtypeStruct((r, c), BF16),
        compiler_params=_cparams("parallel", "parallel"),
        name="to_bf16",
    )(w3)
    return out.reshape(shape)


def _inproj_weight_kernel(w_ref, nxt_ref, lr_ref, wm_ref, wa_ref, *, first_shifted, rank):
    j = pl.program_id(0)

    @pl.when(j < first_shifted)
    def _():
        wm_ref[...] = w_ref[...].astype(BF16)

    @pl.when(j >= first_shifted)
    def _():
        wm_ref[...] = jnp.concatenate([w_ref[rank:, :], nxt_ref[...]], axis=0).astype(BF16)

    @pl.when(j == 0)
    def _():
        pad = jnp.zeros((wa_ref.shape[0] - rank, wa_ref.shape[1]), F32)
        wa_ref[...] = jnp.concatenate([lr_ref[...], pad], axis=0).astype(BF16)


def _inproj_weight(w_t, layer, a_col, rank):
    _, n_in, d = w_t.shape
    n = n_in - rank
    br = _pick(math.gcd(n, a_col), 512)
    assert a_col % br == 0 and rank % 8 == 0 and rank < LANES and br % rank == 0
    per = br // rank
    return pl.pallas_call(
        functools.partial(_inproj_weight_kernel, first_shifted=a_col // br, rank=rank),
        grid=(n // br,),
        in_specs=[
            pl.BlockSpec((None, br, d), lambda j: (layer, j, 0)),
            pl.BlockSpec((None, rank, d), lambda j: (layer, (j + 1) * per, 0)),
            pl.BlockSpec((None, rank, d), lambda j: (layer, a_col // rank, 0)),
        ],
        out_specs=[
            pl.BlockSpec((br, d), lambda j: (j, 0)),
            pl.BlockSpec((LANES, d), lambda j: (0, 0)),
        ],
        out_shape=[jax.ShapeDtypeStruct((n, d), BF16), jax.ShapeDtypeStruct((LANES, d), BF16)],
        compiler_params=_cparams("arbitrary"),
        name="inproj_weight",
    )(w_t, w_t, w_t)


class _SidePlan:
    def __init__(self, sides, n_outer, n_inner):
        steps = n_outer * n_inner
        self.arrays, self.in_specs, self.out_specs, self.out_shapes, self.shapes, blocks = [], [], [], [], [], []
        for side, layer in sides:
            side3 = side.reshape(side.shape[0], -1, side.shape[-1])
            r, c = side3.shape[1:]
            rb = next(cand for cand in range(16, r + 1, 16) if r % cand == 0 and r // cand <= steps)
            nblk = r // rb
            idx = functools.partial(lambda i, j, last: jnp.minimum(i * n_inner + j, last), last=nblk - 1)
            self.arrays.append(side3)
            self.in_specs.append(pl.BlockSpec(
                (None, rb, c), functools.partial(lambda i, j, idx, layer: (layer, idx(i, j), 0), idx=idx, layer=layer)))
            self.out_specs.append(pl.BlockSpec((rb, c), functools.partial(lambda i, j, idx: (idx(i, j), 0), idx=idx)))
            self.out_shapes.append(jax.ShapeDtypeStruct((r, c), BF16))
            self.shapes.append(side.shape[1:])
            blocks.append(nblk)
        self.blocks = tuple(blocks)

    def finish(self, outs):
        return [o.reshape(s) for o, s in zip(outs, self.shapes)]


def _side_casts(blocks, side_refs, side_out_refs):
    step = pl.program_id(0) * pl.num_programs(1) + pl.program_id(1)
    for n_blocks, src, dst in zip(blocks, side_refs, side_out_refs):
        @pl.when(step < n_blocks)
        def _():
            dst[...] = src[...].astype(BF16)


def _norm_inproj_kernel(x_ref, g_ref, w_ref, wa_ref, *rest, rows, side_blocks):
    k = len(side_blocks)
    z_ref, a1_ref = rest[k:k + 2]
    h_ref = rest[2 * k + 2]
    _side_casts(side_blocks, rest[:k], rest[k + 2:2 * k + 2])

    @pl.when(pl.program_id(1) == 0)
    def _():
        def body(c, carry):
            r = pl.multiple_of(c * rows, rows)
            xf = x_ref[pl.ds(r, rows), :]
            ms = jnp.mean(xf * xf, axis=-1, keepdims=True)
            h_ref[pl.ds(r, rows), :] = (xf * lax.rsqrt(ms + EPS) * g_ref[...]).astype(BF16)
            return carry
        lax.fori_loop(0, x_ref.shape[0] // rows, body, 0)
        a1_ref[...] = _dot_nt(h_ref[...], wa_ref[...])

    z_ref[...] = _dot_nt(h_ref[...], w_ref[...]).astype(BF16)


def _norm_inproj(x, g, w_t, wa_t, sides):
    m, d = x.shape
    n = w_t.shape[0]
    bm, bn = min(ROW_TILE, m), _pick(n, COL_TILE)
    assert m % bm == 0 and n % bn == 0
    ni, nj = m // bm, n // bn
    plan = _SidePlan(sides, ni, nj)
    z, a1, *side_out = pl.pallas_call(
        functools.partial(_norm_inproj_kernel, rows=128, side_blocks=plan.blocks),
        grid=(ni, nj),
        in_specs=[
            pl.BlockSpec((bm, d), lambda i, j: (i, 0)),
            pl.BlockSpec((1, d), lambda i, j: (0, 0)),
            pl.BlockSpec((bn, d), lambda i, j: (j, 0)),
            pl.BlockSpec((LANES, d), lambda i, j: (0, 0)),
        ] + plan.in_specs,
        out_specs=[
            pl.BlockSpec((bm, bn), lambda i, j: (i, j)),
            pl.BlockSpec((bm, LANES), lambda i, j: (i, 0)),
        ] + plan.out_specs,
        out_shape=[jax.ShapeDtypeStruct((m, n), BF16), jax.ShapeDtypeStruct((m, LANES), F32)] + plan.out_shapes,
        scratch_shapes=[pltpu.VMEM((bm, d), BF16)],
        compiler_params=_cparams("arbitrary", "arbitrary"),
        name="norm_inproj",
    )(x, g, w_t, wa_t, *plan.arrays)
    return z, a1, plan.finish(side_out)


def _log_sigmoid(x):
    return jnp.minimum(x, 0.0) - jnp.log(1.0 + jnp.exp(-jnp.abs(x)))


def _gla_kernel(q_ref, k_ref, v_ref, r_ref, a1_ref, wal_ref, bal_ref, gn_ref, tri_ref, o_ref, st_ref, *,
                chunk, heads):
    @pl.when(pl.program_id(2) == 0)
    def _():
        st_ref[...] = jnp.zeros_like(st_ref)

    hk = q_ref.shape[1] // heads
    hv = v_ref.shape[1] // heads
    a1 = _split_bf16(a1_ref[...])
    for h in range(heads):
        ks, vs = slice(h * hk, (h + 1) * hk), slice(h * hv, (h + 1) * hv)
        o, st_new = _gla_head(q_ref[:, ks], k_ref[:, ks], v_ref[:, vs], r_ref[:, vs], a1, wal_ref[:, ks],
                              bal_ref[:, ks], gn_ref[...], tri_ref[...], st_ref[h], chunk)
        o_ref[:, vs] = o
        st_ref[h] = st_new


def _gla_head(q, k, v, r, a1, wal, bal, gn, tri, st, chunk):
    t, dk = q.shape
    nc = t // chunk
    pair = 2 * chunk
    shift = chunk.bit_length() - 1
    nt_dot = lambda a, bb: lax.dot_general(a, bb, (((1,), (1,)), ((), ())), preferred_element_type=F32)
    rows = lambda x, c0, c1: x[c0 * chunk:c1 * chunk, :]

    a_hi, a_lo = a1
    w_hi, w_lo = _split_bf16(wal)
    xg = _dot(a_hi, w_hi) + _dot(a_hi, w_lo) + _dot(a_lo, w_hi) + bal
    la = _log_sigmoid(xg) * (1.0 / GLA_GATE_TEMP)

    la_hi, la_lo = _split_bf16(la)
    b = _dot(tri, la_hi) + _dot(tri, la_lo)

    bl = [b[(c + 1) * chunk - 1:(c + 1) * chunk, :] for c in range(nc)]
    pre = [jnp.zeros_like(bl[0])]
    for c in range(nc):
        pre.append(pre[c] + bl[c])

    q = q.astype(F32)
    k = k.astype(F32)
    q_t = q * jnp.exp(b) * (dk ** -0.5)
    k_t = (k * jnp.exp(-b)).astype(BF16)
    qc = [rows(q_t, c, c + 1) for c in range(nc)]
    kec = [rows(k, c, c + 1) * jnp.exp(bl[c] - rows(b, c, c + 1)) for c in range(nc)]
    q_tb = q_t.astype(BF16)
    k_eb = jnp.concatenate(kec, axis=0).astype(BF16)

    def q_from(mid, hi):
        return jnp.concatenate([qc[c] if c == mid else qc[c] * jnp.exp(pre[c] - pre[mid])
                                for c in range(mid, hi)], axis=0).astype(BF16)

    def k_upto(lo, mid):
        return jnp.concatenate([kec[m] if m == mid - 1 else kec[m] * jnp.exp(pre[mid] - pre[m + 1])
                                for m in range(lo, mid)], axis=0).astype(BF16)

    nb = t // pair
    blocks = [[None] * nb for _ in range(nb)]
    r2 = lax.broadcasted_iota(I32, (pair, pair), 0)
    c2 = lax.broadcasted_iota(I32, (pair, pair), 1)
    same = ((r2 >> shift) == (c2 >> shift)) & (r2 >= c2)
    cross = (r2 >= chunk) & (c2 < chunk)
    for p in range(nb):
        qp = rows(q_tb, 2 * p, 2 * p + 2)
        in_chunk = nt_dot(qp, rows(k_t, 2 * p, 2 * p + 2))
        next_chunk = nt_dot(qp, rows(k_eb, 2 * p, 2 * p + 2))
        blocks[p][p] = jnp.where(same, in_chunk, jnp.where(cross, next_chunk, 0.0))
    g = 4
    while g <= nc:
        half = g // 2
        for grp in range(nc // g):
            lo, mid, hi = grp * g, grp * g + half, (grp + 1) * g
            x = nt_dot(q_from(mid, hi), k_upto(lo, mid))
            hb = half // 2
            for i in range(hb):
                for j in range(hb):
                    blocks[mid // 2 + i][lo // 2 + j] = x[i * pair:(i + 1) * pair, j * pair:(j + 1) * pair]
        g *= 2
    zero_blk = jnp.zeros((pair, pair), F32)
    att = jnp.concatenate(
        [jnp.concatenate([zero_blk if blk is None else blk for blk in brow], axis=1) for brow in blocks], axis=0)

    o = _dot(att.astype(BF16), v) + nt_dot(q_from(0, nc), st.astype(BF16))
    upd = lax.dot_general(v, k_upto(0, nc), (((0,), (0,)), ((), ())), preferred_element_type=F32)
    st_new = st * jnp.exp(pre[nc]) + upd

    o = o * lax.rsqrt(jnp.mean(o * o, axis=-1, keepdims=True) + EPS) * gn
    r = r.astype(F32)
    return (o * (r * jax.nn.sigmoid(r))).astype(BF16), st_new


def _gla(z, a1, w_alpha, b_alpha, gla_norm, batch, seq, dk_total, dv_total):
    m = z.shape[0]
    hk, hv = dk_total // GLA_HEADS, dv_total // GLA_HEADS
    t = min(SEQ_TILE, seq)
    assert seq % t == 0 and t % GLA_CHUNK == 0 and hk % LANES == 0 and hv % LANES == 0
    ns = seq // t
    nc = t // GLA_CHUNK
    assert nc >= 2 and nc & (nc - 1) == 0 and GLA_HEADS % GLA_HEADS_PER_STEP == 0
    hps = GLA_HEADS_PER_STEP
    gk, gv = hps * hk, hps * hv
    k_off = dk_total // gk
    v_off = (2 * dk_total) // gv
    r_off = (2 * dk_total + dv_total) // gv
    idx = jnp.arange(t, dtype=I32)
    tri = ((idx[:, None] // GLA_CHUNK == idx[None, :] // GLA_CHUNK) & (idx[:, None] >= idx[None, :])).astype(BF16)
    rowi = lambda b, h, s: b * ns + s
    return pl.pallas_call(
        functools.partial(_gla_kernel, chunk=GLA_CHUNK, heads=hps),
        grid=(batch, GLA_HEADS // hps, ns),
        in_specs=[
            pl.BlockSpec((t, gk), lambda b, h, s: (rowi(b, h, s), h)),
            pl.BlockSpec((t, gk), lambda b, h, s: (rowi(b, h, s), k_off + h)),
            pl.BlockSpec((t, gv), lambda b, h, s: (rowi(b, h, s), v_off + h)),
            pl.BlockSpec((t, gv), lambda b, h, s: (rowi(b, h, s), r_off + h)),
            pl.BlockSpec((t, LANES), lambda b, h, s: (rowi(b, h, s), 0)),
            pl.BlockSpec((LANES, gk), lambda b, h, s: (0, h)),
            pl.BlockSpec((1, gk), lambda b, h, s: (0, h)),
            pl.BlockSpec((1, hv), lambda b, h, s: (0, 0)),
            pl.BlockSpec((t, t), lambda b, h, s: (0, 0)),
        ],
        out_specs=pl.BlockSpec((t, gv), lambda b, h, s: (rowi(b, h, s), h)),
        out_shape=jax.ShapeDtypeStruct((m, dv_total), BF16),
        scratch_shapes=[pltpu.VMEM((hps, hv, hk), F32)],
        compiler_params=_cparams("parallel", "parallel", "arbitrary"),
        name="gla",
    )(z, z, z, z, a1, w_alpha, b_alpha, gla_norm, tri)


def _pool_kernel(p_ref, ph_ref, wp_ref, sc_ref, y_ref, *, windows):
    s = pl.program_id(1)
    t = p_ref.shape[0]
    halo = ph_ref.shape[0]
    gw = wp_ref.shape[1]
    dist = lax.broadcasted_iota(I32, (t, t), 0) - lax.broadcasted_iota(I32, (t, t), 1)
    dist_h = lax.broadcasted_iota(I32, (t, halo), 0) + halo - lax.broadcasted_iota(I32, (t, halo), 1)
    pos = s * t + lax.broadcasted_iota(I32, (t, 1), 0)
    for g, w in enumerate(windows):
        cs = slice(g * gw, (g + 1) * gw)
        pc = p_ref[:, cs]
        band = jnp.where((dist >= 0) & (dist < w), 1.0, 0.0).astype(BF16)
        band_h = jnp.where(dist_h < w, 1.0, 0.0).astype(BF16)
        tot = _dot(band, pc) + jnp.where(s > 0, _dot(band_h, ph_ref[:, cs]), 0.0)
        cnt = jnp.minimum(pos + 1, w).astype(F32)
        mixed = tot / cnt - pc.astype(F32)
        y = _dot(mixed.astype(BF16), wp_ref[g]) * sc_ref[:, cs]
        y_ref[:, cs] = y.astype(BF16)


def _pool(z, w_pool, pool_scale, batch, seq, p_col0):
    m = z.shape[0]
    groups, gw, _ = w_pool.shape
    pw = groups * gw
    t = min(SEQ_TILE, seq)
    assert seq % t == 0 and t % HALO == 0 and p_col0 % pw == 0 and gw % LANES == 0
    ns = seq // t
    pc = p_col0 // pw
    hb = t // HALO
    return pl.pallas_call(
        functools.partial(_pool_kernel, windows=POOL_WINDOWS),
        grid=(batch, ns),
        in_specs=[
            pl.BlockSpec((t, pw), lambda b, s: (b * ns + s, pc)),
            pl.BlockSpec((HALO, pw), lambda b, s: (jnp.maximum((b * ns + s) * hb - 1, 0), pc)),
            pl.BlockSpec((groups, gw, gw), lambda b, s: (0, 0, 0)),
            pl.BlockSpec((1, pw), lambda b, s: (0, 0)),
        ],
        out_specs=pl.BlockSpec((t, pw), lambda b, s: (b * ns + s, 0)),
        out_shape=jax.ShapeDtypeStruct((m, pw), BF16),
        compiler_params=_cparams("parallel", "arbitrary"),
        name="pool",
    )(z, z, w_pool, pool_scale)


def _merge_kernel(og_ref, yp_ref, ga_ref, gb_ref, wa_ref, wb_ref, u_ref):
    ya = _dot(og_ref[...], wa_ref[...])
    yb = _dot(yp_ref[...], wb_ref[...])
    ga = jax.nn.sigmoid(ga_ref[...].astype(F32))
    gb = jax.nn.sigmoid(gb_ref[...].astype(F32))
    u_ref[...] = (ga * ya + gb * yb).astype(BF16)


def _merge(og, yp, z, w_a, w_b, ga_col0, gb_col0):
    m, dv = og.shape
    pw = yp.shape[1]
    d = w_a.shape[1]
    bm, bn = min(ROW_TILE, m), _pick(math.gcd(d, ga_col0, gb_col0), COL_TILE)
    assert m % bm == 0 and d % bn == 0 and ga_col0 % bn == 0 and gb_col0 % bn == 0
    ga0, gb0 = ga_col0 // bn, gb_col0 // bn
    return pl.pallas_call(
        _merge_kernel,
        grid=(m // bm, d // bn),
        in_specs=[
            pl.BlockSpec((bm, dv), lambda i, j: (i, 0)),
            pl.BlockSpec((bm, pw), lambda i, j: (i, 0)),
            pl.BlockSpec((bm, bn), lambda i, j: (i, ga0 + j)),
            pl.BlockSpec((bm, bn), lambda i, j: (i, gb0 + j)),
            pl.BlockSpec((dv, bn), lambda i, j: (0, j)),
            pl.BlockSpec((pw, bn), lambda i, j: (0, j)),
        ],
        out_specs=pl.BlockSpec((bm, bn), lambda i, j: (i, j)),
        out_shape=jax.ShapeDtypeStruct((m, d), BF16),
        compiler_params=_cparams("parallel", "arbitrary"),
        name="merge",
    )(og, yp, z, z, w_a, w_b)


def _out_proj_kernel(u_ref, w_ref, x_ref, o_ref):
    o_ref[...] = x_ref[...] + _dot(u_ref[...], w_ref[...])


def _out_proj(u, w, x):
    m, d = x.shape
    bm, bn = min(ROW_TILE, m), _pick(d, COL_TILE)
    assert m % bm == 0 and d % bn == 0
    return pl.pallas_call(
        _out_proj_kernel,
        grid=(m // bm, d // bn),
        in_specs=[
            pl.BlockSpec((bm, u.shape[1]), lambda i, j: (i, 0)),
            pl.BlockSpec((u.shape[1], bn), lambda i, j: (0, j)),
            pl.BlockSpec((bm, bn), lambda i, j: (i, j)),
        ],
        out_specs=pl.BlockSpec((bm, bn), lambda i, j: (i, j)),
        out_shape=jax.ShapeDtypeStruct((m, d), F32),
        compiler_params=_cparams("parallel", "arbitrary"),
        name="out_proj",
    )(u, w, x)


def _rmsnorm_rows(x_ref, g_ref, h_ref, rows):
    def body(c, carry):
        r = pl.multiple_of(c * rows, rows)
        xf = x_ref[pl.ds(r, rows), :]
        ms = jnp.mean(xf * xf, axis=-1, keepdims=True)
        h_ref[pl.ds(r, rows), :] = (xf * lax.rsqrt(ms + EPS) * g_ref[...]).astype(BF16)
        return carry
    lax.fori_loop(0, x_ref.shape[0] // rows, body, 0)


def _swiglu_block(h, wg, wu, wd):
    g = _dot(h, wg)
    u = _dot(h, wu)
    a = (g * jax.nn.sigmoid(g) * u).astype(BF16)
    return _dot(a, wd)


def _ffn_kernel(x_ref, g_ref, wg_ref, wu_ref, wd_ref, *rest, side_blocks):
    f = pl.program_id(1)
    k = len(side_blocks)
    o_ref = rest[k]
    h_ref = rest[2 * k + 1]
    _side_casts(side_blocks, rest[:k], rest[k + 1:2 * k + 1])

    @pl.when(f == 0)
    def _():
        _rmsnorm_rows(x_ref, g_ref, h_ref, 128)
        o_ref[...] = x_ref[...]

    o_ref[...] += _swiglu_block(h_ref[...], wg_ref[...], wu_ref[...], wd_ref[...])


def _ffn(x, g, wg, wu, wd, sides):
    m, d = x.shape
    ff = wg.shape[1]
    bm, bf = min(DENSE_FFN_ROW_TILE, m), min(FFN_COL_TILE, ff)
    assert m % bm == 0 and ff % bf == 0
    ni, nf = m // bm, ff // bf
    plan = _SidePlan(sides, ni, nf)
    out, *side_out = pl.pallas_call(
        functools.partial(_ffn_kernel, side_blocks=plan.blocks),
        grid=(ni, nf),
        in_specs=[
            pl.BlockSpec((bm, d), lambda i, f: (i, 0)),
            pl.BlockSpec((1, d), lambda i, f: (0, 0)),
            pl.BlockSpec((d, bf), lambda i, f: (0, f)),
            pl.BlockSpec((d, bf), lambda i, f: (0, f)),
            pl.BlockSpec((bf, d), lambda i, f: (f, 0)),
        ] + plan.in_specs,
        out_specs=[pl.BlockSpec((bm, d), lambda i, f: (i, 0))] + plan.out_specs,
        out_shape=[jax.ShapeDtypeStruct((m, d), F32)] + plan.out_shapes,
        scratch_shapes=[pltpu.VMEM((bm, d), BF16)],
        compiler_params=_cparams("arbitrary", "arbitrary"),
        name="ffn",
    )(x, g, wg, wu, wd, *plan.arrays)
    return out, plan.finish(side_out)


U32 = jnp.uint32


def _to_slabs(slab_ref, x):
    rows, d = x.shape
    half = d // 2
    sl = half // LANES
    lo = lax.bitcast_convert_type(x[:, :half], U32) >> 16
    hi = (lax.bitcast_convert_type(x[:, half:], U32) >> 16) << 16
    words = hi | lo
    for s in range(sl):
        slab_ref[pl.ds(s, rows, stride=sl), :] = words[:, s * LANES:(s + 1) * LANES]


def _from_slabs(slab_ref, rows, d):
    sl = d // (2 * LANES)
    words = jnp.concatenate([slab_ref[pl.ds(s, rows, stride=sl), :] for s in range(sl)], axis=1)
    lo = lax.bitcast_convert_type(words << 16, F32)
    hi = lax.bitcast_convert_type((words >> 16) << 16, F32)
    return jnp.concatenate([lo, hi], axis=1)


def _router_kernel(x_ref, g_ref, wr_ref, hs_ref, route_ref, hb_ref):
    _rmsnorm_rows(x_ref, g_ref, hb_ref, 128)
    hb = hb_ref[...]
    _to_slabs(hs_ref, hb.astype(F32))
    logits = _dot(hb, wr_ref[...])
    lane = lax.broadcasted_iota(I32, logits.shape, 1)
    lane_f = lane.astype(F32)
    neg = jnp.float32(-jnp.inf)
    l1 = jnp.where(lane < N_EXPERTS, logits, neg)
    m1 = jnp.max(l1, axis=-1, keepdims=True)
    i1 = jnp.min(jnp.where(l1 == m1, lane_f, float(LANES)), axis=-1, keepdims=True)
    l2 = jnp.where(lane_f == i1, neg, l1)
    m2 = jnp.max(l2, axis=-1, keepdims=True)
    i2 = jnp.min(jnp.where(l2 == m2, lane_f, float(LANES)), axis=-1, keepdims=True)
    e = jnp.exp(m2 - m1)
    den = 1.0 + e
    route = jnp.where(lane == 0, i1, 0.0)
    route = jnp.where(lane == 1, i2, route)
    route = jnp.where(lane == 2, 1.0 / den, route)
    route = jnp.where(lane == 3, e / den, route)
    route_ref[...] = route


def _router(x, g, wr):
    m, d = x.shape
    bm = min(FFN_ROW_TILE, m)
    sl = d // (2 * LANES)
    assert m % bm == 0 and sl % 8 == 0
    return pl.pallas_call(
        _router_kernel,
        grid=(m // bm,),
        in_specs=[
            pl.BlockSpec((bm, d), lambda i: (i, 0)),
            pl.BlockSpec((1, d), lambda i: (0, 0)),
            pl.BlockSpec((d, LANES), lambda i: (0, 0)),
        ],
        out_specs=[
            pl.BlockSpec((bm * sl, LANES), lambda i: (i, 0)),
            pl.BlockSpec((bm, LANES), lambda i: (i, 0)),
        ],
        out_shape=[jax.ShapeDtypeStruct((m * sl, LANES), U32), jax.ShapeDtypeStruct((m, LANES), F32)],
        scratch_shapes=[pltpu.VMEM((bm, d), BF16)],
        compiler_params=_cparams("parallel"),
        name="router",
    )(x, g, wr)


def _route_plan_kernel(route_ref, pos_ref, meta_ref, rank_ref, *, tile, n_tiles, blk):
    m = route_ref.shape[0]
    nb = m // blk
    lane_b = lax.broadcasted_iota(I32, (blk, LANES), 1)
    lane_r = lax.broadcasted_iota(I32, (1, LANES), 1)
    ne = N_EXPERTS

    def onehot(r):
        rt = route_ref[pl.ds(r, blk), :]
        i1 = rt[:, 0:1].astype(I32)
        i2 = rt[:, 1:2].astype(I32)
        return jnp.where((lane_b == i1) | (lane_b == i2 + ne), 1.0, 0.0)

    strict = jnp.where(lax.broadcasted_iota(I32, (blk, blk), 0) > lax.broadcasted_iota(I32, (blk, blk), 1),
                       1.0, 0.0).astype(BF16)

    def rank_body(c, carry):
        r = pl.multiple_of(c * blk, blk)
        oh = onehot(r)
        rank_ref[pl.ds(r, blk), :] = _dot(strict, oh.astype(BF16)) + carry
        return carry + jnp.sum(oh, axis=0, keepdims=True)

    cnt12 = lax.fori_loop(0, nb, rank_body, jnp.zeros((1, LANES), F32))

    def lane_val(row, l):
        return jnp.sum(jnp.where(lane_r == l, row, 0.0), axis=-1, keepdims=True)

    cnt = jnp.zeros((1, LANES), F32)
    for e in range(ne):
        cnt = jnp.where(lane_r == e, lane_val(cnt12, e) + lane_val(cnt12, e + ne), cnt)
    tiles = jnp.floor((cnt + (tile - 1)) * (1.0 / tile))
    start = jnp.zeros((1, LANES), F32)
    for e in range(1, ne):
        start = start + jnp.where(lane_r >= e, lane_val(tiles, e - 1), 0.0)
    start = jnp.where(lane_r < ne, start * tile, 0.0)
    end = start + tiles * tile
    base = start
    for e in range(ne):
        base = jnp.where(lane_r == e + ne, lane_val(start, e) + lane_val(cnt12, e), base)

    def pos_body(c, carry):
        r = pl.multiple_of(c * blk, blk)
        slot = onehot(r) * (rank_ref[pl.ds(r, blk), :] + base)
        p1 = jnp.sum(jnp.where(lane_b < ne, slot, 0.0), axis=-1, keepdims=True)
        p2 = jnp.sum(jnp.where(lane_b >= ne, slot, 0.0), axis=-1, keepdims=True)
        both = jnp.where(lane_b == 0, p1, jnp.where(lane_b == 1, p2, 0.0))
        pos_ref[:, pl.ds(r, blk)] = jnp.transpose(both)[0:8, :].astype(I32)
        return carry

    lax.fori_loop(0, nb, pos_body, 0)

    n_used = lane_val(end, ne - 1) * (1.0 / tile)
    tile_start = lane_r.astype(F32) * tile
    tile_e = jnp.zeros((1, LANES), F32)
    last_e = jnp.zeros((1, 1), F32)
    for e in range(ne):
        tile_e = tile_e + jnp.where(lane_val(end, e) <= tile_start, 1.0, 0.0)
        last_e = jnp.where(lane_val(cnt, e) > 0, float(e), last_e)
    tile_e = jnp.where(lane_r.astype(F32) < n_used, tile_e, last_e)
    info = jnp.where(lane_r < ne, end, jnp.where(lane_r == 2 * ne, n_used, 0.0))
    for e in range(ne):
        info = jnp.where(lane_r == e + ne, lane_val(cnt, e), info)
    row8 = lax.broadcasted_iota(I32, (8, LANES), 0)
    meta = jnp.where(row8 == 0, tile_e, jnp.where(row8 == 1, info, jnp.where(row8 == 2, n_used, 0.0)))
    meta_ref[...] = meta.astype(I32)


def _route_plan(route, tile, n_tiles):
    m = route.shape[0]
    blk = min(PLAN_TILE, m)
    assert m % blk == 0 and n_tiles <= LANES
    return pl.pallas_call(
        functools.partial(_route_plan_kernel, tile=tile, n_tiles=n_tiles, blk=blk),
        out_shape=[jax.ShapeDtypeStruct((8, m), I32), jax.ShapeDtypeStruct((8, LANES), I32)],
        scratch_shapes=[pltpu.VMEM((m, LANES), F32)],
        compiler_params=pltpu.CompilerParams(vmem_limit_bytes=VMEM_LIMIT_BYTES),
        name="route_plan",
    )(route)


def _dispatch_kernel(meta_ref, pos_ref, h_ref, xs_ref, zero_ref, sem, *, tile, sl):
    i = pl.program_id(0)
    tt = h_ref.shape[0] // sl
    n_tiles = xs_ref.shape[0] // (tile * sl)

    def zero_copy(e):
        first = pl.multiple_of((meta_ref[e] - tile) * sl, tile * sl)
        return pltpu.make_async_copy(zero_ref, xs_ref.at[pl.ds(first, tile * sl), :], sem.at[1])

    def tail_copy(j):
        return pltpu.make_async_copy(zero_ref, xs_ref.at[pl.ds(j * tile * sl, tile * sl), :], sem.at[1])

    @pl.when(i == 0)
    def _():
        zero_ref[...] = jnp.zeros_like(zero_ref)
        n_used = meta_ref[2 * N_EXPERTS]
        for e in range(N_EXPERTS):
            @pl.when(meta_ref[N_EXPERTS + e] > 0)
            def _():
                zero_copy(e).start()
        for j in range(n_tiles - N_EXPERTS, n_tiles):
            @pl.when(j >= n_used)
            def _():
                tail_copy(j).start()
        for e in range(N_EXPERTS):
            @pl.when(meta_ref[N_EXPERTS + e] > 0)
            def _():
                zero_copy(e).wait()
        for j in range(n_tiles - N_EXPERTS, n_tiles):
            @pl.when(j >= n_used)
            def _():
                tail_copy(j).wait()

    def row_copy(r, c):
        src = pl.multiple_of(r * sl, sl)
        dst = pl.multiple_of(pos_ref[0, c * tt + r] * sl, sl)
        return pltpu.make_async_copy(h_ref.at[pl.ds(src, sl), :], xs_ref.at[pl.ds(dst, sl), :], sem.at[0])

    def start_body(r, carry):
        row_copy(r, 0).start(priority=0)
        row_copy(r, 1).start(priority=1)
        return carry

    lax.fori_loop(0, tt, start_body, 0, unroll=8)
    for _ in range(2):
        pltpu.make_async_copy(h_ref, xs_ref.at[pl.ds(0, tt * sl), :], sem.at[0]).wait()


def _dispatch(hs, pos_tiles, ginfo, n_slots, tile, sl):
    m = hs.shape[0] // sl
    tt = pos_tiles.shape[2] // 2
    return pl.pallas_call(
        functools.partial(_dispatch_kernel, tile=tile, sl=sl),
        grid_spec=pltpu.PrefetchScalarGridSpec(
            num_scalar_prefetch=1,
            grid=(m // tt,),
            in_specs=[
                pl.BlockSpec((None, 1, 2 * tt), lambda i, meta: (i, 0, 0), memory_space=pltpu.SMEM),
                pl.BlockSpec((tt * sl, LANES), lambda i, meta: (i, 0)),
            ],
            out_specs=pl.BlockSpec(memory_space=pl.ANY),
            scratch_shapes=[pltpu.VMEM((tile * sl, LANES), U32), pltpu.SemaphoreType.DMA((2,))],
        ),
        out_shape=jax.ShapeDtypeStruct((n_slots * sl, LANES), U32),
        compiler_params=_cparams("arbitrary"),
        name="dispatch",
    )(ginfo, pos_tiles, hs)


def _expert_ffn_kernel(te_ref, nu_ref, xs_ref, wg_ref, wu_ref, wd_ref, y_ref, h_ref, acc_ref):
    i, f = pl.program_id(0), pl.program_id(1)
    used = i < nu_ref[0]
    last = pl.num_programs(1) - 1
    tile, d = h_ref.shape

    @pl.when(used & (f == 0))
    def _():
        h_ref[...] = _from_slabs(xs_ref, tile, d).astype(BF16)
        acc_ref[...] = jnp.zeros_like(acc_ref)

    @pl.when(used)
    def _():
        acc_ref[...] += _swiglu_block(h_ref[...], wg_ref[...], wu_ref[...], wd_ref[...])

    @pl.when(used & (f == last))
    def _():
        _to_slabs(y_ref, acc_ref[...].astype(BF16).astype(F32))

    @pl.when(jnp.logical_not(used) & (f == last))
    def _():
        y_ref[...] = jnp.zeros_like(y_ref)


def _expert_ffn(xs, tile_e, n_used, wg, wu, wd, tile, sl):
    n_slots = xs.shape[0] // sl
    d = 2 * sl * LANES
    ff = wg.shape[2]
    bf = _pick(ff, EXPERT_FFN_COL_TILE)
    assert n_slots % tile == 0 and ff % bf == 0
    nf = ff // bf
    row = lambda i, f, te, nu: (jnp.minimum(i, nu[0] - 1), 0)
    fblk = lambda i, f, nu: jnp.where(i < nu[0], f, nf - 1)
    return pl.pallas_call(
        _expert_ffn_kernel,
        grid_spec=pltpu.PrefetchScalarGridSpec(
            num_scalar_prefetch=2,
            grid=(n_slots // tile, nf),
            in_specs=[
                pl.BlockSpec((tile * sl, LANES), row),
                pl.BlockSpec((None, d, bf), lambda i, f, te, nu: (te[i], 0, fblk(i, f, nu))),
                pl.BlockSpec((None, d, bf), lambda i, f, te, nu: (te[i], 0, fblk(i, f, nu))),
                pl.BlockSpec((None, bf, d), lambda i, f, te, nu: (te[i], fblk(i, f, nu), 0)),
            ],
            out_specs=pl.BlockSpec((tile * sl, LANES), lambda i, f, te, nu: (i, 0)),
            scratch_shapes=[pltpu.VMEM((tile, d), BF16), pltpu.VMEM((tile, d), F32)],
        ),
        out_shape=jax.ShapeDtypeStruct((n_slots * sl, LANES), U32),
        compiler_params=_cparams("parallel", "arbitrary"),
        name="expert_ffn",
    )(tile_e, n_used, xs, wg, wu, wd)


def _combine_kernel(pos_ref, nxt_ref, x_ref, route_ref, g_ref, y_ref, o_ref, buf_ref, sem):
    i = pl.program_id(0)
    n = pl.num_programs(0)
    tt, d = x_ref.shape
    sl = d // (2 * LANES)
    slot = lax.rem(i, 2)

    def gather(p_ref, dst_slot):
        def row_copy(r, c):
            src = pl.multiple_of(p_ref[0, c * tt + r] * sl, sl)
            dst = pl.multiple_of(r * sl, sl)
            return pltpu.make_async_copy(y_ref.at[pl.ds(src, sl), :], buf_ref.at[dst_slot, c, pl.ds(dst, sl), :],
                                         sem.at[dst_slot])

        def start_body(r, carry):
            row_copy(r, 0).start()
            row_copy(r, 1).start()
            return carry

        lax.fori_loop(0, tt, start_body, 0, unroll=8)

    @pl.when(i == 0)
    def _():
        gather(pos_ref, 0)

    @pl.when(i + 1 < n)
    def _():
        gather(nxt_ref, 1 - slot)

    for c in range(2):
        pltpu.make_async_copy(y_ref.at[pl.ds(0, tt * sl), :], buf_ref.at[slot, c], sem.at[slot]).wait()

    rt = route_ref[...]
    moe = (rt[:, 2:3] * _from_slabs(buf_ref.at[slot, 0], tt, d)
           + rt[:, 3:4] * _from_slabs(buf_ref.at[slot, 1], tt, d))
    xo = x_ref[...] + moe
    ms = jnp.mean(xo * xo, axis=-1, keepdims=True)
    o_ref[...] = xo * lax.rsqrt(ms + EPS) * g_ref[...]


def _combine(x, route, g, y, pos_tiles):
    m, d = x.shape
    tt = pos_tiles.shape[2] // 2
    nt = m // tt
    return pl.pallas_call(
        _combine_kernel,
        grid=(nt,),
        in_specs=[
            pl.BlockSpec((None, 1, 2 * tt), lambda i: (i, 0, 0), memory_space=pltpu.SMEM),
            pl.BlockSpec((None, 1, 2 * tt), lambda i: (jnp.minimum(i + 1, nt - 1), 0, 0), memory_space=pltpu.SMEM),
            pl.BlockSpec((tt, d), lambda i: (i, 0)),
            pl.BlockSpec((tt, LANES), lambda i: (i, 0)),
            pl.BlockSpec((1, d), lambda i: (0, 0)),
            pl.BlockSpec(memory_space=pl.ANY),
        ],
        out_specs=pl.BlockSpec((tt, d), lambda i: (i, 0)),
        out_shape=jax.ShapeDtypeStruct((m, d), F32),
        scratch_shapes=[pltpu.VMEM((2, 2, tt * (d // (2 * LANES)), LANES), U32), pltpu.SemaphoreType.DMA((2,))],
        compiler_params=_cparams("arbitrary"),
        name="combine",
    )(pos_tiles, pos_tiles, x, route, g, y)


def _moe_ffn_final(x, g_ffn, wr, wg, wu, wd, g_final):
    m, d = x.shape
    tile = min(FFN_ROW_TILE, m)
    n_tiles = (2 * m) // tile + N_EXPERTS
    tt = min(TOK_TILE, m)
    sl = d // (2 * LANES)
    hs, route = _router(x, g_ffn, wr)
    pos, meta = _route_plan(route, tile, n_tiles)
    pos_tiles = pos[:2].reshape(2, m // tt, tt).transpose(1, 0, 2).reshape(m // tt, 1, 2 * tt)
    xs = _dispatch(hs, pos_tiles, meta[1, :2 * N_EXPERTS + 1], n_tiles * tile, tile, sl)
    y = _expert_ffn(xs, meta[0, :n_tiles], meta[2, :1], wg, wu, wd, tile, sl)
    return _combine(x, route, g_final, y, pos_tiles)


def kernel(x, ln_mix, w_in, w_alpha, b_alpha, gla_norm, w_pool, pool_scale, w_branch_gla, w_branch_pool, w_out,
           ln_ffn, ffn_w_gate, ffn_w_up, ffn_w_down, router_w, exp_w_gate, exp_w_up, exp_w_down, ln_final):
    batch, seq, d = x.shape
    depth = w_in.shape[0]
    dk = w_alpha.shape[2]
    dv = w_branch_gla.shape[1]
    pw = w_branch_pool.shape[1]
    rank = w_alpha.shape[1]
    assert depth == 2 and rank <= LANES
    a_col = 2 * dk + 2 * dv
    m = batch * seq
    xf = x.reshape(m, d)
    row = lambda v: v.reshape(1, -1)
    w_in_t = jnp.swapaxes(w_in, 1, 2)

    assert depth == 2 and exp_w_gate.shape[0] == 1 and ffn_w_gate.shape[0] == 1
    mixer = lambda l: (_to_bf16(w_pool, l), _to_bf16(w_branch_gla, l), _to_bf16(w_branch_pool, l),
                       _to_bf16(w_out, l))
    for l in range(depth):
        w_main, w_a1 = _inproj_weight(w_in_t, l, a_col, rank)
        w_al = jnp.pad(w_alpha[l], ((0, LANES - rank), (0, 0)))
        if l == 0:
            z, a1, (e_gate, f_gate, f_up, f_down) = _norm_inproj(
                xf, row(ln_mix[l]), w_main, w_a1,
                [(exp_w_gate, 0), (ffn_w_gate, 0), (ffn_w_up, 0), (ffn_w_down, 0)])
            wp, wbg, wbp, wo = mixer(0)
        else:
            z, a1, (e_up,) = _norm_inproj(xf, row(ln_mix[l]), w_main, w_a1, [(exp_w_up, 0)])
        og = _gla(z, a1, w_al, row(b_alpha[l]), row(gla_norm[l]), batch, seq, dk, dv)
        yp = _pool(z, wp, row(pool_scale[l]), batch, seq, a_col)
        u = _merge(og, yp, z, wbg, wbp, a_col + pw, a_col + pw + d)
        xf = _out_proj(u, wo, xf)
        if l == 0:
            xf, (e_down, wp, wbg, wbp, wo) = _ffn(
                xf, row(ln_ffn[l]), f_gate, f_up, f_down,
                [(exp_w_down, 0), (w_pool, 1), (w_branch_gla, 1), (w_branch_pool, 1), (w_out, 1)])
        else:
            wr = jnp.pad(router_w[0], ((0, 0), (0, LANES - N_EXPERTS))).astype(BF16)
            xf = _moe_ffn_final(xf, row(ln_ffn[l]), wr, e_gate, e_up, e_down, ln_final.reshape(1, -1))
    return xf.reshape(batch, seq, d)
```

```python
import functools
import math

import jax
import jax.numpy as jnp
from jax import lax
from jax.experimental import pallas as pl
from jax.experimental.pallas import tpu as pltpu

F32 = jnp.float32
BF16 = jnp.bfloat16
I32 = jnp.int32

EPS = 1e-6
GLA_HEADS = 4
GLA_LOWRANK = 16
GLA_GATE_TEMP = 16.0
GLA_CHUNK = 64
GLA_HEADS_PER_STEP = 4
POOL_WINDOWS = (2, 4, 8, 16)
POOL_GROUPS = 4
N_EXPERTS = 8

LANES = 128
VMEM_LIMIT_BYTES = 56 * 1024 * 1024

ROW_TILE = 1024
COL_TILE = 1024
FFN_ROW_TILE = 512
DENSE_FFN_ROW_TILE = 512
FFN_COL_TILE = 512
SEQ_TILE = 512
TOK_TILE = 256
PLAN_TILE = 512
HALO = 16


def _cparams(*sem):
    return pltpu.CompilerParams(dimension_semantics=sem, vmem_limit_bytes=VMEM_LIMIT_BYTES)


def _dot(a, b):
    return jnp.dot(a, b, preferred_element_type=F32)


def _dot_nt(a, b):
    return lax.dot_general(a, b, (((1,), (1,)), ((), ())), preferred_element_type=F32)


def _split_bf16(x):
    hi = x.astype(BF16)
    lo = (x - hi.astype(F32)).astype(BF16)
    return hi, lo


def _pick(n, target, align=LANES):
    if n <= target:
        return n
    best = None
    for cand in range(align, target + 1, align):
        if n % cand == 0:
            best = cand
    assert best is not None, (n, target, align)
    return best


def _cast_kernel(x_ref, o_ref):
    o_ref[...] = x_ref[...].astype(BF16)


def _to_bf16(w, layer):
    shape = w.shape[1:]
    w3 = w.reshape(w.shape[0], -1, shape[-1])
    _, r, c = w3.shape
    br, bc = _pick(r, 1024, 16), _pick(c, 2048)
    out = pl.pallas_call(
        _cast_kernel,
        grid=(r // br, c // bc),
        in_specs=[pl.BlockSpec((None, br, bc), lambda i, j: (layer, i, j))],
        out_specs=pl.BlockSpec((br, bc), lambda i, j: (i, j)),
        out_shape=jax.ShapeDtypeStruct((r, c), BF16),
        compiler_params=_cparams("parallel", "parallel"),
        name="to_bf16",
    )(w3)
    return out.reshape(shape)


def _inproj_weight_kernel(w_ref, nxt_ref, lr_ref, wm_ref, wa_ref, *, first_shifted, rank):
    j = pl.program_id(0)

    @pl.when(j < first_shifted)
    def _():
        wm_ref[...] = w_ref[...].astype(BF16)

    @pl.when(j >= first_shifted)
    def _():
        wm_ref[...] = jnp.concatenate([w_ref[rank:, :], nxt_ref[...]], axis=0).astype(BF16)

    @pl.when(j == 0)
    def _():
        pad = jnp.zeros((wa_ref.shape[0] - rank, wa_ref.shape[1]), F32)
        wa_ref[...] = jnp.concatenate([lr_ref[...], pad], axis=0).astype(BF16)


def _inproj_weight(w_t, layer, a_col, rank):
    _, n_in, d = w_t.shape
    n = n_in - rank
    br = _pick(math.gcd(n, a_col), 512)
    assert a_col % br == 0 and rank % 8 == 0 and rank < LANES and br % rank == 0
    per = br // rank
    return pl.pallas_call(
        functools.partial(_inproj_weight_kernel, first_shifted=a_col // br, rank=rank),
        grid=(n // br,),
        in_specs=[
            pl.BlockSpec((None, br, d), lambda j: (layer, j, 0)),
            pl.BlockSpec((None, rank, d), lambda j: (layer, (j + 1) * per, 0)),
            pl.BlockSpec((None, rank, d), lambda j: (layer, a_col // rank, 0)),
        ],
        out_specs=[
            pl.BlockSpec((br, d), lambda j: (j, 0)),
            pl.BlockSpec((LANES, d), lambda j: (0, 0)),
        ],
        out_shape=[jax.ShapeDtypeStruct((n, d), BF16), jax.ShapeDtypeStruct((LANES, d), BF16)],
        compiler_params=_cparams("arbitrary"),
        name="inproj_weight",
    )(w_t, w_t, w_t)


class _SidePlan:
    def __init__(self, sides, n_outer, n_inner):
        steps = n_outer * n_inner
        self.arrays, self.in_specs, self.out_specs, self.out_shapes, self.shapes, blocks = [], [], [], [], [], []
        for side, layer in sides:
            side3 = side.reshape(side.shape[0], -1, side.shape[-1])
            r, c = side3.shape[1:]
            rb = next(cand for cand in range(16, r + 1, 16) if r % cand == 0 and r // cand <= steps)
            nblk = r // rb
            idx = functools.partial(lambda i, j, last: jnp.minimum(i * n_inner + j, last), last=nblk - 1)
            self.arrays.append(side3)
            self.in_specs.append(pl.BlockSpec(
                (None, rb, c), functools.partial(lambda i, j, idx, layer: (layer, idx(i, j), 0), idx=idx, layer=layer)))
            self.out_specs.append(pl.BlockSpec((rb, c), functools.partial(lambda i, j, idx: (idx(i, j), 0), idx=idx)))
            self.out_shapes.append(jax.ShapeDtypeStruct((r, c), BF16))
            self.shapes.append(side.shape[1:])
            blocks.append(nblk)
        self.blocks = tuple(blocks)

    def finish(self, outs):
        return [o.reshape(s) for o, s in zip(outs, self.shapes)]


def _side_casts(blocks, side_refs, side_out_refs):
    step = pl.program_id(0) * pl.num_programs(1) + pl.program_id(1)
    for n_blocks, src, dst in zip(blocks, side_refs, side_out_refs):
        @pl.when(step < n_blocks)
        def _():
            dst[...] = src[...].astype(BF16)


def _norm_inproj_kernel(x_ref, g_ref, w_ref, wa_ref, *rest, rows, side_blocks):
    k = len(side_blocks)
    z_ref, a1_ref = rest[k:k + 2]
    h_ref = rest[2 * k + 2]
    _side_casts(side_blocks, rest[:k], rest[k + 2:2 * k + 2])

    @pl.when(pl.program_id(1) == 0)
    def _():
        def body(c, carry):
            r = pl.multiple_of(c * rows, rows)
            xf = x_ref[pl.ds(r, rows), :]
            ms = jnp.mean(xf * xf, axis=-1, keepdims=True)
            h_ref[pl.ds(r, rows), :] = (xf * lax.rsqrt(ms + EPS) * g_ref[...]).astype(BF16)
            return carry
        lax.fori_loop(0, x_ref.shape[0] // rows, body, 0)
        a1_ref[...] = _dot_nt(h_ref[...], wa_ref[...])

    z_ref[...] = _dot_nt(h_ref[...], w_ref[...]).astype(BF16)


def _norm_inproj(x, g, w_t, wa_t, sides):
    m, d = x.shape
    n = w_t.shape[0]
    bm, bn = min(ROW_TILE, m), _pick(n, COL_TILE)
    assert m % bm == 0 and n % bn == 0
    ni, nj = m // bm, n // bn
    plan = _SidePlan(sides, ni, nj)
    z, a1, *side_out = pl.pallas_call(
        functools.partial(_norm_inproj_kernel, rows=128, side_blocks=plan.blocks),
        grid=(ni, nj),
        in_specs=[
            pl.BlockSpec((bm, d), lambda i, j: (i, 0)),
            pl.BlockSpec((1, d), lambda i, j: (0, 0)),
            pl.BlockSpec((bn, d), lambda i, j: (j, 0)),
            pl.BlockSpec((LANES, d), lambda i, j: (0, 0)),
        ] + plan.in_specs,
        out_specs=[
            pl.BlockSpec((bm, bn), lambda i, j: (i, j)),
            pl.BlockSpec((bm, LANES), lambda i, j: (i, 0)),
        ] + plan.out_specs,
        out_shape=[jax.ShapeDtypeStruct((m, n), BF16), jax.ShapeDtypeStruct((m, LANES), F32)] + plan.out_shapes,
        scratch_shapes=[pltpu.VMEM((bm, d), BF16)],
        compiler_params=_cparams("arbitrary", "arbitrary"),
        name="norm_inproj",
    )(x, g, w_t, wa_t, *plan.arrays)
    return z, a1, plan.finish(side_out)


def _log_sigmoid(x):
    return jnp.minimum(x, 0.0) - jnp.log(1.0 + jnp.exp(-jnp.abs(x)))


def _gla_kernel(q_ref, k_ref, v_ref, r_ref, a1_ref, wal_ref, bal_ref, gn_ref, tri_ref, o_ref, st_ref, *,
                chunk, heads):
    @pl.when(pl.program_id(2) == 0)
    def _():
        st_ref[...] = jnp.zeros_like(st_ref)

    hk = q_ref.shape[1] // heads
    hv = v_ref.shape[1] // heads
    a1 = _split_bf16(a1_ref[...])
    for h in range(heads):
        ks, vs = slice(h * hk, (h + 1) * hk), slice(h * hv, (h + 1) * hv)
        o, st_new = _gla_head(q_ref[:, ks], k_ref[:, ks], v_ref[:, vs], r_ref[:, vs], a1, wal_ref[:, ks],
                              bal_ref[:, ks], gn_ref[...], tri_ref[...], st_ref[h], chunk)
        o_ref[:, vs] = o
        st_ref[h] = st_new


def _gla_head(q, k, v, r, a1, wal, bal, gn, tri, st, chunk):
    t, dk = q.shape
    nc = t // chunk
    pair = 2 * chunk
    shift = chunk.bit_length() - 1
    nt_dot = lambda a, bb: lax.dot_general(a, bb, (((1,), (1,)), ((), ())), preferred_element_type=F32)
    rows = lambda x, c0, c1: x[c0 * chunk:c1 * chunk, :]

    a_hi, a_lo = a1
    w_hi, w_lo = _split_bf16(wal)
    xg = _dot(a_hi, w_hi) + _dot(a_hi, w_lo) + _dot(a_lo, w_hi) + bal
    la = _log_sigmoid(xg) * (1.0 / GLA_GATE_TEMP)

    la_hi, la_lo = _split_bf16(la)
    b = _dot(tri, la_hi) + _dot(tri, la_lo)

    bl = [b[(c + 1) * chunk - 1:(c + 1) * chunk, :] for c in range(nc)]
    pre = [jnp.zeros_like(bl[0])]
    for c in range(nc):
        pre.append(pre[c] + bl[c])

    q = q.astype(F32)
    k = k.astype(F32)
    q_t = q * jnp.exp(b) * (dk ** -0.5)
    k_t = (k * jnp.exp(-b)).astype(BF16)
    qc = [rows(q_t, c, c + 1) for c in range(nc)]
    kec = [rows(k, c, c + 1) * jnp.exp(bl[c] - rows(b, c, c + 1)) for c in range(nc)]
    q_tb = q_t.astype(BF16)
    k_eb = jnp.concatenate(kec, axis=0).astype(BF16)

    def q_from(mid, hi):
        return jnp.concatenate([qc[c] if c == mid else qc[c] * jnp.exp(pre[c] - pre[mid])
                                for c in range(mid, hi)], axis=0).astype(BF16)

    def k_upto(lo, mid):
        return jnp.concatenate([kec[m] if m == mid - 1 else kec[m] * jnp.exp(pre[mid] - pre[m + 1])
                                for m in range(lo, mid)], axis=0).astype(BF16)

    nb = t // pair
    blocks = [[None] * nb for _ in range(nb)]
    r2 = lax.broadcasted_iota(I32, (pair, pair), 0)
    c2 = lax.broadcasted_iota(I32, (pair, pair), 1)
    same = ((r2 >> shift) == (c2 >> shift)) & (r2 >= c2)
    cross = (r2 >= chunk) & (c2 < chunk)
    for p in range(nb):
        qp = rows(q_tb, 2 * p, 2 * p + 2)
        in_chunk = nt_dot(qp, rows(k_t, 2 * p, 2 * p + 2))
        next_chunk = nt_dot(qp, rows(k_eb, 2 * p, 2 * p + 2))
        blocks[p][p] = jnp.where(same, in_chunk, jnp.where(cross, next_chunk, 0.0))
    g = 4
    while g <= nc:
        half = g // 2
        for grp in range(nc // g):
            lo, mid, hi = grp * g, grp * g + half, (grp + 1) * g
            x = nt_dot(q_from(mid, hi), k_upto(lo, mid))
            hb = half // 2
            for i in range(hb):
                for j in range(hb):
                    blocks[mid // 2 + i][lo // 2 + j] = x[i * pair:(i + 1) * pair, j * pair:(j + 1) * pair]
        g *= 2
    zero_blk = jnp.zeros((pair, pair), F32)
    att = jnp.concatenate(
        [jnp.concatenate([zero_blk if blk is None else blk for blk in brow], axis=1) for brow in blocks], axis=0)

    o = _dot(att.astype(BF16), v) + nt_dot(q_from(0, nc), st.astype(BF16))
    upd = lax.dot_general(v, k_upto(0, nc), (((0,), (0,)), ((), ())), preferred_element_type=F32)
    st_new = st * jnp.exp(pre[nc]) + upd

    o = o * lax.rsqrt(jnp.mean(o * o, axis=-1, keepdims=True) + EPS) * gn
    r = r.astype(F32)
    return (o * (r * jax.nn.sigmoid(r))).astype(BF16), st_new


def _gla(z, a1, w_alpha, b_alpha, gla_norm, batch, seq, dk_total, dv_total):
    m = z.shape[0]
    hk, hv = dk_total // GLA_HEADS, dv_total // GLA_HEADS
    t = min(SEQ_TILE, seq)
    assert seq % t == 0 and t % GLA_CHUNK == 0 and hk % LANES == 0 and hv % LANES == 0
    ns = seq // t
    nc = t // GLA_CHUNK
    assert nc >= 2 and nc & (nc - 1) == 0 and GLA_HEADS % GLA_HEADS_PER_STEP == 0
    hps = GLA_HEADS_PER_STEP
    gk, gv = hps * hk, hps * hv
    k_off = dk_total // gk
    v_off = (2 * dk_total) // gv
    r_off = (2 * dk_total + dv_total) // gv
    idx = jnp.arange(t, dtype=I32)
    tri = ((idx[:, None] // GLA_CHUNK == idx[None, :] // GLA_CHUNK) & (idx[:, None] >= idx[None, :])).astype(BF16)
    rowi = lambda b, h, s: b * ns + s
    return pl.pallas_call(
        functools.partial(_gla_kernel, chunk=GLA_CHUNK, heads=hps),
        grid=(batch, GLA_HEADS // hps, ns),
        in_specs=[
            pl.BlockSpec((t, gk), lambda b, h, s: (rowi(b, h, s), h)),
            pl.BlockSpec((t, gk), lambda b, h, s: (rowi(b, h, s), k_off + h)),
            pl.BlockSpec((t, gv), lambda b, h, s: (rowi(b, h, s), v_off + h)),
            pl.BlockSpec((t, gv), lambda b, h, s: (rowi(b, h, s), r_off + h)),
            pl.BlockSpec((t, LANES), lambda b, h, s: (rowi(b, h, s), 0)),
            pl.BlockSpec((LANES, gk), lambda b, h, s: (0, h)),
            pl.BlockSpec((1, gk), lambda b, h, s: (0, h)),
            pl.BlockSpec((1, hv), lambda b, h, s: (0, 0)),
            pl.BlockSpec((t, t), lambda b, h, s: (0, 0)),
        ],
        out_specs=pl.BlockSpec((t, gv), lambda b, h, s: (rowi(b, h, s), h)),
        out_shape=jax.ShapeDtypeStruct((m, dv_total), BF16),
        scratch_shapes=[pltpu.VMEM((hps, hv, hk), F32)],
        compiler_params=_cparams("parallel", "parallel", "arbitrary"),
        name="gla",
    )(z, z, z, z, a1, w_alpha, b_alpha, gla_norm, tri)


def _pool_kernel(p_ref, ph_ref, wp_ref, sc_ref, y_ref, *, windows):
    s = pl.program_id(1)
    t = p_ref.shape[0]
    halo = ph_ref.shape[0]
    gw = wp_ref.shape[1]
    dist = lax.broadcasted_iota(I32, (t, t), 0) - lax.broadcasted_iota(I32, (t, t), 1)
    dist_h = lax.broadcasted_iota(I32, (t, halo), 0) + halo - lax.broadcasted_iota(I32, (t, halo), 1)
    pos = s * t + lax.broadcasted_iota(I32, (t, 1), 0)
    for g, w in enumerate(windows):
        cs = slice(g * gw, (g + 1) * gw)
        pc = p_ref[:, cs]
        band = jnp.where((dist >= 0) & (dist < w), 1.0, 0.0).astype(BF16)
        band_h = jnp.where(dist_h < w, 1.0, 0.0).astype(BF16)
        tot = _dot(band, pc) + jnp.where(s > 0, _dot(band_h, ph_ref[:, cs]), 0.0)
        cnt = jnp.minimum(pos + 1, w).astype(F32)
        mixed = tot / cnt - pc.astype(F32)
        y = _dot(mixed.astype(BF16), wp_ref[g]) * sc_ref[:, cs]
        y_ref[:, cs] = y.astype(BF16)


def _pool(z, w_pool, pool_scale, batch, seq, p_col0):
    m = z.shape[0]
    groups, gw, _ = w_pool.shape
    pw = groups * gw
    t = min(SEQ_TILE, seq)
    assert seq % t == 0 and t % HALO == 0 and p_col0 % pw == 0 and gw % LANES == 0
    ns = seq // t
    pc = p_col0 // pw
    hb = t // HALO
    return pl.pallas_call(
        functools.partial(_pool_kernel, windows=POOL_WINDOWS),
        grid=(batch, ns),
        in_specs=[
            pl.BlockSpec((t, pw), lambda b, s: (b * ns + s, pc)),
            pl.BlockSpec((HALO, pw), lambda b, s: (jnp.maximum((b * ns + s) * hb - 1, 0), pc)),
            pl.BlockSpec((groups, gw, gw), lambda b, s: (0, 0, 0)),
            pl.BlockSpec((1, pw), lambda b, s: (0, 0)),
        ],
        out_specs=pl.BlockSpec((t, pw), lambda b, s: (b * ns + s, 0)),
        out_shape=jax.ShapeDtypeStruct((m, pw), BF16),
        compiler_params=_cparams("parallel", "arbitrary"),
        name="pool",
    )(z, z, w_pool, pool_scale)


def _merge_kernel(og_ref, yp_ref, ga_ref, gb_ref, wa_ref, wb_ref, u_ref):
    ya = _dot(og_ref[...], wa_ref[...])
    yb = _dot(yp_ref[...], wb_ref[...])
    ga = jax.nn.sigmoid(ga_ref[...].astype(F32))
    gb = jax.nn.sigmoid(gb_ref[...].astype(F32))
    u_ref[...] = (ga * ya + gb * yb).astype(BF16)


def _merge(og, yp, z, w_a, w_b, ga_col0, gb_col0):
    m, dv = og.shape
    pw = yp.shape[1]
    d = w_a.shape[1]
    bm, bn = min(ROW_TILE, m), _pick(math.gcd(d, ga_col0, gb_col0), COL_TILE)
    assert m % bm == 0 and d % bn == 0 and ga_col0 % bn == 0 and gb_col0 % bn == 0
    ga0, gb0 = ga_col0 // bn, gb_col0 // bn
    return pl.pallas_call(
        _merge_kernel,
        grid=(m // bm, d // bn),
        in_specs=[
            pl.BlockSpec((bm, dv), lambda i, j: (i, 0)),
            pl.BlockSpec((bm, pw), lambda i, j: (i, 0)),
            pl.BlockSpec((bm, bn), lambda i, j: (i, ga0 + j)),
            pl.BlockSpec((bm, bn), lambda i, j: (i, gb0 + j)),
            pl.BlockSpec((dv, bn), lambda i, j: (0, j)),
            pl.BlockSpec((pw, bn), lambda i, j: (0, j)),
        ],
        out_specs=pl.BlockSpec((bm, bn), lambda i, j: (i, j)),
        out_shape=jax.ShapeDtypeStruct((m, d), BF16),
        compiler_params=_cparams("parallel", "arbitrary"),
        name="merge",
    )(og, yp, z, z, w_a, w_b)


def _out_proj_kernel(u_ref, w_ref, x_ref, o_ref):
    o_ref[...] = x_ref[...] + _dot(u_ref[...], w_ref[...])


def _out_proj(u, w, x):
    m, d = x.shape
    bm, bn = min(ROW_TILE, m), _pick(d, COL_TILE)
    assert m % bm == 0 and d % bn == 0
    return pl.pallas_call(
        _out_proj_kernel,
        grid=(m // bm, d // bn),
        in_specs=[
            pl.BlockSpec((bm, u.shape[1]), lambda i, j: (i, 0)),
            pl.BlockSpec((u.shape[1], bn), lambda i, j: (0, j)),
            pl.BlockSpec((bm, bn), lambda i, j: (i, j)),
        ],
        out_specs=pl.BlockSpec((bm, bn), lambda i, j: (i, j)),
        out_shape=jax.ShapeDtypeStruct((m, d), F32),
        compiler_params=_cparams("parallel", "arbitrary"),
        name="out_proj",
    )(u, w, x)


def _rmsnorm_rows(x_ref, g_ref, h_ref, rows):
    def body(c, carry):
        r = pl.multiple_of(c * rows, rows)
        xf = x_ref[pl.ds(r, rows), :]
        ms = jnp.mean(xf * xf, axis=-1, keepdims=True)
        h_ref[pl.ds(r, rows), :] = (xf * lax.rsqrt(ms + EPS) * g_ref[...]).astype(BF16)
        return carry
    lax.fori_loop(0, x_ref.shape[0] // rows, body, 0)


def _swiglu_block(h, wg, wu, wd):
    g = _dot(h, wg)
    u = _dot(h, wu)
    a = (g * jax.nn.sigmoid(g) * u).astype(BF16)
    return _dot(a, wd)


def _ffn_kernel(x_ref, g_ref, wg_ref, wu_ref, wd_ref, *rest, side_blocks):
    f = pl.program_id(1)
    k = len(side_blocks)
    o_ref = rest[k]
    h_ref = rest[2 * k + 1]
    _side_casts(side_blocks, rest[:k], rest[k + 1:2 * k + 1])

    @pl.when(f == 0)
    def _():
        _rmsnorm_rows(x_ref, g_ref, h_ref, 128)
        o_ref[...] = x_ref[...]

    o_ref[...] += _swiglu_block(h_ref[...], wg_ref[...], wu_ref[...], wd_ref[...])


def _ffn(x, g, wg, wu, wd, sides):
    m, d = x.shape
    ff = wg.shape[1]
    bm, bf = min(DENSE_FFN_ROW_TILE, m), min(FFN_COL_TILE, ff)
    assert m % bm == 0 and ff % bf == 0
    ni, nf = m // bm, ff // bf
    plan = _SidePlan(sides, ni, nf)
    out, *side_out = pl.pallas_call(
        functools.partial(_ffn_kernel, side_blocks=plan.blocks),
        grid=(ni, nf),
        in_specs=[
            pl.BlockSpec((bm, d), lambda i, f: (i, 0)),
            pl.BlockSpec((1, d), lambda i, f: (0, 0)),
            pl.BlockSpec((d, bf), lambda i, f: (0, f)),
            pl.BlockSpec((d, bf), lambda i, f: (0, f)),
            pl.BlockSpec((bf, d), lambda i, f: (f, 0)),
        ] + plan.in_specs,
        out_specs=[pl.BlockSpec((bm, d), lambda i, f: (i, 0))] + plan.out_specs,
        out_shape=[jax.ShapeDtypeStruct((m, d), F32)] + plan.out_shapes,
        scratch_shapes=[pltpu.VMEM((bm, d), BF16)],
        compiler_params=_cparams("arbitrary", "arbitrary"),
        name="ffn",
    )(x, g, wg, wu, wd, *plan.arrays)
    return out, plan.finish(side_out)


def _to_slabs(slab_ref, x):
    rows, d = x.shape
    sl = d // LANES
    for s in range(sl):
        slab_ref[pl.ds(s, rows, stride=sl), :] = x[:, s * LANES:(s + 1) * LANES]


def _from_slabs(slab_ref, rows, d):
    sl = d // LANES
    return jnp.concatenate([slab_ref[pl.ds(s, rows, stride=sl), :] for s in range(sl)], axis=1)


def _router_kernel(x_ref, g_ref, wr_ref, hs_ref, route_ref, hb_ref):
    _rmsnorm_rows(x_ref, g_ref, hb_ref, 128)
    hb = hb_ref[...]
    _to_slabs(hs_ref, hb.astype(F32))
    logits = _dot(hb, wr_ref[...])
    lane = lax.broadcasted_iota(I32, logits.shape, 1)
    lane_f = lane.astype(F32)
    neg = jnp.float32(-jnp.inf)
    l1 = jnp.where(lane < N_EXPERTS, logits, neg)
    m1 = jnp.max(l1, axis=-1, keepdims=True)
    i1 = jnp.min(jnp.where(l1 == m1, lane_f, float(LANES)), axis=-1, keepdims=True)
    l2 = jnp.where(lane_f == i1, neg, l1)
    m2 = jnp.max(l2, axis=-1, keepdims=True)
    i2 = jnp.min(jnp.where(l2 == m2, lane_f, float(LANES)), axis=-1, keepdims=True)
    e = jnp.exp(m2 - m1)
    den = 1.0 + e
    route = jnp.where(lane == 0, i1, 0.0)
    route = jnp.where(lane == 1, i2, route)
    route = jnp.where(lane == 2, 1.0 / den, route)
    route = jnp.where(lane == 3, e / den, route)
    route_ref[...] = route


def _router(x, g, wr):
    m, d = x.shape
    bm = min(FFN_ROW_TILE, m)
    sl = d // LANES
    assert m % bm == 0 and sl % 8 == 0
    return pl.pallas_call(
        _router_kernel,
        grid=(m // bm,),
        in_specs=[
            pl.BlockSpec((bm, d), lambda i: (i, 0)),
            pl.BlockSpec((1, d), lambda i: (0, 0)),
            pl.BlockSpec((d, LANES), lambda i: (0, 0)),
        ],
        out_specs=[
            pl.BlockSpec((bm * sl, LANES), lambda i: (i, 0)),
            pl.BlockSpec((bm, LANES), lambda i: (i, 0)),
        ],
        out_shape=[jax.ShapeDtypeStruct((m * sl, LANES), F32), jax.ShapeDtypeStruct((m, LANES), F32)],
        scratch_shapes=[pltpu.VMEM((bm, d), BF16)],
        compiler_params=_cparams("parallel"),
        name="router",
    )(x, g, wr)


def _route_plan_kernel(route_ref, pos_ref, meta_ref, rank_ref, *, tile, n_tiles, blk):
    m = route_ref.shape[0]
    nb = m // blk
    lane_b = lax.broadcasted_iota(I32, (blk, LANES), 1)
    lane_r = lax.broadcasted_iota(I32, (1, LANES), 1)
    ne = N_EXPERTS

    def onehot(r):
        rt = route_ref[pl.ds(r, blk), :]
        i1 = rt[:, 0:1].astype(I32)
        i2 = rt[:, 1:2].astype(I32)
        return jnp.where((lane_b == i1) | (lane_b == i2 + ne), 1.0, 0.0)

    strict = jnp.where(lax.broadcasted_iota(I32, (blk, blk), 0) > lax.broadcasted_iota(I32, (blk, blk), 1),
                       1.0, 0.0).astype(BF16)

    def rank_body(c, carry):
        r = pl.multiple_of(c * blk, blk)
        oh = onehot(r)
        rank_ref[pl.ds(r, blk), :] = _dot(strict, oh.astype(BF16)) + carry
        return carry + jnp.sum(oh, axis=0, keepdims=True)

    cnt12 = lax.fori_loop(0, nb, rank_body, jnp.zeros((1, LANES), F32))

    def lane_val(row, l):
        return jnp.sum(jnp.where(lane_r == l, row, 0.0), axis=-1, keepdims=True)

    cnt = jnp.zeros((1, LANES), F32)
    for e in range(ne):
        cnt = jnp.where(lane_r == e, lane_val(cnt12, e) + lane_val(cnt12, e + ne), cnt)
    tiles = jnp.floor((cnt + (tile - 1)) * (1.0 / tile))
    start = jnp.zeros((1, LANES), F32)
    for e in range(1, ne):
        start = start + jnp.where(lane_r >= e, lane_val(tiles, e - 1), 0.0)
    start = jnp.where(lane_r < ne, start * tile, 0.0)
    end = start + tiles * tile
    base = start
    for e in range(ne):
        base = jnp.where(lane_r == e + ne, lane_val(start, e) + lane_val(cnt12, e), base)

    def pos_body(c, carry):
        r = pl.multiple_of(c * blk, blk)
        slot = onehot(r) * (rank_ref[pl.ds(r, blk), :] + base)
        p1 = jnp.sum(jnp.where(lane_b < ne, slot, 0.0), axis=-1, keepdims=True)
        p2 = jnp.sum(jnp.where(lane_b >= ne, slot, 0.0), axis=-1, keepdims=True)
        both = jnp.where(lane_b == 0, p1, jnp.where(lane_b == 1, p2, 0.0))
        pos_ref[:, pl.ds(r, blk)] = jnp.transpose(both)[0:8, :].astype(I32)
        return carry

    lax.fori_loop(0, nb, pos_body, 0)

    n_used = lane_val(end, ne - 1) * (1.0 / tile)
    tile_start = lane_r.astype(F32) * tile
    tile_e = jnp.zeros((1, LANES), F32)
    last_e = jnp.zeros((1, 1), F32)
    for e in range(ne):
        tile_e = tile_e + jnp.where(lane_val(end, e) <= tile_start, 1.0, 0.0)
        last_e = jnp.where(lane_val(cnt, e) > 0, float(e), last_e)
    tile_e = jnp.where(lane_r.astype(F32) < n_used, tile_e, last_e)
    info = jnp.where(lane_r < ne, end, jnp.where(lane_r == 2 * ne, n_used, 0.0))
    for e in range(ne):
        info = jnp.where(lane_r == e + ne, lane_val(cnt, e), info)
    row8 = lax.broadcasted_iota(I32, (8, LANES), 0)
    meta = jnp.where(row8 == 0, tile_e, jnp.where(row8 == 1, info, jnp.where(row8 == 2, n_used, 0.0)))
    meta_ref[...] = meta.astype(I32)


def _route_plan(route, tile, n_tiles):
    m = route.shape[0]
    blk = min(PLAN_TILE, m)
    assert m % blk == 0 and n_tiles <= LANES
    return pl.pallas_call(
        functools.partial(_route_plan_kernel, tile=tile, n_tiles=n_tiles, blk=blk),
        out_shape=[jax.ShapeDtypeStruct((8, m), I32), jax.ShapeDtypeStruct((8, LANES), I32)],
        scratch_shapes=[pltpu.VMEM((m, LANES), F32)],
        compiler_params=pltpu.CompilerParams(vmem_limit_bytes=VMEM_LIMIT_BYTES),
        name="route_plan",
    )(route)


def _dispatch_kernel(meta_ref, pos_ref, h_ref, xs_ref, zero_ref, sem, *, tile, sl):
    i = pl.program_id(0)
    tt = h_ref.shape[0] // sl
    n_tiles = xs_ref.shape[0] // (tile * sl)

    def zero_copy(e):
        first = pl.multiple_of((meta_ref[e] - tile) * sl, tile * sl)
        return pltpu.make_async_copy(zero_ref, xs_ref.at[pl.ds(first, tile * sl), :], sem.at[1])

    def tail_copy(j):
        return pltpu.make_async_copy(zero_ref, xs_ref.at[pl.ds(j * tile * sl, tile * sl), :], sem.at[1])

    @pl.when(i == 0)
    def _():
        zero_ref[...] = jnp.zeros_like(zero_ref)
        n_used = meta_ref[2 * N_EXPERTS]
        for e in range(N_EXPERTS):
            @pl.when(meta_ref[N_EXPERTS + e] > 0)
            def _():
                zero_copy(e).start()
        for j in range(n_tiles - N_EXPERTS, n_tiles):
            @pl.when(j >= n_used)
            def _():
                tail_copy(j).start()
        for e in range(N_EXPERTS):
            @pl.when(meta_ref[N_EXPERTS + e] > 0)
            def _():
                zero_copy(e).wait()
        for j in range(n_tiles - N_EXPERTS, n_tiles):
            @pl.when(j >= n_used)
            def _():
                tail_copy(j).wait()

    def row_copy(r, c):
        src = pl.multiple_of(r * sl, sl)
        dst = pl.multiple_of(pos_ref[0, c * tt + r] * sl, sl)
        return pltpu.make_async_copy(h_ref.at[pl.ds(src, sl), :], xs_ref.at[pl.ds(dst, sl), :], sem.at[0])

    def start_body(r, carry):
        row_copy(r, 0).start(priority=0)
        row_copy(r, 1).start(priority=1)
        return carry

    lax.fori_loop(0, tt, start_body, 0, unroll=8)
    for _ in range(2):
        pltpu.make_async_copy(h_ref, xs_ref.at[pl.ds(0, tt * sl), :], sem.at[0]).wait()


def _dispatch(hs, pos_tiles, ginfo, n_slots, tile, sl):
    m = hs.shape[0] // sl
    tt = pos_tiles.shape[2] // 2
    return pl.pallas_call(
        functools.partial(_dispatch_kernel, tile=tile, sl=sl),
        grid_spec=pltpu.PrefetchScalarGridSpec(
            num_scalar_prefetch=1,
            grid=(m // tt,),
            in_specs=[
                pl.BlockSpec((None, 1, 2 * tt), lambda i, meta: (i, 0, 0), memory_space=pltpu.SMEM),
                pl.BlockSpec((tt * sl, LANES), lambda i, meta: (i, 0)),
            ],
            out_specs=pl.BlockSpec(memory_space=pl.ANY),
            scratch_shapes=[pltpu.VMEM((tile * sl, LANES), F32), pltpu.SemaphoreType.DMA((2,))],
        ),
        out_shape=jax.ShapeDtypeStruct((n_slots * sl, LANES), F32),
        compiler_params=_cparams("arbitrary"),
        name="dispatch",
    )(ginfo, pos_tiles, hs)


def _expert_ffn_kernel(te_ref, nu_ref, xs_ref, wg_ref, wu_ref, wd_ref, y_ref, h_ref, acc_ref):
    i, f = pl.program_id(0), pl.program_id(1)
    used = i < nu_ref[0]
    last = pl.num_programs(1) - 1
    tile, d = h_ref.shape

    @pl.when(used & (f == 0))
    def _():
        h_ref[...] = _from_slabs(xs_ref, tile, d).astype(BF16)
        acc_ref[...] = jnp.zeros_like(acc_ref)

    @pl.when(used)
    def _():
        acc_ref[...] += _swiglu_block(h_ref[...], wg_ref[...], wu_ref[...], wd_ref[...])

    @pl.when(used & (f == last))
    def _():
        _to_slabs(y_ref, acc_ref[...])

    @pl.when(jnp.logical_not(used) & (f == last))
    def _():
        y_ref[...] = jnp.zeros_like(y_ref)


def _expert_ffn(xs, tile_e, n_used, wg, wu, wd, tile, sl):
    n_slots = xs.shape[0] // sl
    d = sl * LANES
    ff = wg.shape[2]
    bf = min(FFN_COL_TILE, ff)
    assert n_slots % tile == 0 and ff % bf == 0
    nf = ff // bf
    row = lambda i, f, te, nu: (jnp.minimum(i, nu[0] - 1), 0)
    fblk = lambda i, f, nu: jnp.where(i < nu[0], f, nf - 1)
    return pl.pallas_call(
        _expert_ffn_kernel,
        grid_spec=pltpu.PrefetchScalarGridSpec(
            num_scalar_prefetch=2,
            grid=(n_slots // tile, nf),
            in_specs=[
                pl.BlockSpec((tile * sl, LANES), row),
                pl.BlockSpec((None, d, bf), lambda i, f, te, nu: (te[i], 0, fblk(i, f, nu))),
                pl.BlockSpec((None, d, bf), lambda i, f, te, nu: (te[i], 0, fblk(i, f, nu))),
                pl.BlockSpec((None, bf, d), lambda i, f, te, nu: (te[i], fblk(i, f, nu), 0)),
            ],
            out_specs=pl.BlockSpec((tile * sl, LANES), lambda i, f, te, nu: (i, 0)),
            scratch_shapes=[pltpu.VMEM((tile, d), BF16), pltpu.VMEM((tile, d), F32)],
        ),
        out_shape=jax.ShapeDtypeStruct((n_slots * sl, LANES), F32),
        compiler_params=_cparams("parallel", "arbitrary"),
        name="expert_ffn",
    )(tile_e, n_used, xs, wg, wu, wd)


def _combine_kernel(pos_ref, nxt_ref, x_ref, route_ref, g_ref, y_ref, o_ref, buf_ref, sem):
    i = pl.program_id(0)
    n = pl.num_programs(0)
    tt, d = x_ref.shape
    sl = d // LANES
    slot = lax.rem(i, 2)

    def gather(p_ref, dst_slot):
        def row_copy(r, c):
            src = pl.multiple_of(p_ref[0, c * tt + r] * sl, sl)
            dst = pl.multiple_of(r * sl, sl)
            return pltpu.make_async_copy(y_ref.at[pl.ds(src, sl), :], buf_ref.at[dst_slot, c, pl.ds(dst, sl), :],
                                         sem.at[dst_slot])

        def start_body(r, carry):
            row_copy(r, 0).start()
            row_copy(r, 1).start()
            return carry

        lax.fori_loop(0, tt, start_body, 0, unroll=8)

    @pl.when(i == 0)
    def _():
        gather(pos_ref, 0)

    @pl.when(i + 1 < n)
    def _():
        gather(nxt_ref, 1 - slot)

    for c in range(2):
        pltpu.make_async_copy(y_ref.at[pl.ds(0, tt * sl), :], buf_ref.at[slot, c], sem.at[slot]).wait()

    rt = route_ref[...]
    moe = (rt[:, 2:3] * _from_slabs(buf_ref.at[slot, 0], tt, d)
           + rt[:, 3:4] * _from_slabs(buf_ref.at[slot, 1], tt, d))
    xo = x_ref[...] + moe
    ms = jnp.mean(xo * xo, axis=-1, keepdims=True)
    o_ref[...] = xo * lax.rsqrt(ms + EPS) * g_ref[...]


def _combine(x, route, g, y, pos_tiles):
    m, d = x.shape
    tt = pos_tiles.shape[2] // 2
    nt = m // tt
    return pl.pallas_call(
        _combine_kernel,
        grid=(nt,),
        in_specs=[
            pl.BlockSpec((None, 1, 2 * tt), lambda i: (i, 0, 0), memory_space=pltpu.SMEM),
            pl.BlockSpec((None, 1, 2 * tt), lambda i: (jnp.minimum(i + 1, nt - 1), 0, 0), memory_space=pltpu.SMEM),
            pl.BlockSpec((tt, d), lambda i: (i, 0)),
            pl.BlockSpec((tt, LANES), lambda i: (i, 0)),
            pl.BlockSpec((1, d), lambda i: (0, 0)),
            pl.BlockSpec(memory_space=pl.ANY),
        ],
        out_specs=pl.BlockSpec((tt, d), lambda i: (i, 0)),
        out_shape=jax.ShapeDtypeStruct((m, d), F32),
        scratch_shapes=[pltpu.VMEM((2, 2, tt * (d // LANES), LANES), F32), pltpu.SemaphoreType.DMA((2,))],
        compiler_params=_cparams("arbitrary"),
        name="combine",
    )(pos_tiles, pos_tiles, x, route, g, y)


def _moe_ffn_final(x, g_ffn, wr, wg, wu, wd, g_final):
    m, d = x.shape
    tile = min(FFN_ROW_TILE, m)
    n_tiles = (2 * m) // tile + N_EXPERTS
    tt = min(TOK_TILE, m)
    sl = d // LANES
    hs, route = _router(x, g_ffn, wr)
    pos, meta = _route_plan(route, tile, n_tiles)
    pos_tiles = pos[:2].reshape(2, m // tt, tt).transpose(1, 0, 2).reshape(m // tt, 1, 2 * tt)
    xs = _dispatch(hs, pos_tiles, meta[1, :2 * N_EXPERTS + 1], n_tiles * tile, tile, sl)
    y = _expert_ffn(xs, meta[0, :n_tiles], meta[2, :1], wg, wu, wd, tile, sl)
    return _combine(x, route, g_final, y, pos_tiles)


def kernel(x, ln_mix, w_in, w_alpha, b_alpha, gla_norm, w_pool, pool_scale, w_branch_gla, w_branch_pool, w_out,
           ln_ffn, ffn_w_gate, ffn_w_up, ffn_w_down, router_w, exp_w_gate, exp_w_up, exp_w_down, ln_final):
    batch, seq, d = x.shape
    depth = w_in.shape[0]
    dk = w_alpha.shape[2]
    dv = w_branch_gla.shape[1]
    pw = w_branch_pool.shape[1]
    rank = w_alpha.shape[1]
    assert depth == 2 and rank <= LANES
    a_col = 2 * dk + 2 * dv
    m = batch * seq
    xf = x.reshape(m, d)
    row = lambda v: v.reshape(1, -1)
    w_in_t = jnp.swapaxes(w_in, 1, 2)

    assert depth == 2 and exp_w_gate.shape[0] == 1 and ffn_w_gate.shape[0] == 1
    mixer = lambda l: (_to_bf16(w_pool, l), _to_bf16(w_branch_gla, l), _to_bf16(w_branch_pool, l),
                       _to_bf16(w_out, l))
    for l in range(depth):
        w_main, w_a1 = _inproj_weight(w_in_t, l, a_col, rank)
        w_al = jnp.pad(w_alpha[l], ((0, LANES - rank), (0, 0)))
        if l == 0:
            z, a1, (e_gate, f_gate, f_up, f_down) = _norm_inproj(
                xf, row(ln_mix[l]), w_main, w_a1,
                [(exp_w_gate, 0), (ffn_w_gate, 0), (ffn_w_up, 0), (ffn_w_down, 0)])
            wp, wbg, wbp, wo = mixer(0)
        else:
            z, a1, (e_up,) = _norm_inproj(xf, row(ln_mix[l]), w_main, w_a1, [(exp_w_up, 0)])
        og = _gla(z, a1, w_al, row(b_alpha[l]), row(gla_norm[l]), batch, seq, dk, dv)
        yp = _pool(z, wp, row(pool_scale[l]), batch, seq, a_col)
        u = _merge(og, yp, z, wbg, wbp, a_col + pw, a_col + pw + d)
        xf = _out_proj(u, wo, xf)
        if l == 0:
            xf, (e_down, wp, wbg, wbp, wo) = _ffn(
                xf, row(ln_ffn[l]), f_gate, f_up, f_down,
                [(exp_w_down, 0), (w_pool, 1), (w_branch_gla, 1), (w_branch_pool, 1), (w_out, 1)])
        else:
            wr = jnp.pad(router_w[0], ((0, 0), (0, LANES - N_EXPERTS))).astype(BF16)
            xf = _moe_ffn_final(xf, row(ln_ffn[l]), wr, e_gate, e_up, e_down, ln_final.reshape(1, -1))
    return xf.reshape(batch, seq, d)
```

```python
import functools
import math

import jax
import jax.numpy as jnp
from jax import lax
from jax.experimental import pallas as pl
from jax.experimental.pallas import tpu as pltpu

F32 = jnp.float32
BF16 = jnp.bfloat16
I32 = jnp.int32

EPS = 1e-6
GLA_HEADS = 4
GLA_LOWRANK = 16
GLA_GATE_TEMP = 16.0
GLA_CHUNK = 64
GLA_HEADS_PER_STEP = 4
POOL_WINDOWS = (2, 4, 8, 16)
POOL_GROUPS = 4
N_EXPERTS = 8

LANES = 128
VMEM_LIMIT_BYTES = 56 * 1024 * 1024

ROW_TILE = 1024
COL_TILE = 1024
FFN_ROW_TILE = 512
DENSE_FFN_ROW_TILE = 512
FFN_COL_TILE = 512
SEQ_TILE = 512
TOK_TILE = 256
PLAN_TILE = 512
HALO = 16


def _cparams(*sem):
    return pltpu.CompilerParams(dimension_semantics=sem, vmem_limit_bytes=VMEM_LIMIT_BYTES)


def _dot(a, b):
    return jnp.dot(a, b, preferred_element_type=F32)


def _dot_nt(a, b):
    return lax.dot_general(a, b, (((1,), (1,)), ((), ())), preferred_element_type=F32)


def _split_bf16(x):
    hi = x.astype(BF16)
    lo = (x - hi.astype(F32)).astype(BF16)
    return hi, lo


def _pick(n, target, align=LANES):
    if n <= target:
        return n
    best = None
    for cand in range(align, target + 1, align):
        if n % cand == 0:
            best = cand
    assert best is not None, (n, target, align)
    return best


def _cast_kernel(x_ref, o_ref):
    o_ref[...] = x_ref[...].astype(BF16)


def _to_bf16(w, layer):
    shape = w.shape[1:]
    w3 = w.reshape(w.shape[0], -1, shape[-1])
    _, r, c = w3.shape
    br, bc = _pick(r, 1024, 16), _pick(c, 2048)
    out = pl.pallas_call(
        _cast_kernel,
        grid=(r // br, c // bc),
        in_specs=[pl.BlockSpec((None, br, bc), lambda i, j: (layer, i, j))],
        out_specs=pl.BlockSpec((br, bc), lambda i, j: (i, j)),
        out_shape=jax.ShapeDtypeStruct((r, c), BF16),
        compiler_params=_cparams("parallel", "parallel"),
        name="to_bf16",
    )(w3)
    return out.reshape(shape)


def _inproj_weight_kernel(w_ref, nxt_ref, lr_ref, wm_ref, wa_ref, *, first_shifted, rank):
    j = pl.program_id(0)

    @pl.when(j < first_shifted)
    def _():
        wm_ref[...] = w_ref[...].astype(BF16)

    @pl.when(j >= first_shifted)
    def _():
        wm_ref[...] = jnp.concatenate([w_ref[rank:, :], nxt_ref[...]], axis=0).astype(BF16)

    @pl.when(j == 0)
    def _():
        pad = jnp.zeros((wa_ref.shape[0] - rank, wa_ref.shape[1]), F32)
        wa_ref[...] = jnp.concatenate([lr_ref[...], pad], axis=0).astype(BF16)


def _inproj_weight(w_t, layer, a_col, rank):
    _, n_in, d = w_t.shape
    n = n_in - rank
    br = _pick(math.gcd(n, a_col), 512)
    assert a_col % br == 0 and rank % 8 == 0 and rank < LANES and br % rank == 0
    per = br // rank
    return pl.pallas_call(
        functools.partial(_inproj_weight_kernel, first_shifted=a_col // br, rank=rank),
        grid=(n // br,),
        in_specs=[
            pl.BlockSpec((None, br, d), lambda j: (layer, j, 0)),
            pl.BlockSpec((None, rank, d), lambda j: (layer, (j + 1) * per, 0)),
            pl.BlockSpec((None, rank, d), lambda j: (layer, a_col // rank, 0)),
        ],
        out_specs=[
            pl.BlockSpec((br, d), lambda j: (j, 0)),
            pl.BlockSpec((LANES, d), lambda j: (0, 0)),
        ],
        out_shape=[jax.ShapeDtypeStruct((n, d), BF16), jax.ShapeDtypeStruct((LANES, d), BF16)],
        compiler_params=_cparams("arbitrary"),
        name="inproj_weight",
    )(w_t, w_t, w_t)


class _SidePlan:
    def __init__(self, sides, n_outer, n_inner):
        steps = n_outer * n_inner
        self.arrays, self.in_specs, self.out_specs, self.out_shapes, self.shapes, blocks = [], [], [], [], [], []
        for side, layer in sides:
            side3 = side.reshape(side.shape[0], -1, side.shape[-1])
            r, c = side3.shape[1:]
            rb = next(cand for cand in range(16, r + 1, 16) if r % cand == 0 and r // cand <= steps)
            nblk = r // rb
            idx = functools.partial(lambda i, j, last: jnp.minimum(i * n_inner + j, last), last=nblk - 1)
            self.arrays.append(side3)
            self.in_specs.append(pl.BlockSpec(
                (None, rb, c), functools.partial(lambda i, j, idx, layer: (layer, idx(i, j), 0), idx=idx, layer=layer)))
            self.out_specs.append(pl.BlockSpec((rb, c), functools.partial(lambda i, j, idx: (idx(i, j), 0), idx=idx)))
            self.out_shapes.append(jax.ShapeDtypeStruct((r, c), BF16))
            self.shapes.append(side.shape[1:])
            blocks.append(nblk)
        self.blocks = tuple(blocks)

    def finish(self, outs):
        return [o.reshape(s) for o, s in zip(outs, self.shapes)]


def _side_casts(blocks, side_refs, side_out_refs):
    step = pl.program_id(0) * pl.num_programs(1) + pl.program_id(1)
    for n_blocks, src, dst in zip(blocks, side_refs, side_out_refs):
        @pl.when(step < n_blocks)
        def _():
            dst[...] = src[...].astype(BF16)


def _norm_inproj_kernel(x_ref, g_ref, w_ref, wa_ref, *rest, rows, side_blocks):
    k = len(side_blocks)
    z_ref, a1_ref = rest[k:k + 2]
    h_ref = rest[2 * k + 2]
    _side_casts(side_blocks, rest[:k], rest[k + 2:2 * k + 2])

    @pl.when(pl.program_id(1) == 0)
    def _():
        def body(c, carry):
            r = pl.multiple_of(c * rows, rows)
            xf = x_ref[pl.ds(r, rows), :]
            ms = jnp.mean(xf * xf, axis=-1, keepdims=True)
            h_ref[pl.ds(r, rows), :] = (xf * lax.rsqrt(ms + EPS) * g_ref[...]).astype(BF16)
            return carry
        lax.fori_loop(0, x_ref.shape[0] // rows, body, 0)
        a1_ref[...] = _dot_nt(h_ref[...], wa_ref[...])

    z_ref[...] = _dot_nt(h_ref[...], w_ref[...]).astype(BF16)


def _norm_inproj(x, g, w_t, wa_t, sides):
    m, d = x.shape
    n = w_t.shape[0]
    bm, bn = min(ROW_TILE, m), _pick(n, COL_TILE)
    assert m % bm == 0 and n % bn == 0
    ni, nj = m // bm, n // bn
    plan = _SidePlan(sides, ni, nj)
    z, a1, *side_out = pl.pallas_call(
        functools.partial(_norm_inproj_kernel, rows=128, side_blocks=plan.blocks),
        grid=(ni, nj),
        in_specs=[
            pl.BlockSpec((bm, d), lambda i, j: (i, 0)),
            pl.BlockSpec((1, d), lambda i, j: (0, 0)),
            pl.BlockSpec((bn, d), lambda i, j: (j, 0)),
            pl.BlockSpec((LANES, d), lambda i, j: (0, 0)),
        ] + plan.in_specs,
        out_specs=[
            pl.BlockSpec((bm, bn), lambda i, j: (i, j)),
            pl.BlockSpec((bm, LANES), lambda i, j: (i, 0)),
        ] + plan.out_specs,
        out_shape=[jax.ShapeDtypeStruct((m, n), BF16), jax.ShapeDtypeStruct((m, LANES), F32)] + plan.out_shapes,
        scratch_shapes=[pltpu.VMEM((bm, d), BF16)],
        compiler_params=_cparams("arbitrary", "arbitrary"),
        name="norm_inproj",
    )(x, g, w_t, wa_t, *plan.arrays)
    return z, a1, plan.finish(side_out)


def _log_sigmoid(x):
    return jnp.minimum(x, 0.0) - jnp.log(1.0 + jnp.exp(-jnp.abs(x)))


def _gla_kernel(q_ref, k_ref, v_ref, r_ref, a1_ref, wal_ref, bal_ref, gn_ref, tri_ref, o_ref, st_ref, *,
                chunk, heads):
    @pl.when(pl.program_id(2) == 0)
    def _():
        st_ref[...] = jnp.zeros_like(st_ref)

    hk = q_ref.shape[1] // heads
    hv = v_ref.shape[1] // heads
    a1 = _split_bf16(a1_ref[...])
    for h in range(heads):
        ks, vs = slice(h * hk, (h + 1) * hk), slice(h * hv, (h + 1) * hv)
        o, st_new = _gla_head(q_ref[:, ks], k_ref[:, ks], v_ref[:, vs], r_ref[:, vs], a1, wal_ref[:, ks],
                              bal_ref[:, ks], gn_ref[...], tri_ref[...], st_ref[h], chunk)
        o_ref[:, vs] = o
        st_ref[h] = st_new


def _gla_head(q, k, v, r, a1, wal, bal, gn, tri, st, chunk):
    t, dk = q.shape
    nc = t // chunk
    pair = 2 * chunk
    shift = chunk.bit_length() - 1
    nt_dot = lambda a, bb: lax.dot_general(a, bb, (((1,), (1,)), ((), ())), preferred_element_type=F32)
    rows = lambda x, c0, c1: x[c0 * chunk:c1 * chunk, :]

    a_hi, a_lo = a1
    w_hi, w_lo = _split_bf16(wal)
    xg = _dot(a_hi, w_hi) + _dot(a_hi, w_lo) + _dot(a_lo, w_hi) + bal
    la = _log_sigmoid(xg) * (1.0 / GLA_GATE_TEMP)

    la_hi, la_lo = _split_bf16(la)
    b = _dot(tri, la_hi) + _dot(tri, la_lo)

    bl = [b[(c + 1) * chunk - 1:(c + 1) * chunk, :] for c in range(nc)]
    pre = [jnp.zeros_like(bl[0])]
    for c in range(nc):
        pre.append(pre[c] + bl[c])

    q = q.astype(F32)
    k = k.astype(F32)
    q_t = q * jnp.exp(b) * (dk ** -0.5)
    k_t = (k * jnp.exp(-b)).astype(BF16)
    qc = [rows(q_t, c, c + 1) for c in range(nc)]
    kec = [rows(k, c, c + 1) * jnp.exp(bl[c] - rows(b, c, c + 1)) for c in range(nc)]
    q_tb = q_t.astype(BF16)
    k_eb = jnp.concatenate(kec, axis=0).astype(BF16)

    def q_from(mid, hi):
        return jnp.concatenate([qc[c] if c == mid else qc[c] * jnp.exp(pre[c] - pre[mid])
                                for c in range(mid, hi)], axis=0).astype(BF16)

    def k_upto(lo, mid):
        return jnp.concatenate([kec[m] if m == mid - 1 else kec[m] * jnp.exp(pre[mid] - pre[m + 1])
                                for m in range(lo, mid)], axis=0).astype(BF16)

    nb = t // pair
    blocks = [[None] * nb for _ in range(nb)]
    r2 = lax.broadcasted_iota(I32, (pair, pair), 0)
    c2 = lax.broadcasted_iota(I32, (pair, pair), 1)
    same = ((r2 >> shift) == (c2 >> shift)) & (r2 >= c2)
    cross = (r2 >= chunk) & (c2 < chunk)
    for p in range(nb):
        qp = rows(q_tb, 2 * p, 2 * p + 2)
        in_chunk = nt_dot(qp, rows(k_t, 2 * p, 2 * p + 2))
        next_chunk = nt_dot(qp, rows(k_eb, 2 * p, 2 * p + 2))
        blocks[p][p] = jnp.where(same, in_chunk, jnp.where(cross, next_chunk, 0.0))
    g = 4
    while g <= nc:
        half = g // 2
        for grp in range(nc // g):
            lo, mid, hi = grp * g, grp * g + half, (grp + 1) * g
            x = nt_dot(q_from(mid, hi), k_upto(lo, mid))
            hb = half // 2
            for i in range(hb):
                for j in range(hb):
                    blocks[mid // 2 + i][lo // 2 + j] = x[i * pair:(i + 1) * pair, j * pair:(j + 1) * pair]
        g *= 2
    zero_blk = jnp.zeros((pair, pair), F32)
    att = jnp.concatenate(
        [jnp.concatenate([zero_blk if blk is None else blk for blk in brow], axis=1) for brow in blocks], axis=0)

    o = _dot(att.astype(BF16), v) + nt_dot(q_from(0, nc), st.astype(BF16))
    upd = lax.dot_general(v, k_upto(0, nc), (((0,), (0,)), ((), ())), preferred_element_type=F32)
    st_new = st * jnp.exp(pre[nc]) + upd

    o = o * lax.rsqrt(jnp.mean(o * o, axis=-1, keepdims=True) + EPS) * gn
    r = r.astype(F32)
    return (o * (r * jax.nn.sigmoid(r))).astype(BF16), st_new


def _gla(z, a1, w_alpha, b_alpha, gla_norm, batch, seq, dk_total, dv_total):
    m = z.shape[0]
    hk, hv = dk_total // GLA_HEADS, dv_total // GLA_HEADS
    t = min(SEQ_TILE, seq)
    assert seq % t == 0 and t % GLA_CHUNK == 0 and hk % LANES == 0 and hv % LANES == 0
    ns = seq // t
    nc = t // GLA_CHUNK
    assert nc >= 2 and nc & (nc - 1) == 0 and GLA_HEADS % GLA_HEADS_PER_STEP == 0
    hps = GLA_HEADS_PER_STEP
    gk, gv = hps * hk, hps * hv
    k_off = dk_total // gk
    v_off = (2 * dk_total) // gv
    r_off = (2 * dk_total + dv_total) // gv
    idx = jnp.arange(t, dtype=I32)
    tri = ((idx[:, None] // GLA_CHUNK == idx[None, :] // GLA_CHUNK) & (idx[:, None] >= idx[None, :])).astype(BF16)
    rowi = lambda b, h, s: b * ns + s
    return pl.pallas_call(
        functools.partial(_gla_kernel, chunk=GLA_CHUNK, heads=hps),
        grid=(batch, GLA_HEADS // hps, ns),
        in_specs=[
            pl.BlockSpec((t, gk), lambda b, h, s: (rowi(b, h, s), h)),
            pl.BlockSpec((t, gk), lambda b, h, s: (rowi(b, h, s), k_off + h)),
            pl.BlockSpec((t, gv), lambda b, h, s: (rowi(b, h, s), v_off + h)),
            pl.BlockSpec((t, gv), lambda b, h, s: (rowi(b, h, s), r_off + h)),
            pl.BlockSpec((t, LANES), lambda b, h, s: (rowi(b, h, s), 0)),
            pl.BlockSpec((LANES, gk), lambda b, h, s: (0, h)),
            pl.BlockSpec((1, gk), lambda b, h, s: (0, h)),
            pl.BlockSpec((1, hv), lambda b, h, s: (0, 0)),
            pl.BlockSpec((t, t), lambda b, h, s: (0, 0)),
        ],
        out_specs=pl.BlockSpec((t, gv), lambda b, h, s: (rowi(b, h, s), h)),
        out_shape=jax.ShapeDtypeStruct((m, dv_total), BF16),
        scratch_shapes=[pltpu.VMEM((hps, hv, hk), F32)],
        compiler_params=_cparams("parallel", "parallel", "arbitrary"),
        name="gla",
    )(z, z, z, z, a1, w_alpha, b_alpha, gla_norm, tri)


def _pool_kernel(p_ref, ph_ref, wp_ref, sc_ref, y_ref, *, windows):
    s = pl.program_id(1)
    t = p_ref.shape[0]
    halo = ph_ref.shape[0]
    gw = wp_ref.shape[1]
    dist = lax.broadcasted_iota(I32, (t, t), 0) - lax.broadcasted_iota(I32, (t, t), 1)
    dist_h = lax.broadcasted_iota(I32, (t, halo), 0) + halo - lax.broadcasted_iota(I32, (t, halo), 1)
    pos = s * t + lax.broadcasted_iota(I32, (t, 1), 0)
    for g, w in enumerate(windows):
        cs = slice(g * gw, (g + 1) * gw)
        pc = p_ref[:, cs]
        band = jnp.where((dist >= 0) & (dist < w), 1.0, 0.0).astype(BF16)
        band_h = jnp.where(dist_h < w, 1.0, 0.0).astype(BF16)
        tot = _dot(band, pc) + jnp.where(s > 0, _dot(band_h, ph_ref[:, cs]), 0.0)
        cnt = jnp.minimum(pos + 1, w).astype(F32)
        mixed = tot / cnt - pc.astype(F32)
        y = _dot(mixed.astype(BF16), wp_ref[g]) * sc_ref[:, cs]
        y_ref[:, cs] = y.astype(BF16)


def _pool(z, w_pool, pool_scale, batch, seq, p_col0):
    m = z.shape[0]
    groups, gw, _ = w_pool.shape
    pw = groups * gw
    t = min(SEQ_TILE, seq)
    assert seq % t == 0 and t % HALO == 0 and p_col0 % pw == 0 and gw % LANES == 0
    ns = seq // t
    pc = p_col0 // pw
    hb = t // HALO
    return pl.pallas_call(
        functools.partial(_pool_kernel, windows=POOL_WINDOWS),
        grid=(batch, ns),
        in_specs=[
            pl.BlockSpec((t, pw), lambda b, s: (b * ns + s, pc)),
            pl.BlockSpec((HALO, pw), lambda b, s: (jnp.maximum((b * ns + s) * hb - 1, 0), pc)),
            pl.BlockSpec((groups, gw, gw), lambda b, s: (0, 0, 0)),
            pl.BlockSpec((1, pw), lambda b, s: (0, 0)),
        ],
        out_specs=pl.BlockSpec((t, pw), lambda b, s: (b * ns + s, 0)),
        out_shape=jax.ShapeDtypeStruct((m, pw), BF16),
        compiler_params=_cparams("parallel", "arbitrary"),
        name="pool",
    )(z, z, w_pool, pool_scale)


def _merge_kernel(og_ref, yp_ref, ga_ref, gb_ref, wa_ref, wb_ref, u_ref):
    ya = _dot(og_ref[...], wa_ref[...])
    yb = _dot(yp_ref[...], wb_ref[...])
    ga = jax.nn.sigmoid(ga_ref[...].astype(F32))
    gb = jax.nn.sigmoid(gb_ref[...].astype(F32))
    u_ref[...] = (ga * ya + gb * yb).astype(BF16)


def _merge(og, yp, z, w_a, w_b, ga_col0, gb_col0):
    m, dv = og.shape
    pw = yp.shape[1]
    d = w_a.shape[1]
    bm, bn = min(ROW_TILE, m), _pick(math.gcd(d, ga_col0, gb_col0), COL_TILE)
    assert m % bm == 0 and d % bn == 0 and ga_col0 % bn == 0 and gb_col0 % bn == 0
    ga0, gb0 = ga_col0 // bn, gb_col0 // bn
    return pl.pallas_call(
        _merge_kernel,
        grid=(m // bm, d // bn),
        in_specs=[
            pl.BlockSpec((bm, dv), lambda i, j: (i, 0)),
            pl.BlockSpec((bm, pw), lambda i, j: (i, 0)),
            pl.BlockSpec((bm, bn), lambda i, j: (i, ga0 + j)),
            pl.BlockSpec((bm, bn), lambda i, j: (i, gb0 + j)),
            pl.BlockSpec((dv, bn), lambda i, j: (0, j)),
            pl.BlockSpec((pw, bn), lambda i, j: (0, j)),
        ],
        out_specs=pl.BlockSpec((bm, bn), lambda i, j: (i, j)),
        out_shape=jax.ShapeDtypeStruct((m, d), BF16),
        compiler_params=_cparams("parallel", "arbitrary"),
        name="merge",
    )(og, yp, z, z, w_a, w_b)


def _out_proj_kernel(u_ref, w_ref, x_ref, o_ref):
    o_ref[...] = x_ref[...] + _dot(u_ref[...], w_ref[...])


def _out_proj(u, w, x):
    m, d = x.shape
    bm, bn = min(ROW_TILE, m), _pick(d, COL_TILE)
    assert m % bm == 0 and d % bn == 0
    return pl.pallas_call(
        _out_proj_kernel,
        grid=(m // bm, d // bn),
        in_specs=[
            pl.BlockSpec((bm, u.shape[1]), lambda i, j: (i, 0)),
            pl.BlockSpec((u.shape[1], bn), lambda i, j: (0, j)),
            pl.BlockSpec((bm, bn), lambda i, j: (i, j)),
        ],
        out_specs=pl.BlockSpec((bm, bn), lambda i, j: (i, j)),
        out_shape=jax.ShapeDtypeStruct((m, d), F32),
        compiler_params=_cparams("parallel", "arbitrary"),
        name="out_proj",
    )(u, w, x)


def _rmsnorm_rows(x_ref, g_ref, h_ref, rows):
    def body(c, carry):
        r = pl.multiple_of(c * rows, rows)
        xf = x_ref[pl.ds(r, rows), :]
        ms = jnp.mean(xf * xf, axis=-1, keepdims=True)
        h_ref[pl.ds(r, rows), :] = (xf * lax.rsqrt(ms + EPS) * g_ref[...]).astype(BF16)
        return carry
    lax.fori_loop(0, x_ref.shape[0] // rows, body, 0)


def _swiglu_block(h, wg, wu, wd):
    g = _dot(h, wg)
    u = _dot(h, wu)
    a = (g * jax.nn.sigmoid(g) * u).astype(BF16)
    return _dot(a, wd)


def _ffn_kernel(x_ref, g_ref, wg_ref, wu_ref, wd_ref, *rest, side_blocks):
    f = pl.program_id(1)
    k = len(side_blocks)
    o_ref = rest[k]
    h_ref = rest[2 * k + 1]
    _side_casts(side_blocks, rest[:k], rest[k + 1:2 * k + 1])

    @pl.when(f == 0)
    def _():
        _rmsnorm_rows(x_ref, g_ref, h_ref, 128)
        o_ref[...] = x_ref[...]

    o_ref[...] += _swiglu_block(h_ref[...], wg_ref[...], wu_ref[...], wd_ref[...])


def _ffn(x, g, wg, wu, wd, sides):
    m, d = x.shape
    ff = wg.shape[1]
    bm, bf = min(DENSE_FFN_ROW_TILE, m), min(FFN_COL_TILE, ff)
    assert m % bm == 0 and ff % bf == 0
    ni, nf = m // bm, ff // bf
    plan = _SidePlan(sides, ni, nf)
    out, *side_out = pl.pallas_call(
        functools.partial(_ffn_kernel, side_blocks=plan.blocks),
        grid=(ni, nf),
        in_specs=[
            pl.BlockSpec((bm, d), lambda i, f: (i, 0)),
            pl.BlockSpec((1, d), lambda i, f: (0, 0)),
            pl.BlockSpec((d, bf), lambda i, f: (0, f)),
            pl.BlockSpec((d, bf), lambda i, f: (0, f)),
            pl.BlockSpec((bf, d), lambda i, f: (f, 0)),
        ] + plan.in_specs,
        out_specs=[pl.BlockSpec((bm, d), lambda i, f: (i, 0))] + plan.out_specs,
        out_shape=[jax.ShapeDtypeStruct((m, d), F32)] + plan.out_shapes,
        scratch_shapes=[pltpu.VMEM((bm, d), BF16)],
        compiler_params=_cparams("arbitrary", "arbitrary"),
        name="ffn",
    )(x, g, wg, wu, wd, *plan.arrays)
    return out, plan.finish(side_out)


def _to_slabs(slab_ref, x):
    rows, d = x.shape
    slab_ref[...] = x.astype(BF16).reshape(rows * (d // LANES), LANES)


def _from_slabs(slab_ref, rows, d):
    return slab_ref[...].reshape(rows, d)


def _router_kernel(x_ref, g_ref, wr_ref, hs_ref, route_ref, hb_ref):
    _rmsnorm_rows(x_ref, g_ref, hb_ref, 128)
    hb = hb_ref[...]
    _to_slabs(hs_ref, hb)
    logits = _dot(hb, wr_ref[...])
    lane = lax.broadcasted_iota(I32, logits.shape, 1)
    lane_f = lane.astype(F32)
    neg = jnp.float32(-jnp.inf)
    l1 = jnp.where(lane < N_EXPERTS, logits, neg)
    m1 = jnp.max(l1, axis=-1, keepdims=True)
    i1 = jnp.min(jnp.where(l1 == m1, lane_f, float(LANES)), axis=-1, keepdims=True)
    l2 = jnp.where(lane_f == i1, neg, l1)
    m2 = jnp.max(l2, axis=-1, keepdims=True)
    i2 = jnp.min(jnp.where(l2 == m2, lane_f, float(LANES)), axis=-1, keepdims=True)
    e = jnp.exp(m2 - m1)
    den = 1.0 + e
    route = jnp.where(lane == 0, i1, 0.0)
    route = jnp.where(lane == 1, i2, route)
    route = jnp.where(lane == 2, 1.0 / den, route)
    route = jnp.where(lane == 3, e / den, route)
    route_ref[...] = route


def _router(x, g, wr):
    m, d = x.shape
    bm = min(FFN_ROW_TILE, m)
    sl = d // LANES
    assert m % bm == 0 and sl % 8 == 0
    return pl.pallas_call(
        _router_kernel,
        grid=(m // bm,),
        in_specs=[
            pl.BlockSpec((bm, d), lambda i: (i, 0)),
            pl.BlockSpec((1, d), lambda i: (0, 0)),
            pl.BlockSpec((d, LANES), lambda i: (0, 0)),
        ],
        out_specs=[
            pl.BlockSpec((bm * sl, LANES), lambda i: (i, 0)),
            pl.BlockSpec((bm, LANES), lambda i: (i, 0)),
        ],
        out_shape=[jax.ShapeDtypeStruct((m * sl, LANES), BF16), jax.ShapeDtypeStruct((m, LANES), F32)],
        scratch_shapes=[pltpu.VMEM((bm, d), BF16)],
        compiler_params=_cparams("parallel"),
        name="router",
    )(x, g, wr)


def _route_plan_kernel(route_ref, pos_ref, meta_ref, rank_ref, *, tile, n_tiles, blk):
    m = route_ref.shape[0]
    nb = m // blk
    lane_b = lax.broadcasted_iota(I32, (blk, LANES), 1)
    lane_r = lax.broadcasted_iota(I32, (1, LANES), 1)
    ne = N_EXPERTS

    def onehot(r):
        rt = route_ref[pl.ds(r, blk), :]
        i1 = rt[:, 0:1].astype(I32)
        i2 = rt[:, 1:2].astype(I32)
        return jnp.where((lane_b == i1) | (lane_b == i2 + ne), 1.0, 0.0)

    strict = jnp.where(lax.broadcasted_iota(I32, (blk, blk), 0) > lax.broadcasted_iota(I32, (blk, blk), 1),
                       1.0, 0.0).astype(BF16)

    def rank_body(c, carry):
        r = pl.multiple_of(c * blk, blk)
        oh = onehot(r)
        rank_ref[pl.ds(r, blk), :] = _dot(strict, oh.astype(BF16)) + carry
        return carry + jnp.sum(oh, axis=0, keepdims=True)

    cnt12 = lax.fori_loop(0, nb, rank_body, jnp.zeros((1, LANES), F32))

    def lane_val(row, l):
        return jnp.sum(jnp.where(lane_r == l, row, 0.0), axis=-1, keepdims=True)

    cnt = jnp.zeros((1, LANES), F32)
    for e in range(ne):
        cnt = jnp.where(lane_r == e, lane_val(cnt12, e) + lane_val(cnt12, e + ne), cnt)
    tiles = jnp.floor((cnt + (tile - 1)) * (1.0 / tile))
    start = jnp.zeros((1, LANES), F32)
    for e in range(1, ne):
        start = start + jnp.where(lane_r >= e, lane_val(tiles, e - 1), 0.0)
    start = jnp.where(lane_r < ne, start * tile, 0.0)
    end = start + tiles * tile
    base = start
    for e in range(ne):
        base = jnp.where(lane_r == e + ne, lane_val(start, e) + lane_val(cnt12, e), base)

    def pos_body(c, carry):
        r = pl.multiple_of(c * blk, blk)
        slot = onehot(r) * (rank_ref[pl.ds(r, blk), :] + base)
        p1 = jnp.sum(jnp.where(lane_b < ne, slot, 0.0), axis=-1, keepdims=True)
        p2 = jnp.sum(jnp.where(lane_b >= ne, slot, 0.0), axis=-1, keepdims=True)
        both = jnp.where(lane_b == 0, p1, jnp.where(lane_b == 1, p2, 0.0))
        pos_ref[:, pl.ds(r, blk)] = jnp.transpose(both)[0:8, :].astype(I32)
        return carry

    lax.fori_loop(0, nb, pos_body, 0)

    n_used = lane_val(end, ne - 1) * (1.0 / tile)
    tile_start = lane_r.astype(F32) * tile
    tile_e = jnp.zeros((1, LANES), F32)
    last_e = jnp.zeros((1, 1), F32)
    for e in range(ne):
        tile_e = tile_e + jnp.where(lane_val(end, e) <= tile_start, 1.0, 0.0)
        last_e = jnp.where(lane_val(cnt, e) > 0, float(e), last_e)
    tile_e = jnp.where(lane_r.astype(F32) < n_used, tile_e, last_e)
    info = jnp.where(lane_r < ne, end, jnp.where(lane_r == 2 * ne, n_used, 0.0))
    for e in range(ne):
        info = jnp.where(lane_r == e + ne, lane_val(cnt, e), info)
    row8 = lax.broadcasted_iota(I32, (8, LANES), 0)
    meta = jnp.where(row8 == 0, tile_e, jnp.where(row8 == 1, info, jnp.where(row8 == 2, n_used, 0.0)))
    meta_ref[...] = meta.astype(I32)


def _route_plan(route, tile, n_tiles):
    m = route.shape[0]
    blk = min(PLAN_TILE, m)
    assert m % blk == 0 and n_tiles <= LANES
    return pl.pallas_call(
        functools.partial(_route_plan_kernel, tile=tile, n_tiles=n_tiles, blk=blk),
        out_shape=[jax.ShapeDtypeStruct((8, m), I32), jax.ShapeDtypeStruct((8, LANES), I32)],
        scratch_shapes=[pltpu.VMEM((m, LANES), F32)],
        compiler_params=pltpu.CompilerParams(vmem_limit_bytes=VMEM_LIMIT_BYTES),
        name="route_plan",
    )(route)


def _dispatch_kernel(meta_ref, pos_ref, h_ref, xs_ref, zero_ref, sem, *, tile, sl):
    i = pl.program_id(0)
    tt = h_ref.shape[0] // sl
    n_tiles = xs_ref.shape[0] // (tile * sl)

    def zero_copy(e):
        first = pl.multiple_of((meta_ref[e] - tile) * sl, tile * sl)
        return pltpu.make_async_copy(zero_ref, xs_ref.at[pl.ds(first, tile * sl), :], sem.at[1])

    def tail_copy(j):
        return pltpu.make_async_copy(zero_ref, xs_ref.at[pl.ds(j * tile * sl, tile * sl), :], sem.at[1])

    @pl.when(i == 0)
    def _():
        zero_ref[...] = jnp.zeros_like(zero_ref)
        n_used = meta_ref[2 * N_EXPERTS]
        for e in range(N_EXPERTS):
            @pl.when(meta_ref[N_EXPERTS + e] > 0)
            def _():
                zero_copy(e).start()
        for j in range(n_tiles - N_EXPERTS, n_tiles):
            @pl.when(j >= n_used)
            def _():
                tail_copy(j).start()
        for e in range(N_EXPERTS):
            @pl.when(meta_ref[N_EXPERTS + e] > 0)
            def _():
                zero_copy(e).wait()
        for j in range(n_tiles - N_EXPERTS, n_tiles):
            @pl.when(j >= n_used)
            def _():
                tail_copy(j).wait()

    def row_copy(r, c):
        src = pl.multiple_of(r * sl, sl)
        dst = pl.multiple_of(pos_ref[0, c * tt + r] * sl, sl)
        return pltpu.make_async_copy(h_ref.at[pl.ds(src, sl), :], xs_ref.at[pl.ds(dst, sl), :], sem.at[0])

    def start_body(r, carry):
        row_copy(r, 0).start(priority=0)
        row_copy(r, 1).start(priority=1)
        return carry

    lax.fori_loop(0, tt, start_body, 0, unroll=8)
    for _ in range(2):
        pltpu.make_async_copy(h_ref, xs_ref.at[pl.ds(0, tt * sl), :], sem.at[0]).wait()


def _dispatch(hs, pos_tiles, ginfo, n_slots, tile, sl):
    m = hs.shape[0] // sl
    tt = pos_tiles.shape[2] // 2
    return pl.pallas_call(
        functools.partial(_dispatch_kernel, tile=tile, sl=sl),
        grid_spec=pltpu.PrefetchScalarGridSpec(
            num_scalar_prefetch=1,
            grid=(m // tt,),
            in_specs=[
                pl.BlockSpec((None, 1, 2 * tt), lambda i, meta: (i, 0, 0), memory_space=pltpu.SMEM),
                pl.BlockSpec((tt * sl, LANES), lambda i, meta: (i, 0)),
            ],
            out_specs=pl.BlockSpec(memory_space=pl.ANY),
            scratch_shapes=[pltpu.VMEM((tile * sl, LANES), BF16), pltpu.SemaphoreType.DMA((2,))],
        ),
        out_shape=jax.ShapeDtypeStruct((n_slots * sl, LANES), BF16),
        compiler_params=_cparams("arbitrary"),
        name="dispatch",
    )(ginfo, pos_tiles, hs)


def _expert_ffn_kernel(te_ref, nu_ref, xs_ref, wg_ref, wu_ref, wd_ref, y_ref, h_ref, acc_ref):
    i, f = pl.program_id(0), pl.program_id(1)
    used = i < nu_ref[0]
    last = pl.num_programs(1) - 1
    tile, d = h_ref.shape

    @pl.when(used & (f == 0))
    def _():
        h_ref[...] = _from_slabs(xs_ref, tile, d).astype(BF16)
        acc_ref[...] = jnp.zeros_like(acc_ref)

    @pl.when(used)
    def _():
        acc_ref[...] += _swiglu_block(h_ref[...], wg_ref[...], wu_ref[...], wd_ref[...])

    @pl.when(used & (f == last))
    def _():
        _to_slabs(y_ref, acc_ref[...])

    @pl.when(jnp.logical_not(used) & (f == last))
    def _():
        y_ref[...] = jnp.zeros_like(y_ref)


def _expert_ffn(xs, tile_e, n_used, wg, wu, wd, tile, sl):
    n_slots = xs.shape[0] // sl
    d = sl * LANES
    ff = wg.shape[2]
    bf = min(FFN_COL_TILE, ff)
    assert n_slots % tile == 0 and ff % bf == 0
    nf = ff // bf
    row = lambda i, f, te, nu: (jnp.minimum(i, nu[0] - 1), 0)
    fblk = lambda i, f, nu: jnp.where(i < nu[0], f, nf - 1)
    return pl.pallas_call(
        _expert_ffn_kernel,
        grid_spec=pltpu.PrefetchScalarGridSpec(
            num_scalar_prefetch=2,
            grid=(n_slots // tile, nf),
            in_specs=[
                pl.BlockSpec((tile * sl, LANES), row),
                pl.BlockSpec((None, d, bf), lambda i, f, te, nu: (te[i], 0, fblk(i, f, nu))),
                pl.BlockSpec((None, d, bf), lambda i, f, te, nu: (te[i], 0, fblk(i, f, nu))),
                pl.BlockSpec((None, bf, d), lambda i, f, te, nu: (te[i], fblk(i, f, nu), 0)),
            ],
            out_specs=pl.BlockSpec((tile * sl, LANES), lambda i, f, te, nu: (i, 0)),
            scratch_shapes=[pltpu.VMEM((tile, d), BF16), pltpu.VMEM((tile, d), F32)],
        ),
        out_shape=jax.ShapeDtypeStruct((n_slots * sl, LANES), BF16),
        compiler_params=_cparams("parallel", "arbitrary"),
        name="expert_ffn",
    )(tile_e, n_used, xs, wg, wu, wd)


def _combine_kernel(pos_ref, nxt_ref, x_ref, route_ref, g_ref, y_ref, o_ref, buf_ref, sem):
    i = pl.program_id(0)
    n = pl.num_programs(0)
    tt, d = x_ref.shape
    sl = d // LANES
    slot = lax.rem(i, 2)

    def gather(p_ref, dst_slot):
        def row_copy(r, c):
            src = pl.multiple_of(p_ref[0, c * tt + r] * sl, sl)
            dst = pl.multiple_of(r * sl, sl)
            return pltpu.make_async_copy(y_ref.at[pl.ds(src, sl), :], buf_ref.at[dst_slot, c, pl.ds(dst, sl), :],
                                         sem.at[dst_slot])

        def start_body(r, carry):
            row_copy(r, 0).start()
            row_copy(r, 1).start()
            return carry

        lax.fori_loop(0, tt, start_body, 0, unroll=8)

    @pl.when(i == 0)
    def _():
        gather(pos_ref, 0)

    @pl.when(i + 1 < n)
    def _():
        gather(nxt_ref, 1 - slot)

    for c in range(2):
        pltpu.make_async_copy(y_ref.at[pl.ds(0, tt * sl), :], buf_ref.at[slot, c], sem.at[slot]).wait()

    rt = route_ref[...]
    moe = (rt[:, 2:3] * _from_slabs(buf_ref.at[slot, 0], tt, d)
           + rt[:, 3:4] * _from_slabs(buf_ref.at[slot, 1], tt, d))
    xo = x_ref[...] + moe
    ms = jnp.mean(xo * xo, axis=-1, keepdims=True)
    o_ref[...] = xo * lax.rsqrt(ms + EPS) * g_ref[...]


def _combine(x, route, g, y, pos_tiles):
    m, d = x.shape
    tt = pos_tiles.shape[2] // 2
    nt = m // tt
    return pl.pallas_call(
        _combine_kernel,
        grid=(nt,),
        in_specs=[
            pl.BlockSpec((None, 1, 2 * tt), lambda i: (i, 0, 0), memory_space=pltpu.SMEM),
            pl.BlockSpec((None, 1, 2 * tt), lambda i: (jnp.minimum(i + 1, nt - 1), 0, 0), memory_space=pltpu.SMEM),
            pl.BlockSpec((tt, d), lambda i: (i, 0)),
            pl.BlockSpec((tt, LANES), lambda i: (i, 0)),
            pl.BlockSpec((1, d), lambda i: (0, 0)),
            pl.BlockSpec(memory_space=pl.ANY),
        ],
        out_specs=pl.BlockSpec((tt, d), lambda i: (i, 0)),
        out_shape=jax.ShapeDtypeStruct((m, d), F32),
        scratch_shapes=[pltpu.VMEM((2, 2, tt * (d // LANES), LANES), BF16), pltpu.SemaphoreType.DMA((2,))],
        compiler_params=_cparams("arbitrary"),
        name="combine",
    )(pos_tiles, pos_tiles, x, route, g, y)


def _moe_ffn_final(x, g_ffn, wr, wg, wu, wd, g_final):
    m, d = x.shape
    tile = min(FFN_ROW_TILE, m)
    n_tiles = (2 * m) // tile + N_EXPERTS
    tt = min(TOK_TILE, m)
    sl = d // LANES
    hs, route = _router(x, g_ffn, wr)
    pos, meta = _route_plan(route, tile, n_tiles)
    pos_tiles = pos[:2].reshape(2, m // tt, tt).transpose(1, 0, 2).reshape(m // tt, 1, 2 * tt)
    xs = _dispatch(hs, pos_tiles, meta[1, :2 * N_EXPERTS + 1], n_tiles * tile, tile, sl)
    y = _expert_ffn(xs, meta[0, :n_tiles], meta[2, :1], wg, wu, wd, tile, sl)
    return _combine(x, route, g_final, y, pos_tiles)


def kernel(x, ln_mix, w_in, w_alpha, b_alpha, gla_norm, w_pool, pool_scale, w_branch_gla, w_branch_pool, w_out,
           ln_ffn, ffn_w_gate, ffn_w_up, ffn_w_down, router_w, exp_w_gate, exp_w_up, exp_w_down, ln_final):
    batch, seq, d = x.shape
    depth = w_in.shape[0]
    dk = w_alpha.shape[2]
    dv = w_branch_gla.shape[1]
    pw = w_branch_pool.shape[1]
    rank = w_alpha.shape[1]
    assert depth == 2 and rank <= LANES
    a_col = 2 * dk + 2 * dv
    m = batch * seq
    xf = x.reshape(m, d)
    row = lambda v: v.reshape(1, -1)
    w_in_t = jnp.swapaxes(w_in, 1, 2)

    assert depth == 2 and exp_w_gate.shape[0] == 1 and ffn_w_gate.shape[0] == 1
    mixer = lambda l: (_to_bf16(w_pool, l), _to_bf16(w_branch_gla, l), _to_bf16(w_branch_pool, l),
                       _to_bf16(w_out, l))
    for l in range(depth):
        w_main, w_a1 = _inproj_weight(w_in_t, l, a_col, rank)
        w_al = jnp.pad(w_alpha[l], ((0, LANES - rank), (0, 0)))
        if l == 0:
            z, a1, (e_gate, f_gate, f_up, f_down) = _norm_inproj(
                xf, row(ln_mix[l]), w_main, w_a1,
                [(exp_w_gate, 0), (ffn_w_gate, 0), (ffn_w_up, 0), (ffn_w_down, 0)])
            wp, wbg, wbp, wo = mixer(0)
        else:
            z, a1, (e_up,) = _norm_inproj(xf, row(ln_mix[l]), w_main, w_a1, [(exp_w_up, 0)])
        og = _gla(z, a1, w_al, row(b_alpha[l]), row(gla_norm[l]), batch, seq, dk, dv)
        yp = _pool(z, wp, row(pool_scale[l]), batch, seq, a_col)
        u = _merge(og, yp, z, wbg, wbp, a_col + pw, a_col + pw + d)
        xf = _out_proj(u, wo, xf)
        if l == 0:
            xf, (e_down, wp, wbg, wbp, wo) = _ffn(
                xf, row(ln_ffn[l]), f_gate, f_up, f_down,
                [(exp_w_down, 0), (w_pool, 1), (w_branch_gla, 1), (w_branch_pool, 1), (w_out, 1)])
        else:
            wr = jnp.pad(router_w[0], ((0, 0), (0, LANES - N_EXPERTS))).astype(BF16)
            xf = _moe_ffn_final(xf, row(ln_ffn[l]), wr, e_gate, e_up, e_down, ln_final.reshape(1, -1))
    return xf.reshape(batch, seq, d)
```

```python
import functools
import math

import jax
import jax.numpy as jnp
from jax import lax
from jax.experimental import pallas as pl
from jax.experimental.pallas import tpu as pltpu

F32 = jnp.float32
BF16 = jnp.bfloat16
I32 = jnp.int32

EPS = 1e-6
GLA_HEADS = 4
GLA_GATE_TEMP = 16.0
GLA_CHUNK = 64
GLA_HEADS_PER_STEP = 4
POOL_WINDOWS = (2, 4, 8, 16)
N_EXPERTS = 8

LANES = 128
VMEM_LIMIT_BYTES = 56 * 1024 * 1024

ROW_TILE = 1024
COL_TILE = 1024
FFN_ROW_TILE = 512
FFN_COL_TILE = 512
SEQ_TILE = 512
TOK_TILE = 256
PLAN_TILE = 512
HALO = 16


def _cparams(*sem):
    return pltpu.CompilerParams(dimension_semantics=sem, vmem_limit_bytes=VMEM_LIMIT_BYTES)


def _dot(a, b):
    return jnp.dot(a, b, preferred_element_type=F32)


def _dot_nt(a, b):
    return lax.dot_general(a, b, (((1,), (1,)), ((), ())), preferred_element_type=F32)


def _split_bf16(x):
    hi = x.astype(BF16)
    lo = (x - hi.astype(F32)).astype(BF16)
    return hi, lo


def _pick(n, target, align=LANES):
    if n <= target:
        return n
    best = None
    for cand in range(align, target + 1, align):
        if n % cand == 0:
            best = cand
    assert best is not None, (n, target, align)
    return best


def _cast_kernel(x_ref, o_ref):
    o_ref[...] = x_ref[...].astype(BF16)


def _to_bf16(w, layer):
    shape = w.shape[1:]
    w3 = w.reshape(w.shape[0], -1, shape[-1])
    _, r, c = w3.shape
    br, bc = _pick(r, 1024, 16), _pick(c, 2048)
    out = pl.pallas_call(
        _cast_kernel,
        grid=(r // br, c // bc),
        in_specs=[pl.BlockSpec((None, br, bc), lambda i, j: (layer, i, j))],
        out_specs=pl.BlockSpec((br, bc), lambda i, j: (i, j)),
        out_shape=jax.ShapeDtypeStruct((r, c), BF16),
        compiler_params=_cparams("parallel", "parallel"),
        name="to_bf16",
    )(w3)
    return out.reshape(shape)


def _inproj_weight_kernel(w_ref, nxt_ref, lr_ref, wm_ref, wa_ref, *, first_shifted, rank):
    j = pl.program_id(0)

    @pl.when(j < first_shifted)
    def _():
        wm_ref[...] = w_ref[...].astype(BF16)

    @pl.when(j >= first_shifted)
    def _():
        wm_ref[...] = jnp.concatenate([w_ref[rank:, :], nxt_ref[...]], axis=0).astype(BF16)

    @pl.when(j == 0)
    def _():
        pad = jnp.zeros((wa_ref.shape[0] - rank, wa_ref.shape[1]), F32)
        wa_ref[...] = jnp.concatenate([lr_ref[...], pad], axis=0).astype(BF16)


def _inproj_weight(w_t, layer, a_col, rank):
    _, n_in, d = w_t.shape
    n = n_in - rank
    br = _pick(math.gcd(n, a_col), 512)
    assert a_col % br == 0 and rank % 8 == 0 and rank < LANES and br % rank == 0
    per = br // rank
    return pl.pallas_call(
        functools.partial(_inproj_weight_kernel, first_shifted=a_col // br, rank=rank),
        grid=(n // br,),
        in_specs=[
            pl.BlockSpec((None, br, d), lambda j: (layer, j, 0)),
            pl.BlockSpec((None, rank, d), lambda j: (layer, (j + 1) * per, 0)),
            pl.BlockSpec((None, rank, d), lambda j: (layer, a_col // rank, 0)),
        ],
        out_specs=[
            pl.BlockSpec((br, d), lambda j: (j, 0)),
            pl.BlockSpec((LANES, d), lambda j: (0, 0)),
        ],
        out_shape=[jax.ShapeDtypeStruct((n, d), BF16), jax.ShapeDtypeStruct((LANES, d), BF16)],
        compiler_params=_cparams("arbitrary"),
        name="inproj_weight",
    )(w_t, w_t, w_t)


class _SidePlan:
    def __init__(self, sides, n_outer, n_inner):
        steps = n_outer * n_inner
        self.arrays, self.in_specs, self.out_specs, self.out_shapes, self.shapes, blocks = [], [], [], [], [], []
        for side, layer in sides:
            side3 = side.reshape(side.shape[0], -1, side.shape[-1])
            r, c = side3.shape[1:]
            rb = next(cand for cand in range(16, r + 1, 16) if r % cand == 0 and r // cand <= steps)
            nblk = r // rb
            idx = functools.partial(lambda i, j, last: jnp.minimum(i * n_inner + j, last), last=nblk - 1)
            self.arrays.append(side3)
            self.in_specs.append(pl.BlockSpec(
                (None, rb, c), functools.partial(lambda i, j, idx, layer: (layer, idx(i, j), 0), idx=idx, layer=layer)))
            self.out_specs.append(pl.BlockSpec((rb, c), functools.partial(lambda i, j, idx: (idx(i, j), 0), idx=idx)))
            self.out_shapes.append(jax.ShapeDtypeStruct((r, c), BF16))
            self.shapes.append(side.shape[1:])
            blocks.append(nblk)
        self.blocks = tuple(blocks)

    def finish(self, outs):
        return [o.reshape(s) for o, s in zip(outs, self.shapes)]


def _side_casts(blocks, side_refs, side_out_refs):
    step = pl.program_id(0) * pl.num_programs(1) + pl.program_id(1)
    for n_blocks, src, dst in zip(blocks, side_refs, side_out_refs):
        @pl.when(step < n_blocks)
        def _():
            dst[...] = src[...].astype(BF16)


def _norm_inproj_kernel(x_ref, g_ref, w_ref, wa_ref, *rest, rows, side_blocks):
    k = len(side_blocks)
    z_ref, a1_ref = rest[k:k + 2]
    h_ref = rest[2 * k + 2]
    _side_casts(side_blocks, rest[:k], rest[k + 2:2 * k + 2])

    @pl.when(pl.program_id(1) == 0)
    def _():
        def body(c, carry):
            r = pl.multiple_of(c * rows, rows)
            xf = x_ref[pl.ds(r, rows), :]
            ms = jnp.mean(xf * xf, axis=-1, keepdims=True)
            h_ref[pl.ds(r, rows), :] = (xf * lax.rsqrt(ms + EPS) * g_ref[...]).astype(BF16)
            return carry
        lax.fori_loop(0, x_ref.shape[0] // rows, body, 0)
        a1_ref[...] = _dot_nt(h_ref[...], wa_ref[...])

    z_ref[...] = _dot_nt(h_ref[...], w_ref[...]).astype(BF16)


def _norm_inproj(x, g, w_t, wa_t, sides):
    m, d = x.shape
    n = w_t.shape[0]
    bm, bn = min(ROW_TILE, m), _pick(n, COL_TILE)
    assert m % bm == 0 and n % bn == 0
    ni, nj = m // bm, n // bn
    plan = _SidePlan(sides, ni, nj)
    z, a1, *side_out = pl.pallas_call(
        functools.partial(_norm_inproj_kernel, rows=128, side_blocks=plan.blocks),
        grid=(ni, nj),
        in_specs=[
            pl.BlockSpec((bm, d), lambda i, j: (i, 0)),
            pl.BlockSpec((1, d), lambda i, j: (0, 0)),
            pl.BlockSpec((bn, d), lambda i, j: (j, 0)),
            pl.BlockSpec((LANES, d), lambda i, j: (0, 0)),
        ] + plan.in_specs,
        out_specs=[
            pl.BlockSpec((bm, bn), lambda i, j: (i, j)),
            pl.BlockSpec((bm, LANES), lambda i, j: (i, 0)),
        ] + plan.out_specs,
        out_shape=[jax.ShapeDtypeStruct((m, n), BF16), jax.ShapeDtypeStruct((m, LANES), F32)] + plan.out_shapes,
        scratch_shapes=[pltpu.VMEM((bm, d), BF16)],
        compiler_params=_cparams("arbitrary", "arbitrary"),
        name="norm_inproj",
    )(x, g, w_t, wa_t, *plan.arrays)
    return z, a1, plan.finish(side_out)


def _log_sigmoid(x):
    return jnp.minimum(x, 0.0) - jnp.log(1.0 + jnp.exp(-jnp.abs(x)))


def _gla_kernel(q_ref, k_ref, v_ref, r_ref, a1_ref, wal_ref, bal_ref, gn_ref, tri_ref, o_ref, st_ref, *,
                chunk, heads):
    @pl.when(pl.program_id(2) == 0)
    def _():
        st_ref[...] = jnp.zeros_like(st_ref)

    hk = q_ref.shape[1] // heads
    hv = v_ref.shape[1] // heads
    a1 = _split_bf16(a1_ref[...])
    for h in range(heads):
        ks, vs = slice(h * hk, (h + 1) * hk), slice(h * hv, (h + 1) * hv)
        o, st_new = _gla_head(q_ref[:, ks], k_ref[:, ks], v_ref[:, vs], r_ref[:, vs], a1, wal_ref[:, ks],
                              bal_ref[:, ks], gn_ref[...], tri_ref[...], st_ref[h], chunk)
        o_ref[:, vs] = o
        st_ref[h] = st_new


def _gla_head(q, k, v, r, a1, wal, bal, gn, tri, st, chunk):
    t, dk = q.shape
    nc = t // chunk
    pair = 2 * chunk
    shift = chunk.bit_length() - 1
    nt_dot = _dot_nt
    rows = lambda x, c0, c1: x[c0 * chunk:c1 * chunk, :]

    a_hi, a_lo = a1
    w_hi, w_lo = _split_bf16(wal)
    xg = _dot(a_hi, w_hi) + _dot(a_hi, w_lo) + _dot(a_lo, w_hi) + bal
    la = _log_sigmoid(xg) * (1.0 / GLA_GATE_TEMP)

    la_hi, la_lo = _split_bf16(la)
    b = _dot(tri, la_hi) + _dot(tri, la_lo)

    bl = [b[(c + 1) * chunk - 1:(c + 1) * chunk, :] for c in range(nc)]
    pre = [jnp.zeros_like(bl[0])]
    for c in range(nc):
        pre.append(pre[c] + bl[c])

    q = q.astype(F32)
    k = k.astype(F32)
    q_t = q * jnp.exp(b) * (dk ** -0.5)
    k_t = (k * jnp.exp(-b)).astype(BF16)
    qc = [rows(q_t, c, c + 1) for c in range(nc)]
    kec = [rows(k, c, c + 1) * jnp.exp(bl[c] - rows(b, c, c + 1)) for c in range(nc)]
    q_tb = q_t.astype(BF16)
    k_eb = jnp.concatenate(kec, axis=0).astype(BF16)

    def q_from(mid, hi):
        return jnp.concatenate([qc[c] if c == mid else qc[c] * jnp.exp(pre[c] - pre[mid])
                                for c in range(mid, hi)], axis=0).astype(BF16)

    def k_upto(lo, mid):
        return jnp.concatenate([kec[m] if m == mid - 1 else kec[m] * jnp.exp(pre[mid] - pre[m + 1])
                                for m in range(lo, mid)], axis=0).astype(BF16)

    nb = t // pair
    blocks = [[None] * nb for _ in range(nb)]
    r2 = lax.broadcasted_iota(I32, (pair, pair), 0)
    c2 = lax.broadcasted_iota(I32, (pair, pair), 1)
    same = ((r2 >> shift) == (c2 >> shift)) & (r2 >= c2)
    cross = (r2 >= chunk) & (c2 < chunk)
    for p in range(nb):
        qp = rows(q_tb, 2 * p, 2 * p + 2)
        in_chunk = nt_dot(qp, rows(k_t, 2 * p, 2 * p + 2))
        next_chunk = nt_dot(qp, rows(k_eb, 2 * p, 2 * p + 2))
        blocks[p][p] = jnp.where(same, in_chunk, jnp.where(cross, next_chunk, 0.0))
    g = 4
    while g <= nc:
        half = g // 2
        for grp in range(nc // g):
            lo, mid, hi = grp * g, grp * g + half, (grp + 1) * g
            x = nt_dot(q_from(mid, hi), k_upto(lo, mid))
            hb = half // 2
            for i in range(hb):
                for j in range(hb):
                    blocks[mid // 2 + i][lo // 2 + j] = x[i * pair:(i + 1) * pair, j * pair:(j + 1) * pair]
        g *= 2
    zero_blk = jnp.zeros((pair, pair), F32)
    att = jnp.concatenate(
        [jnp.concatenate([zero_blk if blk is None else blk for blk in brow], axis=1) for brow in blocks], axis=0)

    o = _dot(att.astype(BF16), v) + nt_dot(q_from(0, nc), st.astype(BF16))
    upd = lax.dot_general(v, k_upto(0, nc), (((0,), (0,)), ((), ())), preferred_element_type=F32)
    st_new = st * jnp.exp(pre[nc]) + upd

    o = o * lax.rsqrt(jnp.mean(o * o, axis=-1, keepdims=True) + EPS) * gn
    r = r.astype(F32)
    return (o * (r * jax.nn.sigmoid(r))).astype(BF16), st_new


def _gla(z, a1, w_alpha, b_alpha, gla_norm, batch, seq, dk_total, dv_total):
    m = z.shape[0]
    hk, hv = dk_total // GLA_HEADS, dv_total // GLA_HEADS
    t = min(SEQ_TILE, seq)
    assert seq % t == 0 and t % GLA_CHUNK == 0 and hk % LANES == 0 and hv % LANES == 0
    ns = seq // t
    nc = t // GLA_CHUNK
    assert nc >= 2 and nc & (nc - 1) == 0 and GLA_HEADS % GLA_HEADS_PER_STEP == 0
    hps = GLA_HEADS_PER_STEP
    gk, gv = hps * hk, hps * hv
    k_off = dk_total // gk
    v_off = (2 * dk_total) // gv
    r_off = (2 * dk_total + dv_total) // gv
    idx = jnp.arange(t, dtype=I32)
    tri = ((idx[:, None] // GLA_CHUNK == idx[None, :] // GLA_CHUNK) & (idx[:, None] >= idx[None, :])).astype(BF16)
    rowi = lambda b, h, s: b * ns + s
    return pl.pallas_call(
        functools.partial(_gla_kernel, chunk=GLA_CHUNK, heads=hps),
        grid=(batch, GLA_HEADS // hps, ns),
        in_specs=[
            pl.BlockSpec((t, gk), lambda b, h, s: (rowi(b, h, s), h)),
            pl.BlockSpec((t, gk), lambda b, h, s: (rowi(b, h, s), k_off + h)),
            pl.BlockSpec((t, gv), lambda b, h, s: (rowi(b, h, s), v_off + h)),
            pl.BlockSpec((t, gv), lambda b, h, s: (rowi(b, h, s), r_off + h)),
            pl.BlockSpec((t, LANES), lambda b, h, s: (rowi(b, h, s), 0)),
            pl.BlockSpec((LANES, gk), lambda b, h, s: (0, h)),
            pl.BlockSpec((1, gk), lambda b, h, s: (0, h)),
            pl.BlockSpec((1, hv), lambda b, h, s: (0, 0)),
            pl.BlockSpec((t, t), lambda b, h, s: (0, 0)),
        ],
        out_specs=pl.BlockSpec((t, gv), lambda b, h, s: (rowi(b, h, s), h)),
        out_shape=jax.ShapeDtypeStruct((m, dv_total), BF16),
        scratch_shapes=[pltpu.VMEM((hps, hv, hk), F32)],
        compiler_params=_cparams("parallel", "parallel", "arbitrary"),
        name="gla",
    )(z, z, z, z, a1, w_alpha, b_alpha, gla_norm, tri)


def _pool_kernel(p_ref, ph_ref, wp_ref, sc_ref, y_ref, *, windows):
    s = pl.program_id(1)
    t = p_ref.shape[0]
    halo = ph_ref.shape[0]
    gw = wp_ref.shape[1]
    dist = lax.broadcasted_iota(I32, (t, t), 0) - lax.broadcasted_iota(I32, (t, t), 1)
    dist_h = lax.broadcasted_iota(I32, (t, halo), 0) + halo - lax.broadcasted_iota(I32, (t, halo), 1)
    pos = s * t + lax.broadcasted_iota(I32, (t, 1), 0)
    for g, w in enumerate(windows):
        cs = slice(g * gw, (g + 1) * gw)
        pc = p_ref[:, cs]
        band = jnp.where((dist >= 0) & (dist < w), 1.0, 0.0).astype(BF16)
        band_h = jnp.where(dist_h < w, 1.0, 0.0).astype(BF16)
        tot = _dot(band, pc) + jnp.where(s > 0, _dot(band_h, ph_ref[:, cs]), 0.0)
        cnt = jnp.minimum(pos + 1, w).astype(F32)
        mixed = tot / cnt - pc.astype(F32)
        y = _dot(mixed.astype(BF16), wp_ref[g]) * sc_ref[:, cs]
        y_ref[:, cs] = y.astype(BF16)


def _pool(z, w_pool, pool_scale, batch, seq, p_col0):
    m = z.shape[0]
    groups, gw, _ = w_pool.shape
    pw = groups * gw
    t = min(SEQ_TILE, seq)
    assert seq % t == 0 and t % HALO == 0 and p_col0 % pw == 0 and gw % LANES == 0
    ns = seq // t
    pc = p_col0 // pw
    hb = t // HALO
    return pl.pallas_call(
        functools.partial(_pool_kernel, windows=POOL_WINDOWS),
        grid=(batch, ns),
        in_specs=[
            pl.BlockSpec((t, pw), lambda b, s: (b * ns + s, pc)),
            pl.BlockSpec((HALO, pw), lambda b, s: (jnp.maximum((b * ns + s) * hb - 1, 0), pc)),
            pl.BlockSpec((groups, gw, gw), lambda b, s: (0, 0, 0)),
            pl.BlockSpec((1, pw), lambda b, s: (0, 0)),
        ],
        out_specs=pl.BlockSpec((t, pw), lambda b, s: (b * ns + s, 0)),
        out_shape=jax.ShapeDtypeStruct((m, pw), BF16),
        compiler_params=_cparams("parallel", "arbitrary"),
        name="pool",
    )(z, z, w_pool, pool_scale)


def _merge_kernel(og_ref, yp_ref, ga_ref, gb_ref, wa_ref, wb_ref, u_ref):
    ya = _dot(og_ref[...], wa_ref[...])
    yb = _dot(yp_ref[...], wb_ref[...])
    ga = jax.nn.sigmoid(ga_ref[...].astype(F32))
    gb = jax.nn.sigmoid(gb_ref[...].astype(F32))
    u_ref[...] = (ga * ya + gb * yb).astype(BF16)


def _merge(og, yp, z, w_a, w_b, ga_col0, gb_col0):
    m, dv = og.shape
    pw = yp.shape[1]
    d = w_a.shape[1]
    bm, bn = min(ROW_TILE, m), _pick(math.gcd(d, ga_col0, gb_col0), COL_TILE)
    assert m % bm == 0 and d % bn == 0 and ga_col0 % bn == 0 and gb_col0 % bn == 0
    ga0, gb0 = ga_col0 // bn, gb_col0 // bn
    return pl.pallas_call(
        _merge_kernel,
        grid=(m // bm, d // bn),
        in_specs=[
            pl.BlockSpec((bm, dv), lambda i, j: (i, 0)),
            pl.BlockSpec((bm, pw), lambda i, j: (i, 0)),
            pl.BlockSpec((bm, bn), lambda i, j: (i, ga0 + j)),
            pl.BlockSpec((bm, bn), lambda i, j: (i, gb0 + j)),
            pl.BlockSpec((dv, bn), lambda i, j: (0, j)),
            pl.BlockSpec((pw, bn), lambda i, j: (0, j)),
        ],
        out_specs=pl.BlockSpec((bm, bn), lambda i, j: (i, j)),
        out_shape=jax.ShapeDtypeStruct((m, d), BF16),
        compiler_params=_cparams("parallel", "arbitrary"),
        name="merge",
    )(og, yp, z, z, w_a, w_b)


def _out_proj_kernel(u_ref, w_ref, x_ref, o_ref):
    o_ref[...] = x_ref[...] + _dot(u_ref[...], w_ref[...])


def _out_proj(u, w, x):
    m, d = x.shape
    bm, bn = min(ROW_TILE, m), _pick(d, COL_TILE)
    assert m % bm == 0 and d % bn == 0
    return pl.pallas_call(
        _out_proj_kernel,
        grid=(m // bm, d // bn),
        in_specs=[
            pl.BlockSpec((bm, u.shape[1]), lambda i, j: (i, 0)),
            pl.BlockSpec((u.shape[1], bn), lambda i, j: (0, j)),
            pl.BlockSpec((bm, bn), lambda i, j: (i, j)),
        ],
        out_specs=pl.BlockSpec((bm, bn), lambda i, j: (i, j)),
        out_shape=jax.ShapeDtypeStruct((m, d), F32),
        compiler_params=_cparams("parallel", "arbitrary"),
        name="out_proj",
    )(u, w, x)


def _rmsnorm_rows(x_ref, g_ref, h_ref, rows):
    def body(c, carry):
        r = pl.multiple_of(c * rows, rows)
        xf = x_ref[pl.ds(r, rows), :]
        ms = jnp.mean(xf * xf, axis=-1, keepdims=True)
        h_ref[pl.ds(r, rows), :] = (xf * lax.rsqrt(ms + EPS) * g_ref[...]).astype(BF16)
        return carry
    lax.fori_loop(0, x_ref.shape[0] // rows, body, 0)


def _swiglu_block(h, wg, wu, wd):
    g = _dot(h, wg)
    u = _dot(h, wu)
    a = (g * jax.nn.sigmoid(g) * u).astype(BF16)
    return _dot(a, wd)


def _ffn_kernel(x_ref, g_ref, wg_ref, wu_ref, wd_ref, *rest, side_blocks):
    f = pl.program_id(1)
    k = len(side_blocks)
    o_ref = rest[k]
    h_ref = rest[2 * k + 1]
    _side_casts(side_blocks, rest[:k], rest[k + 1:2 * k + 1])

    @pl.when(f == 0)
    def _():
        _rmsnorm_rows(x_ref, g_ref, h_ref, 128)
        o_ref[...] = x_ref[...]

    o_ref[...] += _swiglu_block(h_ref[...], wg_ref[...], wu_ref[...], wd_ref[...])


def _ffn(x, g, wg, wu, wd, sides):
    m, d = x.shape
    ff = wg.shape[1]
    bm, bf = min(FFN_ROW_TILE, m), min(FFN_COL_TILE, ff)
    assert m % bm == 0 and ff % bf == 0
    ni, nf = m // bm, ff // bf
    plan = _SidePlan(sides, ni, nf)
    out, *side_out = pl.pallas_call(
        functools.partial(_ffn_kernel, side_blocks=plan.blocks),
        grid=(ni, nf),
        in_specs=[
            pl.BlockSpec((bm, d), lambda i, f: (i, 0)),
            pl.BlockSpec((1, d), lambda i, f: (0, 0)),
            pl.BlockSpec((d, bf), lambda i, f: (0, f)),
            pl.BlockSpec((d, bf), lambda i, f: (0, f)),
            pl.BlockSpec((bf, d), lambda i, f: (f, 0)),
        ] + plan.in_specs,
        out_specs=[pl.BlockSpec((bm, d), lambda i, f: (i, 0))] + plan.out_specs,
        out_shape=[jax.ShapeDtypeStruct((m, d), F32)] + plan.out_shapes,
        scratch_shapes=[pltpu.VMEM((bm, d), BF16)],
        compiler_params=_cparams("arbitrary", "arbitrary"),
        name="ffn",
    )(x, g, wg, wu, wd, *plan.arrays)
    return out, plan.finish(side_out)


def _to_slabs(slab_ref, x):
    rows, d = x.shape
    slab_ref[...] = x.astype(BF16).reshape(rows * (d // LANES), LANES)


def _from_slabs(slab_ref, rows, d):
    return slab_ref[...].reshape(rows, d)


def _router_kernel(x_ref, g_ref, wr_ref, hs_ref, route_ref, hb_ref):
    _rmsnorm_rows(x_ref, g_ref, hb_ref, 128)
    hb = hb_ref[...]
    _to_slabs(hs_ref, hb)
    logits = _dot(hb, wr_ref[...])
    lane = lax.broadcasted_iota(I32, logits.shape, 1)
    lane_f = lane.astype(F32)
    neg = jnp.float32(-jnp.inf)
    l1 = jnp.where(lane < N_EXPERTS, logits, neg)
    m1 = jnp.max(l1, axis=-1, keepdims=True)
    i1 = jnp.min(jnp.where(l1 == m1, lane_f, float(LANES)), axis=-1, keepdims=True)
    l2 = jnp.where(lane_f == i1, neg, l1)
    m2 = jnp.max(l2, axis=-1, keepdims=True)
    i2 = jnp.min(jnp.where(l2 == m2, lane_f, float(LANES)), axis=-1, keepdims=True)
    e = jnp.exp(m2 - m1)
    den = 1.0 + e
    route = jnp.where(lane == 0, i1, 0.0)
    route = jnp.where(lane == 1, i2, route)
    route = jnp.where(lane == 2, 1.0 / den, route)
    route = jnp.where(lane == 3, e / den, route)
    route_ref[...] = route


def _router(x, g, wr):
    m, d = x.shape
    bm = min(FFN_ROW_TILE, m)
    sl = d // LANES
    assert m % bm == 0 and sl % 8 == 0
    return pl.pallas_call(
        _router_kernel,
        grid=(m // bm,),
        in_specs=[
            pl.BlockSpec((bm, d), lambda i: (i, 0)),
            pl.BlockSpec((1, d), lambda i: (0, 0)),
            pl.BlockSpec((d, LANES), lambda i: (0, 0)),
        ],
        out_specs=[
            pl.BlockSpec((bm * sl, LANES), lambda i: (i, 0)),
            pl.BlockSpec((bm, LANES), lambda i: (i, 0)),
        ],
        out_shape=[jax.ShapeDtypeStruct((m * sl, LANES), BF16), jax.ShapeDtypeStruct((m, LANES), F32)],
        scratch_shapes=[pltpu.VMEM((bm, d), BF16)],
        compiler_params=_cparams("parallel"),
        name="router",
    )(x, g, wr)


def _route_plan_kernel(route_ref, pos_ref, meta_ref, rank_ref, *, tile, blk):
    m = route_ref.shape[0]
    nb = m // blk
    lane_b = lax.broadcasted_iota(I32, (blk, LANES), 1)
    lane_r = lax.broadcasted_iota(I32, (1, LANES), 1)
    ne = N_EXPERTS

    def onehot(r):
        rt = route_ref[pl.ds(r, blk), :]
        i1 = rt[:, 0:1].astype(I32)
        i2 = rt[:, 1:2].astype(I32)
        return jnp.where((lane_b == i1) | (lane_b == i2 + ne), 1.0, 0.0)

    strict = jnp.where(lax.broadcasted_iota(I32, (blk, blk), 0) > lax.broadcasted_iota(I32, (blk, blk), 1),
                       1.0, 0.0).astype(BF16)

    def rank_body(c, carry):
        r = pl.multiple_of(c * blk, blk)
        oh = onehot(r)
        rank_ref[pl.ds(r, blk), :] = _dot(strict, oh.astype(BF16)) + carry
        return carry + jnp.sum(oh, axis=0, keepdims=True)

    cnt12 = lax.fori_loop(0, nb, rank_body, jnp.zeros((1, LANES), F32))

    def lane_val(row, l):
        return jnp.sum(jnp.where(lane_r == l, row, 0.0), axis=-1, keepdims=True)

    cnt = jnp.zeros((1, LANES), F32)
    for e in range(ne):
        cnt = jnp.where(lane_r == e, lane_val(cnt12, e) + lane_val(cnt12, e + ne), cnt)
    tiles = jnp.floor((cnt + (tile - 1)) * (1.0 / tile))
    start = jnp.zeros((1, LANES), F32)
    for e in range(1, ne):
        start = start + jnp.where(lane_r >= e, lane_val(tiles, e - 1), 0.0)
    start = jnp.where(lane_r < ne, start * tile, 0.0)
    end = start + tiles * tile
    base = start
    for e in range(ne):
        base = jnp.where(lane_r == e + ne, lane_val(start, e) + lane_val(cnt12, e), base)

    def pos_body(c, carry):
        r = pl.multiple_of(c * blk, blk)
        slot = onehot(r) * (rank_ref[pl.ds(r, blk), :] + base)
        p1 = jnp.sum(jnp.where(lane_b < ne, slot, 0.0), axis=-1, keepdims=True)
        p2 = jnp.sum(jnp.where(lane_b >= ne, slot, 0.0), axis=-1, keepdims=True)
        both = jnp.where(lane_b == 0, p1, jnp.where(lane_b == 1, p2, 0.0))
        pos_ref[:, pl.ds(r, blk)] = jnp.transpose(both)[0:8, :].astype(I32)
        return carry

    lax.fori_loop(0, nb, pos_body, 0)

    n_used = lane_val(end, ne - 1) * (1.0 / tile)
    tile_start = lane_r.astype(F32) * tile
    tile_e = jnp.zeros((1, LANES), F32)
    last_e = jnp.zeros((1, 1), F32)
    for e in range(ne):
        tile_e = tile_e + jnp.where(lane_val(end, e) <= tile_start, 1.0, 0.0)
        last_e = jnp.where(lane_val(cnt, e) > 0, float(e), last_e)
    tile_e = jnp.where(lane_r.astype(F32) < n_used, tile_e, last_e)
    info = jnp.where(lane_r < ne, end, jnp.where(lane_r == 2 * ne, n_used, 0.0))
    for e in range(ne):
        info = jnp.where(lane_r == e + ne, lane_val(cnt, e), info)
    row8 = lax.broadcasted_iota(I32, (8, LANES), 0)
    meta = jnp.where(row8 == 0, tile_e, jnp.where(row8 == 1, info, jnp.where(row8 == 2, n_used, 0.0)))
    meta_ref[...] = meta.astype(I32)


def _route_plan(route, tile, n_tiles):
    m = route.shape[0]
    blk = min(PLAN_TILE, m)
    assert m % blk == 0 and n_tiles <= LANES
    return pl.pallas_call(
        functools.partial(_route_plan_kernel, tile=tile, blk=blk),
        out_shape=[jax.ShapeDtypeStruct((8, m), I32), jax.ShapeDtypeStruct((8, LANES), I32)],
        scratch_shapes=[pltpu.VMEM((m, LANES), F32)],
        compiler_params=pltpu.CompilerParams(vmem_limit_bytes=VMEM_LIMIT_BYTES),
        name="route_plan",
    )(route)


def _dispatch_kernel(meta_ref, pos_ref, h_ref, xs_ref, zero_ref, sem, *, tile, sl):
    i = pl.program_id(0)
    tt = h_ref.shape[0] // sl
    n_tiles = xs_ref.shape[0] // (tile * sl)

    def zero_copy(e):
        first = pl.multiple_of((meta_ref[e] - tile) * sl, tile * sl)
        return pltpu.make_async_copy(zero_ref, xs_ref.at[pl.ds(first, tile * sl), :], sem.at[1])

    def tail_copy(j):
        return pltpu.make_async_copy(zero_ref, xs_ref.at[pl.ds(j * tile * sl, tile * sl), :], sem.at[1])

    @pl.when(i == 0)
    def _():
        zero_ref[...] = jnp.zeros_like(zero_ref)
        n_used = meta_ref[2 * N_EXPERTS]
        for e in range(N_EXPERTS):
            @pl.when(meta_ref[N_EXPERTS + e] > 0)
            def _():
                zero_copy(e).start()
        for j in range(n_tiles - N_EXPERTS, n_tiles):
            @pl.when(j >= n_used)
            def _():
                tail_copy(j).start()
        for e in range(N_EXPERTS):
            @pl.when(meta_ref[N_EXPERTS + e] > 0)
            def _():
                zero_copy(e).wait()
        for j in range(n_tiles - N_EXPERTS, n_tiles):
            @pl.when(j >= n_used)
            def _():
                tail_copy(j).wait()

    def row_copy(r, c):
        src = pl.multiple_of(r * sl, sl)
        dst = pl.multiple_of(pos_ref[0, c * tt + r] * sl, sl)
        return pltpu.make_async_copy(h_ref.at[pl.ds(src, sl), :], xs_ref.at[pl.ds(dst, sl), :], sem.at[0])

    def start_body(r, carry):
        row_copy(r, 0).start(priority=0)
        row_copy(r, 1).start(priority=1)
        return carry

    lax.fori_loop(0, tt, start_body, 0, unroll=8)
    for _ in range(2):
        pltpu.make_async_copy(h_ref, xs_ref.at[pl.ds(0, tt * sl), :], sem.at[0]).wait()


def _dispatch(hs, pos_tiles, ginfo, n_slots, tile, sl):
    m = hs.shape[0] // sl
    tt = pos_tiles.shape[2] // 2
    return pl.pallas_call(
        functools.partial(_dispatch_kernel, tile=tile, sl=sl),
        grid_spec=pltpu.PrefetchScalarGridSpec(
            num_scalar_prefetch=1,
            grid=(m // tt,),
            in_specs=[
                pl.BlockSpec((None, 1, 2 * tt), lambda i, meta: (i, 0, 0), memory_space=pltpu.SMEM),
                pl.BlockSpec((tt * sl, LANES), lambda i, meta: (i, 0)),
            ],
            out_specs=pl.BlockSpec(memory_space=pl.ANY),
            scratch_shapes=[pltpu.VMEM((tile * sl, LANES), BF16), pltpu.SemaphoreType.DMA((2,))],
        ),
        out_shape=jax.ShapeDtypeStruct((n_slots * sl, LANES), BF16),
        compiler_params=_cparams("arbitrary"),
        name="dispatch",
    )(ginfo, pos_tiles, hs)


def _expert_ffn_kernel(te_ref, nu_ref, xs_ref, wg_ref, wu_ref, wd_ref, y_ref, h_ref, acc_ref):
    i, f = pl.program_id(0), pl.program_id(1)
    used = i < nu_ref[0]
    last = pl.num_programs(1) - 1
    tile, d = h_ref.shape

    @pl.when(used & (f == 0))
    def _():
        h_ref[...] = _from_slabs(xs_ref, tile, d).astype(BF16)
        acc_ref[...] = jnp.zeros_like(acc_ref)

    @pl.when(used)
    def _():
        acc_ref[...] += _swiglu_block(h_ref[...], wg_ref[...], wu_ref[...], wd_ref[...])

    @pl.when(used & (f == last))
    def _():
        _to_slabs(y_ref, acc_ref[...])

    @pl.when(jnp.logical_not(used) & (f == last))
    def _():
        y_ref[...] = jnp.zeros_like(y_ref)


def _expert_ffn(xs, tile_e, n_used, wg, wu, wd, tile, sl):
    n_slots = xs.shape[0] // sl
    d = sl * LANES
    ff = wg.shape[2]
    bf = min(FFN_COL_TILE, ff)
    assert n_slots % tile == 0 and ff % bf == 0
    nf = ff // bf
    row = lambda i, f, te, nu: (jnp.minimum(i, nu[0] - 1), 0)
    fblk = lambda i, f, nu: jnp.where(i < nu[0], f, nf - 1)
    return pl.pallas_call(
        _expert_ffn_kernel,
        grid_spec=pltpu.PrefetchScalarGridSpec(
            num_scalar_prefetch=2,
            grid=(n_slots // tile, nf),
            in_specs=[
                pl.BlockSpec((tile * sl, LANES), row),
                pl.BlockSpec((None, d, bf), lambda i, f, te, nu: (te[i], 0, fblk(i, f, nu))),
                pl.BlockSpec((None, d, bf), lambda i, f, te, nu: (te[i], 0, fblk(i, f, nu))),
                pl.BlockSpec((None, bf, d), lambda i, f, te, nu: (te[i], fblk(i, f, nu), 0)),
            ],
            out_specs=pl.BlockSpec((tile * sl, LANES), lambda i, f, te, nu: (i, 0)),
            scratch_shapes=[pltpu.VMEM((tile, d), BF16), pltpu.VMEM((tile, d), F32)],
        ),
        out_shape=jax.ShapeDtypeStruct((n_slots * sl, LANES), BF16),
        compiler_params=_cparams("parallel", "arbitrary"),
        name="expert_ffn",
    )(tile_e, n_used, xs, wg, wu, wd)


def _combine_kernel(pos_ref, nxt_ref, x_ref, route_ref, g_ref, y_ref, o_ref, buf_ref, sem):
    i = pl.program_id(0)
    n = pl.num_programs(0)
    tt, d = x_ref.shape
    sl = d // LANES
    slot = lax.rem(i, 2)

    def gather(p_ref, dst_slot):
        def row_copy(r, c):
            src = pl.multiple_of(p_ref[0, c * tt + r] * sl, sl)
            dst = pl.multiple_of(r * sl, sl)
            return pltpu.make_async_copy(y_ref.at[pl.ds(src, sl), :], buf_ref.at[dst_slot, c, pl.ds(dst, sl), :],
                                         sem.at[dst_slot])

        def start_body(r, carry):
            row_copy(r, 0).start()
            row_copy(r, 1).start()
            return carry

        lax.fori_loop(0, tt, start_body, 0, unroll=8)

    @pl.when(i == 0)
    def _():
        gather(pos_ref, 0)

    @pl.when(i + 1 < n)
    def _():
        gather(nxt_ref, 1 - slot)

    for c in range(2):
        pltpu.make_async_copy(y_ref.at[pl.ds(0, tt * sl), :], buf_ref.at[slot, c], sem.at[slot]).wait()

    rt = route_ref[...]
    moe = (rt[:, 2:3] * _from_slabs(buf_ref.at[slot, 0], tt, d)
           + rt[:, 3:4] * _from_slabs(buf_ref.at[slot, 1], tt, d))
    xo = x_ref[...] + moe
    ms = jnp.mean(xo * xo, axis=-1, keepdims=True)
    o_ref[...] = xo * lax.rsqrt(ms + EPS) * g_ref[...]


def _combine(x, route, g, y, pos_tiles):
    m, d = x.shape
    tt = pos_tiles.shape[2] // 2
    nt = m // tt
    return pl.pallas_call(
        _combine_kernel,
        grid=(nt,),
        in_specs=[
            pl.BlockSpec((None, 1, 2 * tt), lambda i: (i, 0, 0), memory_space=pltpu.SMEM),
            pl.BlockSpec((None, 1, 2 * tt), lambda i: (jnp.minimum(i + 1, nt - 1), 0, 0), memory_space=pltpu.SMEM),
            pl.BlockSpec((tt, d), lambda i: (i, 0)),
            pl.BlockSpec((tt, LANES), lambda i: (i, 0)),
            pl.BlockSpec((1, d), lambda i: (0, 0)),
            pl.BlockSpec(memory_space=pl.ANY),
        ],
        out_specs=pl.BlockSpec((tt, d), lambda i: (i, 0)),
        out_shape=jax.ShapeDtypeStruct((m, d), F32),
        scratch_shapes=[pltpu.VMEM((2, 2, tt * (d // LANES), LANES), BF16), pltpu.SemaphoreType.DMA((2,))],
        compiler_params=_cparams("arbitrary"),
        name="combine",
    )(pos_tiles, pos_tiles, x, route, g, y)


def _moe_ffn_final(x, g_ffn, wr, wg, wu, wd, g_final):
    m, d = x.shape
    tile = min(FFN_ROW_TILE, m)
    n_tiles = (2 * m) // tile + N_EXPERTS
    tt = min(TOK_TILE, m)
    sl = d // LANES
    hs, route = _router(x, g_ffn, wr)
    pos, meta = _route_plan(route, tile, n_tiles)
    pos_tiles = pos[:2].reshape(2, m // tt, tt).transpose(1, 0, 2).reshape(m // tt, 1, 2 * tt)
    xs = _dispatch(hs, pos_tiles, meta[1, :2 * N_EXPERTS + 1], n_tiles * tile, tile, sl)
    y = _expert_ffn(xs, meta[0, :n_tiles], meta[2, :1], wg, wu, wd, tile, sl)
    return _combine(x, route, g_final, y, pos_tiles)


def kernel(x, ln_mix, w_in, w_alpha, b_alpha, gla_norm, w_pool, pool_scale, w_branch_gla, w_branch_pool, w_out,
           ln_ffn, ffn_w_gate, ffn_w_up, ffn_w_down, router_w, exp_w_gate, exp_w_up, exp_w_down, ln_final):
    batch, seq, d = x.shape
    depth = w_in.shape[0]
    dk = w_alpha.shape[2]
    dv = w_branch_gla.shape[1]
    pw = w_branch_pool.shape[1]
    rank = w_alpha.shape[1]
    assert depth == 2 and rank <= LANES
    a_col = 2 * dk + 2 * dv
    m = batch * seq
    xf = x.reshape(m, d)
    row = lambda v: v.reshape(1, -1)
    w_in_t = jnp.swapaxes(w_in, 1, 2)

    assert exp_w_gate.shape[0] == 1 and ffn_w_gate.shape[0] == 1

    def mix(xf, l, z, a1, wp, wbg, wbp, wo):
        w_al = jnp.pad(w_alpha[l], ((0, LANES - rank), (0, 0)))
        og = _gla(z, a1, w_al, row(b_alpha[l]), row(gla_norm[l]), batch, seq, dk, dv)
        yp = _pool(z, wp, row(pool_scale[l]), batch, seq, a_col)
        u = _merge(og, yp, z, wbg, wbp, a_col + pw, a_col + pw + d)
        return _out_proj(u, wo, xf)

    w_main, w_a1 = _inproj_weight(w_in_t, 0, a_col, rank)
    z, a1, (e_gate, f_gate, f_up, f_down) = _norm_inproj(
        xf, row(ln_mix[0]), w_main, w_a1, [(exp_w_gate, 0), (ffn_w_gate, 0), (ffn_w_up, 0), (ffn_w_down, 0)])
    mix0 = [_to_bf16(w, 0) for w in (w_pool, w_branch_gla, w_branch_pool, w_out)]
    xf = mix(xf, 0, z, a1, *mix0)
    xf, (e_down, *mix1) = _ffn(
        xf, row(ln_ffn[0]), f_gate, f_up, f_down,
        [(exp_w_down, 0), (w_pool, 1), (w_branch_gla, 1), (w_branch_pool, 1), (w_out, 1)])
    w_main, w_a1 = _inproj_weight(w_in_t, 1, a_col, rank)
    z, a1, (e_up,) = _norm_inproj(xf, row(ln_mix[1]), w_main, w_a1, [(exp_w_up, 0)])
    xf = mix(xf, 1, z, a1, *mix1)
    wr = jnp.pad(router_w[0], ((0, 0), (0, LANES - N_EXPERTS))).astype(BF16)
    xf = _moe_ffn_final(xf, row(ln_ffn[1]), wr, e_gate, e_up, e_down, ln_final.reshape(1, -1))
    return xf.reshape(batch, seq, d)
```

```python
import functools
import math

import jax
import jax.numpy as jnp
from jax import lax
from jax.experimental import pallas as pl
from jax.experimental.pallas import tpu as pltpu

F32 = jnp.float32
BF16 = jnp.bfloat16
I32 = jnp.int32

EPS = 1e-6
GLA_HEADS = 4
GLA_GATE_TEMP = 16.0
GLA_CHUNK = 64
GLA_HEADS_PER_STEP = 4
POOL_WINDOWS = (2, 4, 8, 16)
N_EXPERTS = 8

LANES = 128
VMEM_LIMIT_BYTES = 56 * 1024 * 1024

ROW_TILE = 1024
OUT_PROJ_ROW_TILE = 512
COL_TILE = 1024
FFN_ROW_TILE = 512
FFN_COL_TILE = 512
SEQ_TILE = 512
TOK_TILE = 256
PLAN_TILE = 512
HALO = 16


def _cparams(*sem):
    return pltpu.CompilerParams(dimension_semantics=sem, vmem_limit_bytes=VMEM_LIMIT_BYTES)


def _dot(a, b):
    return jnp.dot(a, b, preferred_element_type=F32)


def _dot_nt(a, b):
    return lax.dot_general(a, b, (((1,), (1,)), ((), ())), preferred_element_type=F32)


def _split_bf16(x):
    hi = x.astype(BF16)
    lo = (x - hi.astype(F32)).astype(BF16)
    return hi, lo


def _pick(n, target, align=LANES):
    if n <= target:
        return n
    best = None
    for cand in range(align, target + 1, align):
        if n % cand == 0:
            best = cand
    assert best is not None, (n, target, align)
    return best


def _cast_kernel(x_ref, o_ref):
    o_ref[...] = x_ref[...].astype(BF16)


def _to_bf16(w, layer):
    shape = w.shape[1:]
    w3 = w.reshape(w.shape[0], -1, shape[-1])
    _, r, c = w3.shape
    br, bc = _pick(r, 1024, 16), _pick(c, 2048)
    out = pl.pallas_call(
        _cast_kernel,
        grid=(r // br, c // bc),
        in_specs=[pl.BlockSpec((None, br, bc), lambda i, j: (layer, i, j))],
        out_specs=pl.BlockSpec((br, bc), lambda i, j: (i, j)),
        out_shape=jax.ShapeDtypeStruct((r, c), BF16),
        compiler_params=_cparams("parallel", "parallel"),
        name="to_bf16",
    )(w3)
    return out.reshape(shape)


def _inproj_weight_kernel(w_ref, nxt_ref, lr_ref, wm_ref, wa_ref, *, first_shifted, rank):
    j = pl.program_id(0)

    @pl.when(j < first_shifted)
    def _():
        wm_ref[...] = w_ref[...].astype(BF16)

    @pl.when(j >= first_shifted)
    def _():
        wm_ref[...] = jnp.concatenate([w_ref[rank:, :], nxt_ref[...]], axis=0).astype(BF16)

    @pl.when(j == 0)
    def _():
        pad = jnp.zeros((wa_ref.shape[0] - rank, wa_ref.shape[1]), F32)
        wa_ref[...] = jnp.concatenate([lr_ref[...], pad], axis=0).astype(BF16)


def _inproj_weight(w_t, layer, a_col, rank):
    _, n_in, d = w_t.shape
    n = n_in - rank
    br = _pick(math.gcd(n, a_col), 512)
    assert a_col % br == 0 and rank % 8 == 0 and rank < LANES and br % rank == 0
    per = br // rank
    return pl.pallas_call(
        functools.partial(_inproj_weight_kernel, first_shifted=a_col // br, rank=rank),
        grid=(n // br,),
        in_specs=[
            pl.BlockSpec((None, br, d), lambda j: (layer, j, 0)),
            pl.BlockSpec((None, rank, d), lambda j: (layer, (j + 1) * per, 0)),
            pl.BlockSpec((None, rank, d), lambda j: (layer, a_col // rank, 0)),
        ],
        out_specs=[
            pl.BlockSpec((br, d), lambda j: (j, 0)),
            pl.BlockSpec((LANES, d), lambda j: (0, 0)),
        ],
        out_shape=[jax.ShapeDtypeStruct((n, d), BF16), jax.ShapeDtypeStruct((LANES, d), BF16)],
        compiler_params=_cparams("arbitrary"),
        name="inproj_weight",
    )(w_t, w_t, w_t)


class _SidePlan:
    def __init__(self, sides, n_outer, n_inner):
        steps = n_outer * n_inner
        self.arrays, self.in_specs, self.out_specs, self.out_shapes, self.shapes, blocks = [], [], [], [], [], []
        for side, layer in sides:
            side3 = side.reshape(side.shape[0], -1, side.shape[-1])
            r, c = side3.shape[1:]
            rb = next(cand for cand in range(16, r + 1, 16) if r % cand == 0 and r // cand <= steps)
            nblk = r // rb
            idx = functools.partial(lambda i, j, last: jnp.minimum(i * n_inner + j, last), last=nblk - 1)
            self.arrays.append(side3)
            self.in_specs.append(pl.BlockSpec(
                (None, rb, c), functools.partial(lambda i, j, idx, layer: (layer, idx(i, j), 0), idx=idx, layer=layer)))
            self.out_specs.append(pl.BlockSpec((rb, c), functools.partial(lambda i, j, idx: (idx(i, j), 0), idx=idx)))
            self.out_shapes.append(jax.ShapeDtypeStruct((r, c), BF16))
            self.shapes.append(side.shape[1:])
            blocks.append(nblk)
        self.blocks = tuple(blocks)

    def finish(self, outs):
        return [o.reshape(s) for o, s in zip(outs, self.shapes)]


def _side_casts(blocks, side_refs, side_out_refs):
    step = pl.program_id(0) * pl.num_programs(1) + pl.program_id(1)
    for n_blocks, src, dst in zip(blocks, side_refs, side_out_refs):
        @pl.when(step < n_blocks)
        def _():
            dst[...] = src[...].astype(BF16)


def _norm_inproj_kernel(x_ref, g_ref, w_ref, wa_ref, *rest, rows, side_blocks):
    k = len(side_blocks)
    z_ref, a1_ref = rest[k:k + 2]
    h_ref = rest[2 * k + 2]
    _side_casts(side_blocks, rest[:k], rest[k + 2:2 * k + 2])

    @pl.when(pl.program_id(1) == 0)
    def _():
        def body(c, carry):
            r = pl.multiple_of(c * rows, rows)
            xf = x_ref[pl.ds(r, rows), :]
            ms = jnp.mean(xf * xf, axis=-1, keepdims=True)
            h_ref[pl.ds(r, rows), :] = (xf * lax.rsqrt(ms + EPS) * g_ref[...]).astype(BF16)
            return carry
        lax.fori_loop(0, x_ref.shape[0] // rows, body, 0)
        a1_ref[...] = _dot_nt(h_ref[...], wa_ref[...])

    z_ref[...] = _dot_nt(h_ref[...], w_ref[...]).astype(BF16)


def _norm_inproj(x, g, w_t, wa_t, sides):
    m, d = x.shape
    n = w_t.shape[0]
    bm, bn = min(ROW_TILE, m), _pick(n, COL_TILE)
    assert m % bm == 0 and n % bn == 0
    ni, nj = m // bm, n // bn
    plan = _SidePlan(sides, ni, nj)
    z, a1, *side_out = pl.pallas_call(
        functools.partial(_norm_inproj_kernel, rows=128, side_blocks=plan.blocks),
        grid=(ni, nj),
        in_specs=[
            pl.BlockSpec((bm, d), lambda i, j: (i, 0)),
            pl.BlockSpec((1, d), lambda i, j: (0, 0)),
            pl.BlockSpec((bn, d), lambda i, j: (j, 0)),
            pl.BlockSpec((LANES, d), lambda i, j: (0, 0)),
        ] + plan.in_specs,
        out_specs=[
            pl.BlockSpec((bm, bn), lambda i, j: (i, j)),
            pl.BlockSpec((bm, LANES), lambda i, j: (i, 0)),
        ] + plan.out_specs,
        out_shape=[jax.ShapeDtypeStruct((m, n), BF16), jax.ShapeDtypeStruct((m, LANES), F32)] + plan.out_shapes,
        scratch_shapes=[pltpu.VMEM((bm, d), BF16)],
        compiler_params=_cparams("arbitrary", "arbitrary"),
        name="norm_inproj",
    )(x, g, w_t, wa_t, *plan.arrays)
    return z, a1, plan.finish(side_out)


def _log_sigmoid(x):
    return jnp.minimum(x, 0.0) - jnp.log(1.0 + jnp.exp(-jnp.abs(x)))


def _gla_kernel(q_ref, k_ref, v_ref, r_ref, a1_ref, wal_ref, bal_ref, gn_ref, tri_ref, o_ref, st_ref, *,
                chunk, heads):
    @pl.when(pl.program_id(2) == 0)
    def _():
        st_ref[...] = jnp.zeros_like(st_ref)

    hk = q_ref.shape[1] // heads
    hv = v_ref.shape[1] // heads
    a1 = _split_bf16(a1_ref[...])
    for h in range(heads):
        ks, vs = slice(h * hk, (h + 1) * hk), slice(h * hv, (h + 1) * hv)
        o, st_new = _gla_head(q_ref[:, ks], k_ref[:, ks], v_ref[:, vs], r_ref[:, vs], a1, wal_ref[:, ks],
                              bal_ref[:, ks], gn_ref[...], tri_ref[...], st_ref[h], chunk)
        o_ref[:, vs] = o
        st_ref[h] = st_new


def _gla_head(q, k, v, r, a1, wal, bal, gn, tri, st, chunk):
    t, dk = q.shape
    nc = t // chunk
    pair = 2 * chunk
    shift = chunk.bit_length() - 1
    nt_dot = _dot_nt
    rows = lambda x, c0, c1: x[c0 * chunk:c1 * chunk, :]

    a_hi, a_lo = a1
    w_hi, w_lo = _split_bf16(wal)
    xg = _dot(a_hi, w_hi) + _dot(a_hi, w_lo) + _dot(a_lo, w_hi) + bal
    la = _log_sigmoid(xg) * (1.0 / GLA_GATE_TEMP)

    la_hi, la_lo = _split_bf16(la)
    b = _dot(tri, la_hi) + _dot(tri, la_lo)

    bl = [b[(c + 1) * chunk - 1:(c + 1) * chunk, :] for c in range(nc)]
    pre = [jnp.zeros_like(bl[0])]
    for c in range(nc):
        pre.append(pre[c] + bl[c])

    q = q.astype(F32)
    k = k.astype(F32)
    q_t = q * jnp.exp(b) * (dk ** -0.5)
    k_t = (k * jnp.exp(-b)).astype(BF16)
    qc = [rows(q_t, c, c + 1) for c in range(nc)]
    kec = [rows(k, c, c + 1) * jnp.exp(bl[c] - rows(b, c, c + 1)) for c in range(nc)]
    q_tb = q_t.astype(BF16)
    k_eb = jnp.concatenate(kec, axis=0).astype(BF16)

    def q_from(mid, hi):
        return jnp.concatenate([qc[c] if c == mid else qc[c] * jnp.exp(pre[c] - pre[mid])
                                for c in range(mid, hi)], axis=0).astype(BF16)

    def k_upto(lo, mid):
        return jnp.concatenate([kec[m] if m == mid - 1 else kec[m] * jnp.exp(pre[mid] - pre[m + 1])
                                for m in range(lo, mid)], axis=0).astype(BF16)

    nb = t // pair
    blocks = [[None] * nb for _ in range(nb)]
    r2 = lax.broadcasted_iota(I32, (pair, pair), 0)
    c2 = lax.broadcasted_iota(I32, (pair, pair), 1)
    same = ((r2 >> shift) == (c2 >> shift)) & (r2 >= c2)
    cross = (r2 >= chunk) & (c2 < chunk)
    for p in range(nb):
        qp = rows(q_tb, 2 * p, 2 * p + 2)
        in_chunk = nt_dot(qp, rows(k_t, 2 * p, 2 * p + 2))
        next_chunk = nt_dot(qp, rows(k_eb, 2 * p, 2 * p + 2))
        blocks[p][p] = jnp.where(same, in_chunk, jnp.where(cross, next_chunk, 0.0))
    g = 4
    while g <= nc:
        half = g // 2
        for grp in range(nc // g):
            lo, mid, hi = grp * g, grp * g + half, (grp + 1) * g
            x = nt_dot(q_from(mid, hi), k_upto(lo, mid))
            hb = half // 2
            for i in range(hb):
                for j in range(hb):
                    blocks[mid // 2 + i][lo // 2 + j] = x[i * pair:(i + 1) * pair, j * pair:(j + 1) * pair]
        g *= 2
    zero_blk = jnp.zeros((pair, pair), F32)
    att = jnp.concatenate(
        [jnp.concatenate([zero_blk if blk is None else blk for blk in brow], axis=1) for brow in blocks], axis=0)

    o = _dot(att.astype(BF16), v) + nt_dot(q_from(0, nc), st.astype(BF16))
    upd = lax.dot_general(v, k_upto(0, nc), (((0,), (0,)), ((), ())), preferred_element_type=F32)
    st_new = st * jnp.exp(pre[nc]) + upd

    o = o * lax.rsqrt(jnp.mean(o * o, axis=-1, keepdims=True) + EPS) * gn
    r = r.astype(F32)
    return (o * (r * jax.nn.sigmoid(r))).astype(BF16), st_new


def _gla(z, a1, w_alpha, b_alpha, gla_norm, batch, seq, dk_total, dv_total):
    m = z.shape[0]
    hk, hv = dk_total // GLA_HEADS, dv_total // GLA_HEADS
    t = min(SEQ_TILE, seq)
    assert seq % t == 0 and t % GLA_CHUNK == 0 and hk % LANES == 0 and hv % LANES == 0
    ns = seq // t
    nc = t // GLA_CHUNK
    assert nc >= 2 and nc & (nc - 1) == 0 and GLA_HEADS % GLA_HEADS_PER_STEP == 0
    hps = GLA_HEADS_PER_STEP
    gk, gv = hps * hk, hps * hv
    k_off = dk_total // gk
    v_off = (2 * dk_total) // gv
    r_off = (2 * dk_total + dv_total) // gv
    idx = jnp.arange(t, dtype=I32)
    tri = ((idx[:, None] // GLA_CHUNK == idx[None, :] // GLA_CHUNK) & (idx[:, None] >= idx[None, :])).astype(BF16)
    rowi = lambda b, h, s: b * ns + s
    return pl.pallas_call(
        functools.partial(_gla_kernel, chunk=GLA_CHUNK, heads=hps),
        grid=(batch, GLA_HEADS // hps, ns),
        in_specs=[
            pl.BlockSpec((t, gk), lambda b, h, s: (rowi(b, h, s), h)),
            pl.BlockSpec((t, gk), lambda b, h, s: (rowi(b, h, s), k_off + h)),
            pl.BlockSpec((t, gv), lambda b, h, s: (rowi(b, h, s), v_off + h)),
            pl.BlockSpec((t, gv), lambda b, h, s: (rowi(b, h, s), r_off + h)),
            pl.BlockSpec((t, LANES), lambda b, h, s: (rowi(b, h, s), 0)),
            pl.BlockSpec((LANES, gk), lambda b, h, s: (0, h)),
            pl.BlockSpec((1, gk), lambda b, h, s: (0, h)),
            pl.BlockSpec((1, hv), lambda b, h, s: (0, 0)),
            pl.BlockSpec((t, t), lambda b, h, s: (0, 0)),
        ],
        out_specs=pl.BlockSpec((t, gv), lambda b, h, s: (rowi(b, h, s), h)),
        out_shape=jax.ShapeDtypeStruct((m, dv_total), BF16),
        scratch_shapes=[pltpu.VMEM((hps, hv, hk), F32)],
        compiler_params=_cparams("parallel", "parallel", "arbitrary"),
        name="gla",
    )(z, z, z, z, a1, w_alpha, b_alpha, gla_norm, tri)


def _pool_kernel(p_ref, ph_ref, wp_ref, sc_ref, y_ref, *, windows):
    s = pl.program_id(1)
    t = p_ref.shape[0]
    halo = ph_ref.shape[0]
    gw = wp_ref.shape[1]
    dist = lax.broadcasted_iota(I32, (t, t), 0) - lax.broadcasted_iota(I32, (t, t), 1)
    dist_h = lax.broadcasted_iota(I32, (t, halo), 0) + halo - lax.broadcasted_iota(I32, (t, halo), 1)
    pos = s * t + lax.broadcasted_iota(I32, (t, 1), 0)
    for g, w in enumerate(windows):
        cs = slice(g * gw, (g + 1) * gw)
        pc = p_ref[:, cs]
        band = jnp.where((dist >= 0) & (dist < w), 1.0, 0.0).astype(BF16)
        band_h = jnp.where(dist_h < w, 1.0, 0.0).astype(BF16)
        tot = _dot(band, pc) + jnp.where(s > 0, _dot(band_h, ph_ref[:, cs]), 0.0)
        cnt = jnp.minimum(pos + 1, w).astype(F32)
        mixed = tot / cnt - pc.astype(F32)
        y = _dot(mixed.astype(BF16), wp_ref[g]) * sc_ref[:, cs]
        y_ref[:, cs] = y.astype(BF16)


def _pool(z, w_pool, pool_scale, batch, seq, p_col0):
    m = z.shape[0]
    groups, gw, _ = w_pool.shape
    pw = groups * gw
    t = min(SEQ_TILE, seq)
    assert seq % t == 0 and t % HALO == 0 and p_col0 % pw == 0 and gw % LANES == 0
    ns = seq // t
    pc = p_col0 // pw
    hb = t // HALO
    return pl.pallas_call(
        functools.partial(_pool_kernel, windows=POOL_WINDOWS),
        grid=(batch, ns),
        in_specs=[
            pl.BlockSpec((t, pw), lambda b, s: (b * ns + s, pc)),
            pl.BlockSpec((HALO, pw), lambda b, s: (jnp.maximum((b * ns + s) * hb - 1, 0), pc)),
            pl.BlockSpec((groups, gw, gw), lambda b, s: (0, 0, 0)),
            pl.BlockSpec((1, pw), lambda b, s: (0, 0)),
        ],
        out_specs=pl.BlockSpec((t, pw), lambda b, s: (b * ns + s, 0)),
        out_shape=jax.ShapeDtypeStruct((m, pw), BF16),
        compiler_params=_cparams("parallel", "arbitrary"),
        name="pool",
    )(z, z, w_pool, pool_scale)


def _merge_kernel(og_ref, yp_ref, ga_ref, gb_ref, wa_ref, wb_ref, u_ref):
    ya = _dot(og_ref[...], wa_ref[...])
    yb = _dot(yp_ref[...], wb_ref[...])
    ga = jax.nn.sigmoid(ga_ref[...].astype(F32))
    gb = jax.nn.sigmoid(gb_ref[...].astype(F32))
    u_ref[...] = (ga * ya + gb * yb).astype(BF16)


def _merge(og, yp, z, w_a, w_b, ga_col0, gb_col0):
    m, dv = og.shape
    pw = yp.shape[1]
    d = w_a.shape[1]
    bm, bn = min(ROW_TILE, m), _pick(math.gcd(d, ga_col0, gb_col0), COL_TILE)
    assert m % bm == 0 and d % bn == 0 and ga_col0 % bn == 0 and gb_col0 % bn == 0
    ga0, gb0 = ga_col0 // bn, gb_col0 // bn
    return pl.pallas_call(
        _merge_kernel,
        grid=(m // bm, d // bn),
        in_specs=[
            pl.BlockSpec((bm, dv), lambda i, j: (i, 0)),
            pl.BlockSpec((bm, pw), lambda i, j: (i, 0)),
            pl.BlockSpec((bm, bn), lambda i, j: (i, ga0 + j)),
            pl.BlockSpec((bm, bn), lambda i, j: (i, gb0 + j)),
            pl.BlockSpec((dv, bn), lambda i, j: (0, j)),
            pl.BlockSpec((pw, bn), lambda i, j: (0, j)),
        ],
        out_specs=pl.BlockSpec((bm, bn), lambda i, j: (i, j)),
        out_shape=jax.ShapeDtypeStruct((m, d), BF16),
        compiler_params=_cparams("parallel", "arbitrary"),
        name="merge",
    )(og, yp, z, z, w_a, w_b)


def _out_proj_kernel(u_ref, w_ref, x_ref, o_ref):
    o_ref[...] = x_ref[...] + _dot(u_ref[...], w_ref[...])


def _out_proj(u, w, x):
    m, d = x.shape
    bm, bn = min(OUT_PROJ_ROW_TILE, m), _pick(d, 2 * COL_TILE)
    assert m % bm == 0 and d % bn == 0
    return pl.pallas_call(
        _out_proj_kernel,
        grid=(m // bm, d // bn),
        in_specs=[
            pl.BlockSpec((bm, u.shape[1]), lambda i, j: (i, 0)),
            pl.BlockSpec((u.shape[1], bn), lambda i, j: (0, j)),
            pl.BlockSpec((bm, bn), lambda i, j: (i, j)),
        ],
        out_specs=pl.BlockSpec((bm, bn), lambda i, j: (i, j)),
        out_shape=jax.ShapeDtypeStruct((m, d), F32),
        compiler_params=_cparams("parallel", "arbitrary"),
        name="out_proj",
    )(u, w, x)


def _rmsnorm_rows(x_ref, g_ref, h_ref, rows):
    def body(c, carry):
        r = pl.multiple_of(c * rows, rows)
        xf = x_ref[pl.ds(r, rows), :]
        ms = jnp.mean(xf * xf, axis=-1, keepdims=True)
        h_ref[pl.ds(r, rows), :] = (xf * lax.rsqrt(ms + EPS) * g_ref[...]).astype(BF16)
        return carry
    lax.fori_loop(0, x_ref.shape[0] // rows, body, 0)


def _swiglu_block(h, wg, wu, wd):
    g = _dot(h, wg)
    u = _dot(h, wu)
    a = (g * jax.nn.sigmoid(g) * u).astype(BF16)
    return _dot(a, wd)


def _ffn_kernel(x_ref, g_ref, wg_ref, wu_ref, wd_ref, *rest, side_blocks):
    f = pl.program_id(1)
    k = len(side_blocks)
    o_ref = rest[k]
    h_ref = rest[2 * k + 1]
    _side_casts(side_blocks, rest[:k], rest[k + 1:2 * k + 1])

    @pl.when(f == 0)
    def _():
        _rmsnorm_rows(x_ref, g_ref, h_ref, 128)
        o_ref[...] = x_ref[...]

    o_ref[...] += _swiglu_block(h_ref[...], wg_ref[...], wu_ref[...], wd_ref[...])


def _ffn(x, g, wg, wu, wd, sides):
    m, d = x.shape
    ff = wg.shape[1]
    bm, bf = min(FFN_ROW_TILE, m), min(FFN_COL_TILE, ff)
    assert m % bm == 0 and ff % bf == 0
    ni, nf = m // bm, ff // bf
    plan = _SidePlan(sides, ni, nf)
    out, *side_out = pl.pallas_call(
        functools.partial(_ffn_kernel, side_blocks=plan.blocks),
        grid=(ni, nf),
        in_specs=[
            pl.BlockSpec((bm, d), lambda i, f: (i, 0)),
            pl.BlockSpec((1, d), lambda i, f: (0, 0)),
            pl.BlockSpec((d, bf), lambda i, f: (0, f)),
            pl.BlockSpec((d, bf), lambda i, f: (0, f)),
            pl.BlockSpec((bf, d), lambda i, f: (f, 0)),
        ] + plan.in_specs,
        out_specs=[pl.BlockSpec((bm, d), lambda i, f: (i, 0))] + plan.out_specs,
        out_shape=[jax.ShapeDtypeStruct((m, d), F32)] + plan.out_shapes,
        scratch_shapes=[pltpu.VMEM((bm, d), BF16)],
        compiler_params=_cparams("arbitrary", "arbitrary"),
        name="ffn",
    )(x, g, wg, wu, wd, *plan.arrays)
    return out, plan.finish(side_out)


def _to_slabs(slab_ref, x):
    rows, d = x.shape
    slab_ref[...] = x.astype(BF16).reshape(rows * (d // LANES), LANES)


def _from_slabs(slab_ref, rows, d):
    return slab_ref[...].reshape(rows, d)


def _router_kernel(x_ref, g_ref, wr_ref, hs_ref, route_ref, hb_ref):
    _rmsnorm_rows(x_ref, g_ref, hb_ref, 128)
    hb = hb_ref[...]
    _to_slabs(hs_ref, hb)
    logits = _dot(hb, wr_ref[...])
    lane = lax.broadcasted_iota(I32, logits.shape, 1)
    lane_f = lane.astype(F32)
    neg = jnp.float32(-jnp.inf)
    l1 = jnp.where(lane < N_EXPERTS, logits, neg)
    m1 = jnp.max(l1, axis=-1, keepdims=True)
    i1 = jnp.min(jnp.where(l1 == m1, lane_f, float(LANES)), axis=-1, keepdims=True)
    l2 = jnp.where(lane_f == i1, neg, l1)
    m2 = jnp.max(l2, axis=-1, keepdims=True)
    i2 = jnp.min(jnp.where(l2 == m2, lane_f, float(LANES)), axis=-1, keepdims=True)
    e = jnp.exp(m2 - m1)
    den = 1.0 + e
    route = jnp.where(lane == 0, i1, 0.0)
    route = jnp.where(lane == 1, i2, route)
    route = jnp.where(lane == 2, 1.0 / den, route)
    route = jnp.where(lane == 3, e / den, route)
    route_ref[...] = route


def _router(x, g, wr):
    m, d = x.shape
    bm = min(FFN_ROW_TILE, m)
    sl = d // LANES
    assert m % bm == 0 and sl % 8 == 0
    return pl.pallas_call(
        _router_kernel,
        grid=(m // bm,),
        in_specs=[
            pl.BlockSpec((bm, d), lambda i: (i, 0)),
            pl.BlockSpec((1, d), lambda i: (0, 0)),
            pl.BlockSpec((d, LANES), lambda i: (0, 0)),
        ],
        out_specs=[
            pl.BlockSpec((bm * sl, LANES), lambda i: (i, 0)),
            pl.BlockSpec((bm, LANES), lambda i: (i, 0)),
        ],
        out_shape=[jax.ShapeDtypeStruct((m * sl, LANES), BF16), jax.ShapeDtypeStruct((m, LANES), F32)],
        scratch_shapes=[pltpu.VMEM((bm, d), BF16)],
        compiler_params=_cparams("parallel"),
        name="router",
    )(x, g, wr)


def _route_plan_kernel(route_ref, pos_ref, meta_ref, rank_ref, *, tile, blk):
    m = route_ref.shape[0]
    nb = m // blk
    lane_b = lax.broadcasted_iota(I32, (blk, LANES), 1)
    lane_r = lax.broadcasted_iota(I32, (1, LANES), 1)
    ne = N_EXPERTS

    def onehot(r):
        rt = route_ref[pl.ds(r, blk), :]
        i1 = rt[:, 0:1].astype(I32)
        i2 = rt[:, 1:2].astype(I32)
        return jnp.where((lane_b == i1) | (lane_b == i2 + ne), 1.0, 0.0)

    strict = jnp.where(lax.broadcasted_iota(I32, (blk, blk), 0) > lax.broadcasted_iota(I32, (blk, blk), 1),
                       1.0, 0.0).astype(BF16)

    def rank_body(c, carry):
        r = pl.multiple_of(c * blk, blk)
        oh = onehot(r)
        rank_ref[pl.ds(r, blk), :] = _dot(strict, oh.astype(BF16)) + carry
        return carry + jnp.sum(oh, axis=0, keepdims=True)

    cnt12 = lax.fori_loop(0, nb, rank_body, jnp.zeros((1, LANES), F32))

    def lane_val(row, l):
        return jnp.sum(jnp.where(lane_r == l, row, 0.0), axis=-1, keepdims=True)

    cnt = jnp.zeros((1, LANES), F32)
    for e in range(ne):
        cnt = jnp.where(lane_r == e, lane_val(cnt12, e) + lane_val(cnt12, e + ne), cnt)
    tiles = jnp.floor((cnt + (tile - 1)) * (1.0 / tile))
    start = jnp.zeros((1, LANES), F32)
    for e in range(1, ne):
        start = start + jnp.where(lane_r >= e, lane_val(tiles, e - 1), 0.0)
    start = jnp.where(lane_r < ne, start * tile, 0.0)
    end = start + tiles * tile
    base = start
    for e in range(ne):
        base = jnp.where(lane_r == e + ne, lane_val(start, e) + lane_val(cnt12, e), base)

    def pos_body(c, carry):
        r = pl.multiple_of(c * blk, blk)
        slot = onehot(r) * (rank_ref[pl.ds(r, blk), :] + base)
        p1 = jnp.sum(jnp.where(lane_b < ne, slot, 0.0), axis=-1, keepdims=True)
        p2 = jnp.sum(jnp.where(lane_b >= ne, slot, 0.0), axis=-1, keepdims=True)
        both = jnp.where(lane_b == 0, p1, jnp.where(lane_b == 1, p2, 0.0))
        pos_ref[:, pl.ds(r, blk)] = jnp.transpose(both)[0:8, :].astype(I32)
        return carry

    lax.fori_loop(0, nb, pos_body, 0)

    n_used = lane_val(end, ne - 1) * (1.0 / tile)
    tile_start = lane_r.astype(F32) * tile
    tile_e = jnp.zeros((1, LANES), F32)
    last_e = jnp.zeros((1, 1), F32)
    for e in range(ne):
        tile_e = tile_e + jnp.where(lane_val(end, e) <= tile_start, 1.0, 0.0)
        last_e = jnp.where(lane_val(cnt, e) > 0, float(e), last_e)
    tile_e = jnp.where(lane_r.astype(F32) < n_used, tile_e, last_e)
    info = jnp.where(lane_r < ne, end, jnp.where(lane_r == 2 * ne, n_used, 0.0))
    for e in range(ne):
        info = jnp.where(lane_r == e + ne, lane_val(cnt, e), info)
    row8 = lax.broadcasted_iota(I32, (8, LANES), 0)
    meta = jnp.where(row8 == 0, tile_e, jnp.where(row8 == 1, info, jnp.where(row8 == 2, n_used, 0.0)))
    meta_ref[...] = meta.astype(I32)


def _route_plan(route, tile, n_tiles):
    m = route.shape[0]
    blk = min(PLAN_TILE, m)
    assert m % blk == 0 and n_tiles <= LANES
    return pl.pallas_call(
        functools.partial(_route_plan_kernel, tile=tile, blk=blk),
        out_shape=[jax.ShapeDtypeStruct((8, m), I32), jax.ShapeDtypeStruct((8, LANES), I32)],
        scratch_shapes=[pltpu.VMEM((m, LANES), F32)],
        compiler_params=pltpu.CompilerParams(vmem_limit_bytes=VMEM_LIMIT_BYTES),
        name="route_plan",
    )(route)


def _dispatch_kernel(meta_ref, pos_ref, h_ref, xs_ref, zero_ref, sem, *, tile, sl):
    i = pl.program_id(0)
    tt = h_ref.shape[0] // sl
    n_tiles = xs_ref.shape[0] // (tile * sl)

    def zero_copy(e):
        first = pl.multiple_of((meta_ref[e] - tile) * sl, tile * sl)
        return pltpu.make_async_copy(zero_ref, xs_ref.at[pl.ds(first, tile * sl), :], sem.at[1])

    def tail_copy(j):
        return pltpu.make_async_copy(zero_ref, xs_ref.at[pl.ds(j * tile * sl, tile * sl), :], sem.at[1])

    @pl.when(i == 0)
    def _():
        zero_ref[...] = jnp.zeros_like(zero_ref)
        n_used = meta_ref[2 * N_EXPERTS]
        for e in range(N_EXPERTS):
            @pl.when(meta_ref[N_EXPERTS + e] > 0)
            def _():
                zero_copy(e).start()
        for j in range(n_tiles - N_EXPERTS, n_tiles):
            @pl.when(j >= n_used)
            def _():
                tail_copy(j).start()
        for e in range(N_EXPERTS):
            @pl.when(meta_ref[N_EXPERTS + e] > 0)
            def _():
                zero_copy(e).wait()
        for j in range(n_tiles - N_EXPERTS, n_tiles):
            @pl.when(j >= n_used)
            def _():
                tail_copy(j).wait()

    def row_copy(r, c):
        src = pl.multiple_of(r * sl, sl)
        dst = pl.multiple_of(pos_ref[0, c * tt + r] * sl, sl)
        return pltpu.make_async_copy(h_ref.at[pl.ds(src, sl), :], xs_ref.at[pl.ds(dst, sl), :], sem.at[0])

    def start_body(r, carry):
        row_copy(r, 0).start(priority=0)
        row_copy(r, 1).start(priority=1)
        return carry

    lax.fori_loop(0, tt, start_body, 0, unroll=8)
    for _ in range(2):
        pltpu.make_async_copy(h_ref, xs_ref.at[pl.ds(0, tt * sl), :], sem.at[0]).wait()


def _dispatch(hs, pos_tiles, ginfo, n_slots, tile, sl):
    m = hs.shape[0] // sl
    tt = pos_tiles.shape[2] // 2
    return pl.pallas_call(
        functools.partial(_dispatch_kernel, tile=tile, sl=sl),
        grid_spec=pltpu.PrefetchScalarGridSpec(
            num_scalar_prefetch=1,
            grid=(m // tt,),
            in_specs=[
                pl.BlockSpec((None, 1, 2 * tt), lambda i, meta: (i, 0, 0), memory_space=pltpu.SMEM),
                pl.BlockSpec((tt * sl, LANES), lambda i, meta: (i, 0)),
            ],
            out_specs=pl.BlockSpec(memory_space=pl.ANY),
            scratch_shapes=[pltpu.VMEM((tile * sl, LANES), BF16), pltpu.SemaphoreType.DMA((2,))],
        ),
        out_shape=jax.ShapeDtypeStruct((n_slots * sl, LANES), BF16),
        compiler_params=_cparams("arbitrary"),
        name="dispatch",
    )(ginfo, pos_tiles, hs)


def _expert_ffn_kernel(te_ref, nu_ref, xs_ref, wg_ref, wu_ref, wd_ref, y_ref, h_ref, acc_ref):
    i, f = pl.program_id(0), pl.program_id(1)
    used = i < nu_ref[0]
    last = pl.num_programs(1) - 1
    tile, d = h_ref.shape

    @pl.when(used & (f == 0))
    def _():
        h_ref[...] = _from_slabs(xs_ref, tile, d).astype(BF16)
        acc_ref[...] = jnp.zeros_like(acc_ref)

    @pl.when(used)
    def _():
        acc_ref[...] += _swiglu_block(h_ref[...], wg_ref[...], wu_ref[...], wd_ref[...])

    @pl.when(used & (f == last))
    def _():
        _to_slabs(y_ref, acc_ref[...])

    @pl.when(jnp.logical_not(used) & (f == last))
    def _():
        y_ref[...] = jnp.zeros_like(y_ref)


def _expert_ffn(xs, tile_e, n_used, wg, wu, wd, tile, sl):
    n_slots = xs.shape[0] // sl
    d = sl * LANES
    ff = wg.shape[2]
    bf = min(FFN_COL_TILE, ff)
    assert n_slots % tile == 0 and ff % bf == 0
    nf = ff // bf
    row = lambda i, f, te, nu: (jnp.minimum(i, nu[0] - 1), 0)
    fblk = lambda i, f, nu: jnp.where(i < nu[0], f, nf - 1)
    return pl.pallas_call(
        _expert_ffn_kernel,
        grid_spec=pltpu.PrefetchScalarGridSpec(
            num_scalar_prefetch=2,
            grid=(n_slots // tile, nf),
            in_specs=[
                pl.BlockSpec((tile * sl, LANES), row),
                pl.BlockSpec((None, d, bf), lambda i, f, te, nu: (te[i], 0, fblk(i, f, nu))),
                pl.BlockSpec((None, d, bf), lambda i, f, te, nu: (te[i], 0, fblk(i, f, nu))),
                pl.BlockSpec((None, bf, d), lambda i, f, te, nu: (te[i], fblk(i, f, nu), 0)),
            ],
            out_specs=pl.BlockSpec((tile * sl, LANES), lambda i, f, te, nu: (i, 0)),
            scratch_shapes=[pltpu.VMEM((tile, d), BF16), pltpu.VMEM((tile, d), F32)],
        ),
        out_shape=jax.ShapeDtypeStruct((n_slots * sl, LANES), BF16),
        compiler_params=_cparams("parallel", "arbitrary"),
        name="expert_ffn",
    )(tile_e, n_used, xs, wg, wu, wd)


def _combine_kernel(pos_ref, nxt_ref, x_ref, route_ref, g_ref, y_ref, o_ref, buf_ref, sem):
    i = pl.program_id(0)
    n = pl.num_programs(0)
    tt, d = x_ref.shape
    sl = d // LANES
    slot = lax.rem(i, 2)

    def gather(p_ref, dst_slot):
        def row_copy(r, c):
            src = pl.multiple_of(p_ref[0, c * tt + r] * sl, sl)
            dst = pl.multiple_of(r * sl, sl)
            return pltpu.make_async_copy(y_ref.at[pl.ds(src, sl), :], buf_ref.at[dst_slot, c, pl.ds(dst, sl), :],
                                         sem.at[dst_slot])

        def start_body(r, carry):
            row_copy(r, 0).start()
            row_copy(r, 1).start()
            return carry

        lax.fori_loop(0, tt, start_body, 0, unroll=8)

    @pl.when(i == 0)
    def _():
        gather(pos_ref, 0)

    @pl.when(i + 1 < n)
    def _():
        gather(nxt_ref, 1 - slot)

    for c in range(2):
        pltpu.make_async_copy(y_ref.at[pl.ds(0, tt * sl), :], buf_ref.at[slot, c], sem.at[slot]).wait()

    rt = route_ref[...]
    moe = (rt[:, 2:3] * _from_slabs(buf_ref.at[slot, 0], tt, d)
           + rt[:, 3:4] * _from_slabs(buf_ref.at[slot, 1], tt, d))
    xo = x_ref[...] + moe
    ms = jnp.mean(xo * xo, axis=-1, keepdims=True)
    o_ref[...] = xo * lax.rsqrt(ms + EPS) * g_ref[...]


def _combine(x, route, g, y, pos_tiles):
    m, d = x.shape
    tt = pos_tiles.shape[2] // 2
    nt = m // tt
    return pl.pallas_call(
        _combine_kernel,
        grid=(nt,),
        in_specs=[
            pl.BlockSpec((None, 1, 2 * tt), lambda i: (i, 0, 0), memory_space=pltpu.SMEM),
            pl.BlockSpec((None, 1, 2 * tt), lambda i: (jnp.minimum(i + 1, nt - 1), 0, 0), memory_space=pltpu.SMEM),
            pl.BlockSpec((tt, d), lambda i: (i, 0)),
            pl.BlockSpec((tt, LANES), lambda i: (i, 0)),
            pl.BlockSpec((1, d), lambda i: (0, 0)),
            pl.BlockSpec(memory_space=pl.ANY),
        ],
        out_specs=pl.BlockSpec((tt, d), lambda i: (i, 0)),
        out_shape=jax.ShapeDtypeStruct((m, d), F32),
        scratch_shapes=[pltpu.VMEM((2, 2, tt * (d // LANES), LANES), BF16), pltpu.SemaphoreType.DMA((2,))],
        compiler_params=_cparams("arbitrary"),
        name="combine",
    )(pos_tiles, pos_tiles, x, route, g, y)


def _moe_ffn_final(x, g_ffn, wr, wg, wu, wd, g_final):
    m, d = x.shape
    tile = min(FFN_ROW_TILE, m)
    n_tiles = (2 * m) // tile + N_EXPERTS
    tt = min(TOK_TILE, m)
    sl = d // LANES
    hs, route = _router(x, g_ffn, wr)
    pos, meta = _route_plan(route, tile, n_tiles)
    pos_tiles = pos[:2].reshape(2, m // tt, tt).transpose(1, 0, 2).reshape(m // tt, 1, 2 * tt)
    xs = _dispatch(hs, pos_tiles, meta[1, :2 * N_EXPERTS + 1], n_tiles * tile, tile, sl)
    y = _expert_ffn(xs, meta[0, :n_tiles], meta[2, :1], wg, wu, wd, tile, sl)
    return _combine(x, route, g_final, y, pos_tiles)


def kernel(x, ln_mix, w_in, w_alpha, b_alpha, gla_norm, w_pool, pool_scale, w_branch_gla, w_branch_pool, w_out,
           ln_ffn, ffn_w_gate, ffn_w_up, ffn_w_down, router_w, exp_w_gate, exp_w_up, exp_w_down, ln_final):
    batch, seq, d = x.shape
    depth = w_in.shape[0]
    dk = w_alpha.shape[2]
    dv = w_branch_gla.shape[1]
    pw = w_branch_pool.shape[1]
    rank = w_alpha.shape[1]
    assert depth == 2 and rank <= LANES
    a_col = 2 * dk + 2 * dv
    m = batch * seq
    xf = x.reshape(m, d)
    row = lambda v: v.reshape(1, -1)
    w_in_t = jnp.swapaxes(w_in, 1, 2)

    assert exp_w_gate.shape[0] == 1 and ffn_w_gate.shape[0] == 1

    def mix(xf, l, z, a1, wp, wbg, wbp, wo):
        w_al = jnp.pad(w_alpha[l], ((0, LANES - rank), (0, 0)))
        og = _gla(z, a1, w_al, row(b_alpha[l]), row(gla_norm[l]), batch, seq, dk, dv)
        yp = _pool(z, wp, row(pool_scale[l]), batch, seq, a_col)
        u = _merge(og, yp, z, wbg, wbp, a_col + pw, a_col + pw + d)
        return _out_proj(u, wo, xf)

    w_main, w_a1 = _inproj_weight(w_in_t, 0, a_col, rank)
    z, a1, (e_gate, f_gate, f_up, f_down) = _norm_inproj(
        xf, row(ln_mix[0]), w_main, w_a1, [(exp_w_gate, 0), (ffn_w_gate, 0), (ffn_w_up, 0), (ffn_w_down, 0)])
    mix0 = [_to_bf16(w, 0) for w in (w_pool, w_branch_gla, w_branch_pool, w_out)]
    xf = mix(xf, 0, z, a1, *mix0)
    xf, (e_down, *mix1) = _ffn(
        xf, row(ln_ffn[0]), f_gate, f_up, f_down,
        [(exp_w_down, 0), (w_pool, 1), (w_branch_gla, 1), (w_branch_pool, 1), (w_out, 1)])
    w_main, w_a1 = _inproj_weight(w_in_t, 1, a_col, rank)
    z, a1, (e_up,) = _norm_inproj(xf, row(ln_mix[1]), w_main, w_a1, [(exp_w_up, 0)])
    xf = mix(xf, 1, z, a1, *mix1)
    wr = jnp.pad(router_w[0], ((0, 0), (0, LANES - N_EXPERTS))).astype(BF16)
    xf = _moe_ffn_final(xf, row(ln_ffn[1]), wr, e_gate, e_up, e_down, ln_final.reshape(1, -1))
    return xf.reshape(batch, seq, d)
```

```python
import functools
import math

import jax
import jax.numpy as jnp
from jax import lax
from jax.experimental import pallas as pl
from jax.experimental.pallas import tpu as pltpu

F32 = jnp.float32
BF16 = jnp.bfloat16
I32 = jnp.int32

EPS = 1e-6
GLA_HEADS = 4
GLA_GATE_TEMP = 16.0
GLA_CHUNK = 64
GLA_HEADS_PER_STEP = 4
POOL_WINDOWS = (2, 4, 8, 16)
N_EXPERTS = 8

LANES = 128
VMEM_LIMIT_BYTES = 56 * 1024 * 1024

ROW_TILE = 1024
OUT_PROJ_ROW_TILE = 512
COL_TILE = 1024
FFN_ROW_TILE = 512
FFN_COL_TILE = 512
SEQ_TILE = 512
TOK_TILE = 256
PLAN_TILE = 512
HALO = 16


def _cparams(*sem):
    return pltpu.CompilerParams(dimension_semantics=sem, vmem_limit_bytes=VMEM_LIMIT_BYTES)


def _dot(a, b):
    return jnp.dot(a, b, preferred_element_type=F32)


def _dot_nt(a, b):
    return lax.dot_general(a, b, (((1,), (1,)), ((), ())), preferred_element_type=F32)


def _split_bf16(x):
    hi = x.astype(BF16)
    lo = (x - hi.astype(F32)).astype(BF16)
    return hi, lo


def _pick(n, target, align=LANES):
    if n <= target:
        return n
    best = None
    for cand in range(align, target + 1, align):
        if n % cand == 0:
            best = cand
    assert best is not None, (n, target, align)
    return best


def _cast_kernel(x_ref, o_ref):
    o_ref[...] = x_ref[...].astype(BF16)


def _to_bf16(w, layer):
    shape = w.shape[1:]
    w3 = w.reshape(w.shape[0], -1, shape[-1])
    _, r, c = w3.shape
    br, bc = _pick(r, 1024, 16), _pick(c, 2048)
    out = pl.pallas_call(
        _cast_kernel,
        grid=(r // br, c // bc),
        in_specs=[pl.BlockSpec((None, br, bc), lambda i, j: (layer, i, j))],
        out_specs=pl.BlockSpec((br, bc), lambda i, j: (i, j)),
        out_shape=jax.ShapeDtypeStruct((r, c), BF16),
        compiler_params=_cparams("parallel", "parallel"),
        name="to_bf16",
    )(w3)
    return out.reshape(shape)


def _inproj_weight_kernel(w_ref, nxt_ref, lr_ref, wm_ref, wa_ref, *, first_shifted, rank):
    j = pl.program_id(0)

    @pl.when(j < first_shifted)
    def _():
        wm_ref[...] = w_ref[...].astype(BF16)

    @pl.when(j >= first_shifted)
    def _():
        wm_ref[...] = jnp.concatenate([w_ref[rank:, :], nxt_ref[...]], axis=0).astype(BF16)

    @pl.when(j == 0)
    def _():
        pad = jnp.zeros((wa_ref.shape[0] - rank, wa_ref.shape[1]), F32)
        wa_ref[...] = jnp.concatenate([lr_ref[...], pad], axis=0).astype(BF16)


def _inproj_weight(w_t, layer, a_col, rank):
    _, n_in, d = w_t.shape
    n = n_in - rank
    br = _pick(math.gcd(n, a_col), 512)
    assert a_col % br == 0 and rank % 8 == 0 and rank < LANES and br % rank == 0
    per = br // rank
    return pl.pallas_call(
        functools.partial(_inproj_weight_kernel, first_shifted=a_col // br, rank=rank),
        grid=(n // br,),
        in_specs=[
            pl.BlockSpec((None, br, d), lambda j: (layer, j, 0)),
            pl.BlockSpec((None, rank, d), lambda j: (layer, (j + 1) * per, 0)),
            pl.BlockSpec((None, rank, d), lambda j: (layer, a_col // rank, 0)),
        ],
        out_specs=[
            pl.BlockSpec((br, d), lambda j: (j, 0)),
            pl.BlockSpec((LANES, d), lambda j: (0, 0)),
        ],
        out_shape=[jax.ShapeDtypeStruct((n, d), BF16), jax.ShapeDtypeStruct((LANES, d), BF16)],
        compiler_params=_cparams("arbitrary"),
        name="inproj_weight",
    )(w_t, w_t, w_t)


class _SidePlan:
    def __init__(self, sides, n_outer, n_inner):
        steps = n_outer * n_inner
        self.arrays, self.in_specs, self.out_specs, self.out_shapes, self.shapes, blocks = [], [], [], [], [], []
        for side, layer in sides:
            side3 = side.reshape(side.shape[0], -1, side.shape[-1])
            r, c = side3.shape[1:]
            rb = next(cand for cand in range(16, r + 1, 16) if r % cand == 0 and r // cand <= steps)
            nblk = r // rb
            idx = functools.partial(lambda i, j, last: jnp.minimum(i * n_inner + j, last), last=nblk - 1)
            self.arrays.append(side3)
            self.in_specs.append(pl.BlockSpec(
                (None, rb, c), functools.partial(lambda i, j, idx, layer: (layer, idx(i, j), 0), idx=idx, layer=layer)))
            self.out_specs.append(pl.BlockSpec((rb, c), functools.partial(lambda i, j, idx: (idx(i, j), 0), idx=idx)))
            self.out_shapes.append(jax.ShapeDtypeStruct((r, c), BF16))
            self.shapes.append(side.shape[1:])
            blocks.append(nblk)
        self.blocks = tuple(blocks)

    def finish(self, outs):
        return [o.reshape(s) for o, s in zip(outs, self.shapes)]


def _side_casts(blocks, side_refs, side_out_refs):
    step = pl.program_id(0) * pl.num_programs(1) + pl.program_id(1)
    for n_blocks, src, dst in zip(blocks, side_refs, side_out_refs):
        @pl.when(step < n_blocks)
        def _():
            dst[...] = src[...].astype(BF16)


def _norm_inproj_kernel(x_ref, g_ref, w_ref, wa_ref, *rest, rows, side_blocks):
    k = len(side_blocks)
    z_ref, a1_ref = rest[k:k + 2]
    h_ref = rest[2 * k + 2]
    _side_casts(side_blocks, rest[:k], rest[k + 2:2 * k + 2])

    @pl.when(pl.program_id(1) == 0)
    def _():
        def body(c, carry):
            r = pl.multiple_of(c * rows, rows)
            xf = x_ref[pl.ds(r, rows), :]
            ms = jnp.mean(xf * xf, axis=-1, keepdims=True)
            h_ref[pl.ds(r, rows), :] = (xf * lax.rsqrt(ms + EPS) * g_ref[...]).astype(BF16)
            return carry
        lax.fori_loop(0, x_ref.shape[0] // rows, body, 0)
        a1_ref[...] = _dot_nt(h_ref[...], wa_ref[...])

    z_ref[...] = _dot_nt(h_ref[...], w_ref[...]).astype(BF16)


def _norm_inproj(x, g, w_t, wa_t, sides):
    m, d = x.shape
    n = w_t.shape[0]
    bm, bn = min(ROW_TILE, m), _pick(n, COL_TILE)
    assert m % bm == 0 and n % bn == 0
    ni, nj = m // bm, n // bn
    plan = _SidePlan(sides, ni, nj)
    z, a1, *side_out = pl.pallas_call(
        functools.partial(_norm_inproj_kernel, rows=128, side_blocks=plan.blocks),
        grid=(ni, nj),
        in_specs=[
            pl.BlockSpec((bm, d), lambda i, j: (i, 0)),
            pl.BlockSpec((1, d), lambda i, j: (0, 0)),
            pl.BlockSpec((bn, d), lambda i, j: (j, 0)),
            pl.BlockSpec((LANES, d), lambda i, j: (0, 0)),
        ] + plan.in_specs,
        out_specs=[
            pl.BlockSpec((bm, bn), lambda i, j: (i, j)),
            pl.BlockSpec((bm, LANES), lambda i, j: (i, 0)),
        ] + plan.out_specs,
        out_shape=[jax.ShapeDtypeStruct((m, n), BF16), jax.ShapeDtypeStruct((m, LANES), F32)] + plan.out_shapes,
        scratch_shapes=[pltpu.VMEM((bm, d), BF16)],
        compiler_params=_cparams("arbitrary", "arbitrary"),
        name="norm_inproj",
    )(x, g, w_t, wa_t, *plan.arrays)
    return z, a1, plan.finish(side_out)


def _log_sigmoid(x):
    return jnp.minimum(x, 0.0) - jnp.log(1.0 + jnp.exp(-jnp.abs(x)))


def _gla_kernel(q_ref, k_ref, v_ref, r_ref, a1_ref, wal_ref, bal_ref, gn_ref, tri_ref, o_ref, st_ref, *,
                chunk, heads):
    @pl.when(pl.program_id(2) == 0)
    def _():
        st_ref[...] = jnp.zeros_like(st_ref)

    hk = q_ref.shape[1] // heads
    hv = v_ref.shape[1] // heads
    a1 = _split_bf16(a1_ref[...])
    for h in range(heads):
        ks, vs = slice(h * hk, (h + 1) * hk), slice(h * hv, (h + 1) * hv)
        o, st_new = _gla_head(q_ref[:, ks], k_ref[:, ks], v_ref[:, vs], r_ref[:, vs], a1, wal_ref[:, ks],
                              bal_ref[:, ks], gn_ref[...], tri_ref[...], st_ref[h], chunk)
        o_ref[:, vs] = o
        st_ref[h] = st_new


def _gla_head(q, k, v, r, a1, wal, bal, gn, tri, st, chunk):
    t, dk = q.shape
    nc = t // chunk
    pair = 2 * chunk
    shift = chunk.bit_length() - 1
    nt_dot = _dot_nt
    rows = lambda x, c0, c1: x[c0 * chunk:c1 * chunk, :]

    a_hi, a_lo = a1
    w_hi, w_lo = _split_bf16(wal)
    xg = _dot(a_hi, w_hi) + _dot(a_hi, w_lo) + _dot(a_lo, w_hi) + bal
    la = _log_sigmoid(xg) * (1.0 / GLA_GATE_TEMP)

    la_hi, la_lo = _split_bf16(la)
    b = _dot(tri, la_hi) + _dot(tri, la_lo)

    bl = [b[(c + 1) * chunk - 1:(c + 1) * chunk, :] for c in range(nc)]
    pre = [jnp.zeros_like(bl[0])]
    for c in range(nc):
        pre.append(pre[c] + bl[c])

    q = q.astype(F32)
    k = k.astype(F32)
    q_t = q * jnp.exp(b) * (dk ** -0.5)
    k_t = (k * jnp.exp(-b)).astype(BF16)
    qc = [rows(q_t, c, c + 1) for c in range(nc)]
    kec = [rows(k, c, c + 1) * jnp.exp(bl[c] - rows(b, c, c + 1)) for c in range(nc)]
    q_tb = q_t.astype(BF16)
    k_eb = jnp.concatenate(kec, axis=0).astype(BF16)

    def q_from(mid, hi):
        return jnp.concatenate([qc[c] if c == mid else qc[c] * jnp.exp(pre[c] - pre[mid])
                                for c in range(mid, hi)], axis=0).astype(BF16)

    def k_upto(lo, mid):
        return jnp.concatenate([kec[m] if m == mid - 1 else kec[m] * jnp.exp(pre[mid] - pre[m + 1])
                                for m in range(lo, mid)], axis=0).astype(BF16)

    nb = t // pair
    blocks = [[None] * nb for _ in range(nb)]
    r2 = lax.broadcasted_iota(I32, (pair, pair), 0)
    c2 = lax.broadcasted_iota(I32, (pair, pair), 1)
    same = ((r2 >> shift) == (c2 >> shift)) & (r2 >= c2)
    cross = (r2 >= chunk) & (c2 < chunk)
    for p in range(nb):
        qp = rows(q_tb, 2 * p, 2 * p + 2)
        in_chunk = nt_dot(qp, rows(k_t, 2 * p, 2 * p + 2))
        next_chunk = nt_dot(qp, rows(k_eb, 2 * p, 2 * p + 2))
        blocks[p][p] = jnp.where(same, in_chunk, jnp.where(cross, next_chunk, 0.0))
    g = 4
    while g <= nc:
        half = g // 2
        for grp in range(nc // g):
            lo, mid, hi = grp * g, grp * g + half, (grp + 1) * g
            x = nt_dot(q_from(mid, hi), k_upto(lo, mid))
            hb = half // 2
            for i in range(hb):
                for j in range(hb):
                    blocks[mid // 2 + i][lo // 2 + j] = x[i * pair:(i + 1) * pair, j * pair:(j + 1) * pair]
        g *= 2
    zero_blk = jnp.zeros((pair, pair), F32)
    att = jnp.concatenate(
        [jnp.concatenate([zero_blk if blk is None else blk for blk in brow], axis=1) for brow in blocks], axis=0)

    o = _dot(att.astype(BF16), v) + nt_dot(q_from(0, nc), st.astype(BF16))
    upd = lax.dot_general(v, k_upto(0, nc), (((0,), (0,)), ((), ())), preferred_element_type=F32)
    st_new = st * jnp.exp(pre[nc]) + upd

    o = o * lax.rsqrt(jnp.mean(o * o, axis=-1, keepdims=True) + EPS) * gn
    r = r.astype(F32)
    return (o * (r * jax.nn.sigmoid(r))).astype(BF16), st_new


def _gla(z, a1, w_alpha, b_alpha, gla_norm, batch, seq, dk_total, dv_total):
    m = z.shape[0]
    hk, hv = dk_total // GLA_HEADS, dv_total // GLA_HEADS
    t = min(SEQ_TILE, seq)
    assert seq % t == 0 and t % GLA_CHUNK == 0 and hk % LANES == 0 and hv % LANES == 0
    ns = seq // t
    nc = t // GLA_CHUNK
    assert nc >= 2 and nc & (nc - 1) == 0 and GLA_HEADS % GLA_HEADS_PER_STEP == 0
    hps = GLA_HEADS_PER_STEP
    gk, gv = hps * hk, hps * hv
    k_off = dk_total // gk
    v_off = (2 * dk_total) // gv
    r_off = (2 * dk_total + dv_total) // gv
    idx = jnp.arange(t, dtype=I32)
    tri = ((idx[:, None] // GLA_CHUNK == idx[None, :] // GLA_CHUNK) & (idx[:, None] >= idx[None, :])).astype(BF16)
    rowi = lambda b, h, s: b * ns + s
    return pl.pallas_call(
        functools.partial(_gla_kernel, chunk=GLA_CHUNK, heads=hps),
        grid=(batch, GLA_HEADS // hps, ns),
        in_specs=[
            pl.BlockSpec((t, gk), lambda b, h, s: (rowi(b, h, s), h)),
            pl.BlockSpec((t, gk), lambda b, h, s: (rowi(b, h, s), k_off + h)),
            pl.BlockSpec((t, gv), lambda b, h, s: (rowi(b, h, s), v_off + h)),
            pl.BlockSpec((t, gv), lambda b, h, s: (rowi(b, h, s), r_off + h)),
            pl.BlockSpec((t, LANES), lambda b, h, s: (rowi(b, h, s), 0)),
            pl.BlockSpec((LANES, gk), lambda b, h, s: (0, h)),
            pl.BlockSpec((1, gk), lambda b, h, s: (0, h)),
            pl.BlockSpec((1, hv), lambda b, h, s: (0, 0)),
            pl.BlockSpec((t, t), lambda b, h, s: (0, 0)),
        ],
        out_specs=pl.BlockSpec((t, gv), lambda b, h, s: (rowi(b, h, s), h)),
        out_shape=jax.ShapeDtypeStruct((m, dv_total), BF16),
        scratch_shapes=[pltpu.VMEM((hps, hv, hk), F32)],
        compiler_params=_cparams("parallel", "parallel", "arbitrary"),
        name="gla",
    )(z, z, z, z, a1, w_alpha, b_alpha, gla_norm, tri)


def _pool_kernel(p_ref, ph_ref, wp_ref, sc_ref, band_ref, bandh_ref, y_ref, *, windows):
    s = pl.program_id(1)
    t = p_ref.shape[0]
    gw = wp_ref.shape[1]
    pos = s * t + lax.broadcasted_iota(I32, (t, 1), 0)
    for g, w in enumerate(windows):
        cs = slice(g * gw, (g + 1) * gw)
        pc = p_ref[:, cs]
        tot = _dot(band_ref[g], pc) + jnp.where(s > 0, _dot(bandh_ref[g], ph_ref[:, cs]), 0.0)
        cnt = jnp.minimum(pos + 1, w).astype(F32)
        mixed = tot / cnt - pc.astype(F32)
        y = _dot(mixed.astype(BF16), wp_ref[g]) * sc_ref[:, cs]
        y_ref[:, cs] = y.astype(BF16)


def _pool(z, w_pool, pool_scale, batch, seq, p_col0):
    m = z.shape[0]
    groups, gw, _ = w_pool.shape
    pw = groups * gw
    t = min(SEQ_TILE, seq)
    assert seq % t == 0 and t % HALO == 0 and p_col0 % pw == 0 and gw % LANES == 0
    ns = seq // t
    pc = p_col0 // pw
    hb = t // HALO
    win = jnp.asarray(POOL_WINDOWS, I32)[:, None, None]
    dist = (jnp.arange(t, dtype=I32)[:, None] - jnp.arange(t, dtype=I32)[None, :])[None]
    dist_h = (jnp.arange(t, dtype=I32)[:, None] + HALO - jnp.arange(HALO, dtype=I32)[None, :])[None]
    band = ((dist >= 0) & (dist < win)).astype(BF16)
    band_h = (dist_h < win).astype(BF16)
    return pl.pallas_call(
        functools.partial(_pool_kernel, windows=POOL_WINDOWS),
        grid=(batch, ns),
        in_specs=[
            pl.BlockSpec((t, pw), lambda b, s: (b * ns + s, pc)),
            pl.BlockSpec((HALO, pw), lambda b, s: (jnp.maximum((b * ns + s) * hb - 1, 0), pc)),
            pl.BlockSpec((groups, gw, gw), lambda b, s: (0, 0, 0)),
            pl.BlockSpec((1, pw), lambda b, s: (0, 0)),
            pl.BlockSpec((groups, t, t), lambda b, s: (0, 0, 0)),
            pl.BlockSpec((groups, t, HALO), lambda b, s: (0, 0, 0)),
        ],
        out_specs=pl.BlockSpec((t, pw), lambda b, s: (b * ns + s, 0)),
        out_shape=jax.ShapeDtypeStruct((m, pw), BF16),
        compiler_params=_cparams("parallel", "arbitrary"),
        name="pool",
    )(z, z, w_pool, pool_scale, band, band_h)


def _merge_kernel(og_ref, yp_ref, ga_ref, gb_ref, wa_ref, wb_ref, u_ref):
    ya = _dot(og_ref[...], wa_ref[...])
    yb = _dot(yp_ref[...], wb_ref[...])
    ga = jax.nn.sigmoid(ga_ref[...].astype(F32))
    gb = jax.nn.sigmoid(gb_ref[...].astype(F32))
    u_ref[...] = (ga * ya + gb * yb).astype(BF16)


def _merge(og, yp, z, w_a, w_b, ga_col0, gb_col0):
    m, dv = og.shape
    pw = yp.shape[1]
    d = w_a.shape[1]
    bm, bn = min(ROW_TILE, m), _pick(math.gcd(d, ga_col0, gb_col0), COL_TILE)
    assert m % bm == 0 and d % bn == 0 and ga_col0 % bn == 0 and gb_col0 % bn == 0
    ga0, gb0 = ga_col0 // bn, gb_col0 // bn
    return pl.pallas_call(
        _merge_kernel,
        grid=(m // bm, d // bn),
        in_specs=[
            pl.BlockSpec((bm, dv), lambda i, j: (i, 0)),
            pl.BlockSpec((bm, pw), lambda i, j: (i, 0)),
            pl.BlockSpec((bm, bn), lambda i, j: (i, ga0 + j)),
            pl.BlockSpec((bm, bn), lambda i, j: (i, gb0 + j)),
            pl.BlockSpec((dv, bn), lambda i, j: (0, j)),
            pl.BlockSpec((pw, bn), lambda i, j: (0, j)),
        ],
        out_specs=pl.BlockSpec((bm, bn), lambda i, j: (i, j)),
        out_shape=jax.ShapeDtypeStruct((m, d), BF16),
        compiler_params=_cparams("parallel", "arbitrary"),
        name="merge",
    )(og, yp, z, z, w_a, w_b)


def _out_proj_kernel(u_ref, w_ref, x_ref, o_ref):
    o_ref[...] = x_ref[...] + _dot(u_ref[...], w_ref[...])


def _out_proj(u, w, x):
    m, d = x.shape
    bm, bn = min(OUT_PROJ_ROW_TILE, m), _pick(d, 2 * COL_TILE)
    assert m % bm == 0 and d % bn == 0
    return pl.pallas_call(
        _out_proj_kernel,
        grid=(m // bm, d // bn),
        in_specs=[
            pl.BlockSpec((bm, u.shape[1]), lambda i, j: (i, 0)),
            pl.BlockSpec((u.shape[1], bn), lambda i, j: (0, j)),
            pl.BlockSpec((bm, bn), lambda i, j: (i, j)),
        ],
        out_specs=pl.BlockSpec((bm, bn), lambda i, j: (i, j)),
        out_shape=jax.ShapeDtypeStruct((m, d), F32),
        compiler_params=_cparams("parallel", "arbitrary"),
        name="out_proj",
    )(u, w, x)


def _rmsnorm_rows(x_ref, g_ref, h_ref, rows):
    def body(c, carry):
        r = pl.multiple_of(c * rows, rows)
        xf = x_ref[pl.ds(r, rows), :]
        ms = jnp.mean(xf * xf, axis=-1, keepdims=True)
        h_ref[pl.ds(r, rows), :] = (xf * lax.rsqrt(ms + EPS) * g_ref[...]).astype(BF16)
        return carry
    lax.fori_loop(0, x_ref.shape[0] // rows, body, 0)


def _swiglu_block(h, wg, wu, wd):
    g = _dot(h, wg)
    u = _dot(h, wu)
    a = (g * jax.nn.sigmoid(g) * u).astype(BF16)
    return _dot(a, wd)


def _ffn_kernel(x_ref, g_ref, wg_ref, wu_ref, wd_ref, *rest, side_blocks):
    f = pl.program_id(1)
    k = len(side_blocks)
    o_ref = rest[k]
    h_ref = rest[2 * k + 1]
    _side_casts(side_blocks, rest[:k], rest[k + 1:2 * k + 1])

    @pl.when(f == 0)
    def _():
        _rmsnorm_rows(x_ref, g_ref, h_ref, 128)
        o_ref[...] = x_ref[...]

    o_ref[...] += _swiglu_block(h_ref[...], wg_ref[...], wu_ref[...], wd_ref[...])


def _ffn(x, g, wg, wu, wd, sides):
    m, d = x.shape
    ff = wg.shape[1]
    bm, bf = min(FFN_ROW_TILE, m), min(FFN_COL_TILE, ff)
    assert m % bm == 0 and ff % bf == 0
    ni, nf = m // bm, ff // bf
    plan = _SidePlan(sides, ni, nf)
    out, *side_out = pl.pallas_call(
        functools.partial(_ffn_kernel, side_blocks=plan.blocks),
        grid=(ni, nf),
        in_specs=[
            pl.BlockSpec((bm, d), lambda i, f: (i, 0)),
            pl.BlockSpec((1, d), lambda i, f: (0, 0)),
            pl.BlockSpec((d, bf), lambda i, f: (0, f)),
            pl.BlockSpec((d, bf), lambda i, f: (0, f)),
            pl.BlockSpec((bf, d), lambda i, f: (f, 0)),
        ] + plan.in_specs,
        out_specs=[pl.BlockSpec((bm, d), lambda i, f: (i, 0))] + plan.out_specs,
        out_shape=[jax.ShapeDtypeStruct((m, d), F32)] + plan.out_shapes,
        scratch_shapes=[pltpu.VMEM((bm, d), BF16)],
        compiler_params=_cparams("arbitrary", "arbitrary"),
        name="ffn",
    )(x, g, wg, wu, wd, *plan.arrays)
    return out, plan.finish(side_out)


def _to_slabs(slab_ref, x):
    rows, d = x.shape
    slab_ref[...] = x.astype(BF16).reshape(rows * (d // LANES), LANES)


def _from_slabs(slab_ref, rows, d):
    return slab_ref[...].reshape(rows, d)


def _router_kernel(x_ref, g_ref, wr_ref, hs_ref, route_ref, hb_ref):
    _rmsnorm_rows(x_ref, g_ref, hb_ref, 128)
    hb = hb_ref[...]
    _to_slabs(hs_ref, hb)
    logits = _dot(hb, wr_ref[...])
    lane = lax.broadcasted_iota(I32, logits.shape, 1)
    lane_f = lane.astype(F32)
    neg = jnp.float32(-jnp.inf)
    l1 = jnp.where(lane < N_EXPERTS, logits, neg)
    m1 = jnp.max(l1, axis=-1, keepdims=True)
    i1 = jnp.min(jnp.where(l1 == m1, lane_f, float(LANES)), axis=-1, keepdims=True)
    l2 = jnp.where(lane_f == i1, neg, l1)
    m2 = jnp.max(l2, axis=-1, keepdims=True)
    i2 = jnp.min(jnp.where(l2 == m2, lane_f, float(LANES)), axis=-1, keepdims=True)
    e = jnp.exp(m2 - m1)
    den = 1.0 + e
    route = jnp.where(lane == 0, i1, 0.0)
    route = jnp.where(lane == 1, i2, route)
    route = jnp.where(lane == 2, 1.0 / den, route)
    route = jnp.where(lane == 3, e / den, route)
    route_ref[...] = route


def _router(x, g, wr):
    m, d = x.shape
    bm = min(FFN_ROW_TILE, m)
    sl = d // LANES
    assert m % bm == 0 and sl % 8 == 0
    return pl.pallas_call(
        _router_kernel,
        grid=(m // bm,),
        in_specs=[
            pl.BlockSpec((bm, d), lambda i: (i, 0)),
            pl.BlockSpec((1, d), lambda i: (0, 0)),
            pl.BlockSpec((d, LANES), lambda i: (0, 0)),
        ],
        out_specs=[
            pl.BlockSpec((bm * sl, LANES), lambda i: (i, 0)),
            pl.BlockSpec((bm, LANES), lambda i: (i, 0)),
        ],
        out_shape=[jax.ShapeDtypeStruct((m * sl, LANES), BF16), jax.ShapeDtypeStruct((m, LANES), F32)],
        scratch_shapes=[pltpu.VMEM((bm, d), BF16)],
        compiler_params=_cparams("parallel"),
        name="router",
    )(x, g, wr)


def _route_plan_kernel(route_ref, pos_ref, meta_ref, rank_ref, *, tile, blk):
    m = route_ref.shape[0]
    nb = m // blk
    lane_b = lax.broadcasted_iota(I32, (blk, LANES), 1)
    lane_r = lax.broadcasted_iota(I32, (1, LANES), 1)
    ne = N_EXPERTS

    def onehot(r):
        rt = route_ref[pl.ds(r, blk), :]
        i1 = rt[:, 0:1].astype(I32)
        i2 = rt[:, 1:2].astype(I32)
        return jnp.where((lane_b == i1) | (lane_b == i2 + ne), 1.0, 0.0)

    strict = jnp.where(lax.broadcasted_iota(I32, (blk, blk), 0) > lax.broadcasted_iota(I32, (blk, blk), 1),
                       1.0, 0.0).astype(BF16)

    def rank_body(c, carry):
        r = pl.multiple_of(c * blk, blk)
        oh = onehot(r)
        rank_ref[pl.ds(r, blk), :] = _dot(strict, oh.astype(BF16)) + carry
        return carry + jnp.sum(oh, axis=0, keepdims=True)

    cnt12 = lax.fori_loop(0, nb, rank_body, jnp.zeros((1, LANES), F32))

    def lane_val(row, l):
        return jnp.sum(jnp.where(lane_r == l, row, 0.0), axis=-1, keepdims=True)

    cnt = jnp.zeros((1, LANES), F32)
    for e in range(ne):
        cnt = jnp.where(lane_r == e, lane_val(cnt12, e) + lane_val(cnt12, e + ne), cnt)
    tiles = jnp.floor((cnt + (tile - 1)) * (1.0 / tile))
    start = jnp.zeros((1, LANES), F32)
    for e in range(1, ne):
        start = start + jnp.where(lane_r >= e, lane_val(tiles, e - 1), 0.0)
    start = jnp.where(lane_r < ne, start * tile, 0.0)
    end = start + tiles * tile
    base = start
    for e in range(ne):
        base = jnp.where(lane_r == e + ne, lane_val(start, e) + lane_val(cnt12, e), base)

    def pos_body(c, carry):
        r = pl.multiple_of(c * blk, blk)
        slot = onehot(r) * (rank_ref[pl.ds(r, blk), :] + base)
        p1 = jnp.sum(jnp.where(lane_b < ne, slot, 0.0), axis=-1, keepdims=True)
        p2 = jnp.sum(jnp.where(lane_b >= ne, slot, 0.0), axis=-1, keepdims=True)
        both = jnp.where(lane_b == 0, p1, jnp.where(lane_b == 1, p2, 0.0))
        pos_ref[:, pl.ds(r, blk)] = jnp.transpose(both)[0:8, :].astype(I32)
        return carry

    lax.fori_loop(0, nb, pos_body, 0)

    n_used = lane_val(end, ne - 1) * (1.0 / tile)
    tile_start = lane_r.astype(F32) * tile
    tile_e = jnp.zeros((1, LANES), F32)
    last_e = jnp.zeros((1, 1), F32)
    for e in range(ne):
        tile_e = tile_e + jnp.where(lane_val(end, e) <= tile_start, 1.0, 0.0)
        last_e = jnp.where(lane_val(cnt, e) > 0, float(e), last_e)
    tile_e = jnp.where(lane_r.astype(F32) < n_used, tile_e, last_e)
    info = jnp.where(lane_r < ne, end, jnp.where(lane_r == 2 * ne, n_used, 0.0))
    for e in range(ne):
        info = jnp.where(lane_r == e + ne, lane_val(cnt, e), info)
    row8 = lax.broadcasted_iota(I32, (8, LANES), 0)
    meta = jnp.where(row8 == 0, tile_e, jnp.where(row8 == 1, info, jnp.where(row8 == 2, n_used, 0.0)))
    meta_ref[...] = meta.astype(I32)


def _route_plan(route, tile, n_tiles):
    m = route.shape[0]
    blk = min(PLAN_TILE, m)
    assert m % blk == 0 and n_tiles <= LANES
    return pl.pallas_call(
        functools.partial(_route_plan_kernel, tile=tile, blk=blk),
        out_shape=[jax.ShapeDtypeStruct((8, m), I32), jax.ShapeDtypeStruct((8, LANES), I32)],
        scratch_shapes=[pltpu.VMEM((m, LANES), F32)],
        compiler_params=pltpu.CompilerParams(vmem_limit_bytes=VMEM_LIMIT_BYTES),
        name="route_plan",
    )(route)


def _dispatch_kernel(meta_ref, pos_ref, h_ref, xs_ref, zero_ref, sem, *, tile, sl):
    i = pl.program_id(0)
    tt = h_ref.shape[0] // sl
    n_tiles = xs_ref.shape[0] // (tile * sl)

    def zero_copy(e):
        first = pl.multiple_of((meta_ref[e] - tile) * sl, tile * sl)
        return pltpu.make_async_copy(zero_ref, xs_ref.at[pl.ds(first, tile * sl), :], sem.at[1])

    def tail_copy(j):
        return pltpu.make_async_copy(zero_ref, xs_ref.at[pl.ds(j * tile * sl, tile * sl), :], sem.at[1])

    @pl.when(i == 0)
    def _():
        zero_ref[...] = jnp.zeros_like(zero_ref)
        n_used = meta_ref[2 * N_EXPERTS]
        for e in range(N_EXPERTS):
            @pl.when(meta_ref[N_EXPERTS + e] > 0)
            def _():
                zero_copy(e).start()
        for j in range(n_tiles - N_EXPERTS, n_tiles):
            @pl.when(j >= n_used)
            def _():
                tail_copy(j).start()
        for e in range(N_EXPERTS):
            @pl.when(meta_ref[N_EXPERTS + e] > 0)
            def _():
                zero_copy(e).wait()
        for j in range(n_tiles - N_EXPERTS, n_tiles):
            @pl.when(j >= n_used)
            def _():
                tail_copy(j).wait()

    def row_copy(r, c):
        src = pl.multiple_of(r * sl, sl)
        dst = pl.multiple_of(pos_ref[0, c * tt + r] * sl, sl)
        return pltpu.make_async_copy(h_ref.at[pl.ds(src, sl), :], xs_ref.at[pl.ds(dst, sl), :], sem.at[0])

    def start_body(r, carry):
        row_copy(r, 0).start(priority=0)
        row_copy(r, 1).start(priority=1)
        return carry

    lax.fori_loop(0, tt, start_body, 0, unroll=8)
    for _ in range(2):
        pltpu.make_async_copy(h_ref, xs_ref.at[pl.ds(0, tt * sl), :], sem.at[0]).wait()


def _dispatch(hs, pos_tiles, ginfo, n_slots, tile, sl):
    m = hs.shape[0] // sl
    tt = pos_tiles.shape[2] // 2
    return pl.pallas_call(
        functools.partial(_dispatch_kernel, tile=tile, sl=sl),
        grid_spec=pltpu.PrefetchScalarGridSpec(
            num_scalar_prefetch=1,
            grid=(m // tt,),
            in_specs=[
                pl.BlockSpec((None, 1, 2 * tt), lambda i, meta: (i, 0, 0), memory_space=pltpu.SMEM),
                pl.BlockSpec((tt * sl, LANES), lambda i, meta: (i, 0)),
            ],
            out_specs=pl.BlockSpec(memory_space=pl.ANY),
            scratch_shapes=[pltpu.VMEM((tile * sl, LANES), BF16), pltpu.SemaphoreType.DMA((2,))],
        ),
        out_shape=jax.ShapeDtypeStruct((n_slots * sl, LANES), BF16),
        compiler_params=_cparams("arbitrary"),
        name="dispatch",
    )(ginfo, pos_tiles, hs)


def _expert_ffn_kernel(te_ref, nu_ref, xs_ref, wg_ref, wu_ref, wd_ref, y_ref, h_ref, acc_ref):
    i, f = pl.program_id(0), pl.program_id(1)
    used = i < nu_ref[0]
    last = pl.num_programs(1) - 1
    tile, d = h_ref.shape

    @pl.when(used & (f == 0))
    def _():
        h_ref[...] = _from_slabs(xs_ref, tile, d).astype(BF16)
        acc_ref[...] = jnp.zeros_like(acc_ref)

    @pl.when(used)
    def _():
        acc_ref[...] += _swiglu_block(h_ref[...], wg_ref[...], wu_ref[...], wd_ref[...])

    @pl.when(used & (f == last))
    def _():
        _to_slabs(y_ref, acc_ref[...])

    @pl.when(jnp.logical_not(used) & (f == last))
    def _():
        y_ref[...] = jnp.zeros_like(y_ref)


def _expert_ffn(xs, tile_e, n_used, wg, wu, wd, tile, sl):
    n_slots = xs.shape[0] // sl
    d = sl * LANES
    ff = wg.shape[2]
    bf = min(FFN_COL_TILE, ff)
    assert n_slots % tile == 0 and ff % bf == 0
    nf = ff // bf
    row = lambda i, f, te, nu: (jnp.minimum(i, nu[0] - 1), 0)
    fblk = lambda i, f, nu: jnp.where(i < nu[0], f, nf - 1)
    return pl.pallas_call(
        _expert_ffn_kernel,
        grid_spec=pltpu.PrefetchScalarGridSpec(
            num_scalar_prefetch=2,
            grid=(n_slots // tile, nf),
            in_specs=[
                pl.BlockSpec((tile * sl, LANES), row),
                pl.BlockSpec((None, d, bf), lambda i, f, te, nu: (te[i], 0, fblk(i, f, nu))),
                pl.BlockSpec((None, d, bf), lambda i, f, te, nu: (te[i], 0, fblk(i, f, nu))),
                pl.BlockSpec((None, bf, d), lambda i, f, te, nu: (te[i], fblk(i, f, nu), 0)),
            ],
            out_specs=pl.BlockSpec((tile * sl, LANES), lambda i, f, te, nu: (i, 0)),
            scratch_shapes=[pltpu.VMEM((tile, d), BF16), pltpu.VMEM((tile, d), F32)],
        ),
        out_shape=jax.ShapeDtypeStruct((n_slots * sl, LANES), BF16),
        compiler_params=_cparams("parallel", "arbitrary"),
        name="expert_ffn",
    )(tile_e, n_used, xs, wg, wu, wd)


def _combine_kernel(pos_ref, nxt_ref, x_ref, route_ref, g_ref, y_ref, o_ref, buf_ref, sem):
    i = pl.program_id(0)
    n = pl.num_programs(0)
    tt, d = x_ref.shape
    sl = d // LANES
    slot = lax.rem(i, 2)

    def gather(p_ref, dst_slot):
        def row_copy(r, c):
            src = pl.multiple_of(p_ref[0, c * tt + r] * sl, sl)
            dst = pl.multiple_of(r * sl, sl)
            return pltpu.make_async_copy(y_ref.at[pl.ds(src, sl), :], buf_ref.at[dst_slot, c, pl.ds(dst, sl), :],
                                         sem.at[dst_slot])

        def start_body(r, carry):
            row_copy(r, 0).start()
            row_copy(r, 1).start()
            return carry

        lax.fori_loop(0, tt, start_body, 0, unroll=8)

    @pl.when(i == 0)
    def _():
        gather(pos_ref, 0)

    @pl.when(i + 1 < n)
    def _():
        gather(nxt_ref, 1 - slot)

    for c in range(2):
        pltpu.make_async_copy(y_ref.at[pl.ds(0, tt * sl), :], buf_ref.at[slot, c], sem.at[slot]).wait()

    rt = route_ref[...]
    moe = (rt[:, 2:3] * _from_slabs(buf_ref.at[slot, 0], tt, d)
           + rt[:, 3:4] * _from_slabs(buf_ref.at[slot, 1], tt, d))
    xo = x_ref[...] + moe
    ms = jnp.mean(xo * xo, axis=-1, keepdims=True)
    o_ref[...] = xo * lax.rsqrt(ms + EPS) * g_ref[...]


def _combine(x, route, g, y, pos_tiles):
    m, d = x.shape
    tt = pos_tiles.shape[2] // 2
    nt = m // tt
    return pl.pallas_call(
        _combine_kernel,
        grid=(nt,),
        in_specs=[
            pl.BlockSpec((None, 1, 2 * tt), lambda i: (i, 0, 0), memory_space=pltpu.SMEM),
            pl.BlockSpec((None, 1, 2 * tt), lambda i: (jnp.minimum(i + 1, nt - 1), 0, 0), memory_space=pltpu.SMEM),
            pl.BlockSpec((tt, d), lambda i: (i, 0)),
            pl.BlockSpec((tt, LANES), lambda i: (i, 0)),
            pl.BlockSpec((1, d), lambda i: (0, 0)),
            pl.BlockSpec(memory_space=pl.ANY),
        ],
        out_specs=pl.BlockSpec((tt, d), lambda i: (i, 0)),
        out_shape=jax.ShapeDtypeStruct((m, d), F32),
        scratch_shapes=[pltpu.VMEM((2, 2, tt * (d // LANES), LANES), BF16), pltpu.SemaphoreType.DMA((2,))],
        compiler_params=_cparams("arbitrary"),
        name="combine",
    )(pos_tiles, pos_tiles, x, route, g, y)


def _moe_ffn_final(x, g_ffn, wr, wg, wu, wd, g_final):
    m, d = x.shape
    tile = min(FFN_ROW_TILE, m)
    n_tiles = (2 * m) // tile + N_EXPERTS
    tt = min(TOK_TILE, m)
    sl = d // LANES
    hs, route = _router(x, g_ffn, wr)
    pos, meta = _route_plan(route, tile, n_tiles)
    pos_tiles = pos[:2].reshape(2, m // tt, tt).transpose(1, 0, 2).reshape(m // tt, 1, 2 * tt)
    xs = _dispatch(hs, pos_tiles, meta[1, :2 * N_EXPERTS + 1], n_tiles * tile, tile, sl)
    y = _expert_ffn(xs, meta[0, :n_tiles], meta[2, :1], wg, wu, wd, tile, sl)
    return _combine(x, route, g_final, y, pos_tiles)


def kernel(x, ln_mix, w_in, w_alpha, b_alpha, gla_norm, w_pool, pool_scale, w_branch_gla, w_branch_pool, w_out,
           ln_ffn, ffn_w_gate, ffn_w_up, ffn_w_down, router_w, exp_w_gate, exp_w_up, exp_w_down, ln_final):
    batch, seq, d = x.shape
    depth = w_in.shape[0]
    dk = w_alpha.shape[2]
    dv = w_branch_gla.shape[1]
    pw = w_branch_pool.shape[1]
    rank = w_alpha.shape[1]
    assert depth == 2 and rank <= LANES
    a_col = 2 * dk + 2 * dv
    m = batch * seq
    xf = x.reshape(m, d)
    row = lambda v: v.reshape(1, -1)
    w_in_t = jnp.swapaxes(w_in, 1, 2)

    assert exp_w_gate.shape[0] == 1 and ffn_w_gate.shape[0] == 1

    def mix(xf, l, z, a1, wp, wbg, wbp, wo):
        w_al = jnp.pad(w_alpha[l], ((0, LANES - rank), (0, 0)))
        og = _gla(z, a1, w_al, row(b_alpha[l]), row(gla_norm[l]), batch, seq, dk, dv)
        yp = _pool(z, wp, row(pool_scale[l]), batch, seq, a_col)
        u = _merge(og, yp, z, wbg, wbp, a_col + pw, a_col + pw + d)
        return _out_proj(u, wo, xf)

    w_main, w_a1 = _inproj_weight(w_in_t, 0, a_col, rank)
    z, a1, (e_gate, f_gate, f_up, f_down) = _norm_inproj(
        xf, row(ln_mix[0]), w_main, w_a1, [(exp_w_gate, 0), (ffn_w_gate, 0), (ffn_w_up, 0), (ffn_w_down, 0)])
    mix0 = [_to_bf16(w, 0) for w in (w_pool, w_branch_gla, w_branch_pool, w_out)]
    xf = mix(xf, 0, z, a1, *mix0)
    xf, (e_down, *mix1) = _ffn(
        xf, row(ln_ffn[0]), f_gate, f_up, f_down,
        [(exp_w_down, 0), (w_pool, 1), (w_branch_gla, 1), (w_branch_pool, 1), (w_out, 1)])
    w_main, w_a1 = _inproj_weight(w_in_t, 1, a_col, rank)
    z, a1, (e_up,) = _norm_inproj(xf, row(ln_mix[1]), w_main, w_a1, [(exp_w_up, 0)])
    xf = mix(xf, 1, z, a1, *mix1)
    wr = jnp.pad(router_w[0], ((0, 0), (0, LANES - N_EXPERTS))).astype(BF16)
    xf = _moe_ffn_final(xf, row(ln_ffn[1]), wr, e_gate, e_up, e_down, ln_final.reshape(1, -1))
    return xf.reshape(batch, seq, d)
```
